```python
import jax, jax.numpy as jnp
from jax import lax
import numpy as np

D_MODEL = 1024
BATCH = 4
SEQ = 4096
DEPTH = 4
DEC_BATCH = 32
DEC_SEQ = 8
PAST_LEN = 8192
PAGE_SIZE = 128

N_A_LAYERS = DEPTH // 2
N_B_LAYERS = DEPTH - N_A_LAYERS
RWKV_HEAD = 64
RWKV_HEADS = D_MODEL // RWKV_HEAD
DECAY_LORA = 64
AAA_LORA = 64
MV_LORA = 32
GATE_LORA = 128
GN_EPS = 64e-5
ATT_HEAD = 64
ATT_HEADS = D_MODEL // ATT_HEAD
ATT_WIDTH = ATT_HEADS * ATT_HEAD
GROUPS = ((128, 1), (512, 4), (2048, 16))
N_GROUPS = len(GROUPS)
MAX_WINDOW = 2048
N_EXPERTS = 32
TOP_K = 4
D_EXPERT = D_MODEL
SWIGLU_LIMIT = 7.0
SWIGLU_ALPHA = 1.702
MOE_BLOCK = 128
LN_EPS = 1e-5
DN_ALPHA = (2 * DEPTH) ** 0.25
DN_BETA = (8 * DEPTH) ** -0.25

kernel_name = 'yoco_rwkv7_dilated_alibi_moe_step'


def layer_norm(x, g, b):
    xf = x.astype(jnp.float32)
    mu = jnp.mean(xf, axis=-1, keepdims=True)
    var = jnp.mean(jnp.square(xf - mu), axis=-1, keepdims=True)
    return ((xf - mu) * lax.rsqrt(var + LN_EPS) * g + b).astype(x.dtype)


def alibi_slopes(n_heads):
    return 2.0 ** (-8.0 * jnp.arange(1, n_heads + 1, dtype=jnp.float32) / n_heads)


def rwkv7_time_mix(x, shift0, wkv0, v_first, vres, mu, w_r, w_k, w_v, w_o,
                   w0, w1, w2, a0, a1, a2, g1, g2, k_k, k_a, r_k, gn_g, gn_b):
    B, T, D = x.shape
    H, N = RWKV_HEADS, RWKV_HEAD
    f32 = jnp.float32
    x_prev = jnp.concatenate([shift0[:, None, :].astype(x.dtype), x[:, :-1]], axis=1)
    xx = x_prev - x
    xr, xw, xk, xv, xa, xg = [x + xx * mu[i] for i in range(6)]
    r = xr @ w_r
    k = xk @ w_k
    v = xv @ w_v
    log_w = -jax.nn.softplus(-(w0 + jnp.tanh(xw @ w1) @ w2)) - 0.5
    if vres is None:
        v_first = v
    else:
        v0, v1, v2 = vres
        v = v + (v_first - v) * jax.nn.sigmoid(v0 + (xv @ v1) @ v2)
    a = jax.nn.sigmoid(a0 + (xa @ a1) @ a2)
    g = jax.nn.sigmoid(xg @ g1) @ g2
    heads = lambda t: t.reshape(B, T, H, N).astype(f32)
    kk = heads(k * k_k)
    kk = kk / jnp.maximum(jnp.sqrt(jnp.sum(kk * kk, axis=-1, keepdims=True)), 1e-12)
    k = k * (1.0 + (a - 1.0) * k_a)
    rh, kh, vh, ah = heads(r), heads(k), heads(v), heads(a)
    decay = jnp.exp(-jnp.exp(heads(log_w)))

    def step(S, inp):
        r_t, w_t, k_t, v_t, kk_t, a_t = inp
        sk = jnp.einsum('bhij,bhj->bhi', S, kk_t)
        S = (S * w_t[:, :, None, :] - sk[..., None] * (kk_t * a_t)[:, :, None, :]
             + v_t[..., None] * k_t[:, :, None, :])
        return S, jnp.einsum('bhij,bhj->bhi', S, r_t)

    seq = tuple(jnp.swapaxes(t, 0, 1) for t in (rh, decay, kh, vh, kk, ah))
    S, o = lax.scan(step, wkv0.astype(f32), seq)
    o = jnp.swapaxes(o, 0, 1)
    mean = jnp.mean(o, axis=-1, keepdims=True)
    var = jnp.mean(jnp.square(o - mean), axis=-1, keepdims=True)
    o = ((o - mean) * lax.rsqrt(var + GN_EPS)).reshape(B, T, D) * gn_g + gn_b
    o = o + (jnp.sum(rh * kh * r_k, axis=-1, keepdims=True) * vh).reshape(B, T, D)
    y = (o.astype(x.dtype) * g) @ w_o
    return y, v_first, S.astype(wkv0.dtype), x[:, -1]


def dilated_attn_prompt(q, k, v, slopes, window, dil):
    B, S, H, E = q.shape
    nk = window // dil
    L = S // dil
    nb = -(-L // nk)
    Lp = nb * nk

    def sub(t):
        t = t.reshape(B, L, dil, H, E).transpose(0, 2, 1, 3, 4).reshape(B * dil, L, H, E)
        return jnp.pad(t, ((0, 0), (0, Lp - L), (0, 0), (0, 0))).reshape(B * dil, nb, nk, H, E)

    def with_prev(t):
        prev = jnp.pad(t[:, :-1], ((0, 0), (1, 0), (0, 0), (0, 0), (0, 0)))
        return jnp.concatenate([prev, t], axis=2)

    qb = sub(q)
    kc, vc = with_prev(sub(k)), with_prev(sub(v))
    s = jnp.einsum('znqhe,znkhe->znhqk', qb, kc, preferred_element_type=jnp.float32) * ATT_HEAD ** -0.5
    qi = jnp.arange(nk)[:, None] + nk
    kj = jnp.arange(2 * nk)[None, :]
    delta = qi - kj
    key_sub = jnp.arange(nb)[:, None, None] * nk - nk + kj[None]
    valid = (delta >= 0)[None] & (delta <= nk)[None] & (key_sub >= 0)
    bias = -slopes[:, None, None] * (delta * dil).astype(jnp.float32)[None]
    s = jnp.where(valid[None, :, None], s + bias[None, None], -jnp.inf)
    lse = jax.nn.logsumexp(s, axis=-1)
    p = jnp.exp(s - lse[..., None])
    o = jnp.einsum('znhqk,znkhe->znqhe', p.astype(v.dtype), vc)
    o = o.reshape(B * dil, Lp, H, E)[:, :L].reshape(B, dil, L, H, E).transpose(0, 2, 1, 3, 4).reshape(B, S, H, E)
    lse = lse.transpose(0, 1, 3, 2).reshape(B * dil, Lp, H)[:, :L]
    lse = lse.reshape(B, dil, L, H).transpose(0, 2, 1, 3).reshape(B, S, H)
    return o, lse


def dilated_attn_sample(q, k_all, v_all, slopes, window, dil):
    B, T, H, E = q.shape
    KB = k_all.shape[1] - T
    nk = window // dil
    steps = jnp.arange(nk + 1)
    idx = KB + jnp.arange(T)[:, None] - steps[None, :] * dil
    valid = idx >= 0
    idx_c = jnp.maximum(idx, 0)
    kg = k_all[:, idx_c]
    vg = v_all[:, idx_c]
    dist = (steps * dil).astype(jnp.float32)
    s = jnp.einsum('bthe,btjhe->bthj', q, kg, preferred_element_type=jnp.float32) * ATT_HEAD ** -0.5
    s = jnp.where(valid[None, :, None, :], s - slopes[:, None] * dist[None, :], -jnp.inf)
    lse = jax.nn.logsumexp(s, axis=-1)
    p = jnp.exp(s - lse[..., None])
    o = jnp.einsum('bthj,btjhe->bthe', p.astype(v_all.dtype), vg)
    return o, lse


def dilated_mixer(x, k_all, v_all, w_q, w_o, slopes, attn_fn):
    B, T, _ = x.shape
    q = (x @ w_q).reshape(B, T, N_GROUPS, ATT_HEADS, ATT_HEAD)
    outs, lses = [], []
    for gi, (window, dil) in enumerate(GROUPS):
        o, lse = attn_fn(q[:, :, gi], k_all, v_all, slopes, window, dil)
        outs.append(o.astype(jnp.float32))
        lses.append(lse)
    wts = jax.nn.softmax(jnp.stack(lses), axis=0)
    o = jnp.sum(wts[..., None] * jnp.stack(outs), axis=0)
    return o.astype(x.dtype).reshape(B, T, ATT_WIDTH) @ w_o


def moe_ffn(x, w_router, b_router, w_in, b_in, w_out, b_out):
    B, T, D = x.shape
    xt = x.reshape(-1, D)
    M = xt.shape[0]
    logits = (xt @ w_router + b_router).astype(jnp.float32)
    top_logit, top_e = lax.top_k(logits, TOP_K)
    gate = jax.nn.softmax(top_logit, axis=-1)
    MK = M * TOP_K
    flat_e = top_e.reshape(-1)
    order = jnp.argsort(flat_e)
    e_sorted = flat_e[order]
    tok_sorted = order // TOP_K
    gate_sorted = gate.reshape(-1)[order]
    counts = jnp.zeros((N_EXPERTS,), jnp.int32).at[flat_e].add(1)
    padded = (counts + MOE_BLOCK - 1) // MOE_BLOCK * MOE_BLOCK
    pad_end = jnp.cumsum(padded)
    pad_start = pad_end - padded
    start = jnp.cumsum(counts) - counts
    dest = pad_start[e_sorted] + jnp.arange(MK) - start[e_sorted]
    n_blocks = -(-MK // MOE_BLOCK) + N_EXPERTS
    rows = n_blocks * MOE_BLOCK
    row_tok = jnp.full((rows,), M, jnp.int32).at[dest].set(tok_sorted)
    block_e = jnp.minimum(jnp.searchsorted(pad_end, jnp.arange(n_blocks) * MOE_BLOCK, side='right'), N_EXPERTS - 1)
    x_pad = jnp.concatenate([xt, jnp.zeros((1, D), xt.dtype)], axis=0)
    xb = x_pad[row_tok].reshape(n_blocks, MOE_BLOCK, D)

    def expert_block(args):
        xblk, e = args
        h = xblk @ w_in[e] + b_in[e]
        h_gate = jnp.minimum(h[:, :D_EXPERT], SWIGLU_LIMIT)
        h_up = jnp.clip(h[:, D_EXPERT:], -SWIGLU_LIMIT, SWIGLU_LIMIT)
        act = (h_up + 1.0) * h_gate * jax.nn.sigmoid(SWIGLU_ALPHA * h_gate)
        return act @ w_out[e] + b_out[e]

    yb = lax.map(expert_block, (xb, block_e)).reshape(rows, D)
    y = jnp.zeros((M, D), jnp.float32).at[tok_sorted].add(yb[dest].astype(jnp.float32) * gate_sorted[:, None])
    return y.astype(x.dtype).reshape(B, T, D)


def setup_inputs(seed: int = 0) -> dict:
    key = jax.random.key(seed)
    keys = iter(jax.random.split(key, 48))
    f32 = jnp.float32

    def nrm(shape, scale):
        return jax.random.normal(next(keys), shape, f32) * scale

    def uni(shape, lo, hi):
        return jax.random.uniform(next(keys), shape, f32, lo, hi)

    D, H, N = D_MODEL, RWKV_HEADS, RWKV_HEAD
    NA, NB, L, E, F = N_A_LAYERS, N_B_LAYERS, DEPTH, N_EXPERTS, D_EXPERT
    kv_buf = min(MAX_WINDOW, PAST_LEN)
    s_d = D ** -0.5
    return {
        'x_prompt': nrm((BATCH, SEQ, D), 1.0),
        'x_sample': nrm((DEC_BATCH, DEC_SEQ, D), 1.0),
        'state_wkv': nrm((NA, DEC_BATCH, H, N, N), 0.3),
        'state_shift': nrm((NA, DEC_BATCH, D), 1.0),
        'cache_k': nrm((DEC_BATCH, kv_buf, ATT_HEADS, ATT_HEAD), 1.0),
        'cache_v': nrm((DEC_BATCH, kv_buf, ATT_HEADS, ATT_HEAD), DN_BETA),
        'ln_g': 1.0 + nrm((L, 2, D), 0.02),
        'ln_b': nrm((L, 2, D), 0.02),
        'rw_mu': uni((NA, 6, D), 0.0, 1.0),
        'rw_wr': nrm((NA, D, D), s_d),
        'rw_wk': nrm((NA, D, D), s_d),
        'rw_wv': nrm((NA, D, D), s_d * DN_BETA),
        'rw_wo': nrm((NA, D, D), s_d * DN_BETA),
        'rw_w0': uni((NA, D), -4.0, 1.0),
        'rw_w1': nrm((NA, D, DECAY_LORA), s_d),
        'rw_w2': nrm((NA, DECAY_LORA, D), 0.1 * DECAY_LORA ** -0.5),
        'rw_a0': nrm((NA, D), 0.1),
        'rw_a1': nrm((NA, D, AAA_LORA), s_d),
        'rw_a2': nrm((NA, AAA_LORA, D), 0.1 * AAA_LORA ** -0.5),
        'rw_v0': 1.0 + nrm((NA - 1, D), 0.1),
        'rw_v1': nrm((NA - 1, D, MV_LORA), s_d),
        'rw_v2': nrm((NA - 1, MV_LORA, D), 0.1 * MV_LORA ** -0.5),
        'rw_g1': nrm((NA, D, GATE_LORA), s_d),
        'rw_g2': nrm((NA, GATE_LORA, D), GATE_LORA ** -0.5),
        'rw_kk': 0.85 + nrm((NA, D), 0.02),
        'rw_ka': 1.0 + nrm((NA, D), 0.02),
        'rw_rk': nrm((NA, H, N), 0.1),
        'rw_gn_g': 1.0 + nrm((NA, D), 0.02),
        'rw_gn_b': nrm((NA, D), 0.02),
        'kv_w': jnp.concatenate([nrm((D, ATT_WIDTH), s_d), nrm((D, ATT_WIDTH), s_d * DN_BETA)], axis=1),
        'att_wq': nrm((NB, D, N_GROUPS * ATT_WIDTH), s_d),
        'att_wo': nrm((NB, ATT_WIDTH, D), ATT_WIDTH ** -0.5 * DN_BETA),
        'moe_wr': nrm((L, D, E), s_d),
        'moe_br': nrm((L, E), 0.01),
        'moe_win': nrm((L, E, D, 2 * F), s_d),
        'moe_bin': nrm((L, E, 2 * F), 0.01),
        'moe_wout': nrm((L, E, F, D), F ** -0.5 * DN_BETA),
        'moe_bout': nrm((L, E, D), 0.01),
    }


def reference(x_prompt, x_sample, state_wkv, state_shift, cache_k, cache_v,
              ln_g, ln_b, rw_mu, rw_wr, rw_wk, rw_wv, rw_wo, rw_w0, rw_w1, rw_w2,
              rw_a0, rw_a1, rw_a2, rw_v0, rw_v1, rw_v2, rw_g1, rw_g2, rw_kk, rw_ka,
              rw_rk, rw_gn_g, rw_gn_b, kv_w, att_wq, att_wo,
              moe_wr, moe_br, moe_win, moe_bin, moe_wout, moe_bout):
    slopes = alibi_slopes(ATT_HEADS)

    def trunk(x, shift0, wkv0, past_k, past_v, attn_fn):
        B, T, _ = x.shape
        v_first = None
        wkv_out, shift_out = [], []
        k_all = v_all = None
        for layer in range(DEPTH):
            if layer < N_A_LAYERS:
                i = layer
                vres = None if i == 0 else (rw_v0[i - 1], rw_v1[i - 1], rw_v2[i - 1])
                mix, v_first, wkv, last = rwkv7_time_mix(
                    x, shift0[i], wkv0[i], v_first, vres, rw_mu[i], rw_wr[i], rw_wk[i], rw_wv[i], rw_wo[i],
                    rw_w0[i], rw_w1[i], rw_w2[i], rw_a0[i], rw_a1[i], rw_a2[i], rw_g1[i], rw_g2[i],
                    rw_kk[i], rw_ka[i], rw_rk[i], rw_gn_g[i], rw_gn_b[i])
                wkv_out.append(wkv)
                shift_out.append(last)
            else:
                j = layer - N_A_LAYERS
                mix = dilated_mixer(x, k_all, v_all, att_wq[j], att_wo[j], slopes, attn_fn)
            x = layer_norm(DN_ALPHA * x + mix, ln_g[layer, 0], ln_b[layer, 0])
            ffn = moe_ffn(x, moe_wr[layer], moe_br[layer], moe_win[layer], moe_bin[layer],
                          moe_wout[layer], moe_bout[layer])
            x = layer_norm(DN_ALPHA * x + ffn, ln_g[layer, 1], ln_b[layer, 1])
            if layer == N_A_LAYERS - 1:
                kv = (x @ kv_w).reshape(B, T, 2, ATT_HEADS, ATT_HEAD)
                k_all, v_all = kv[:, :, 0], kv[:, :, 1]
                if past_k is not None:
                    k_all = jnp.concatenate([past_k.astype(k_all.dtype), k_all], axis=1)
                    v_all = jnp.concatenate([past_v.astype(v_all.dtype), v_all], axis=1)
        buf = min(MAX_WINDOW, k_all.shape[1]) if past_k is None else past_k.shape[1]
        return x, jnp.stack(wkv_out), jnp.stack(shift_out), k_all[:, -buf:], v_all[:, -buf:]

    bp = x_prompt.shape[0]
    shift_p0 = jnp.zeros((N_A_LAYERS, bp, D_MODEL), x_prompt.dtype)
    wkv_p0 = jnp.zeros((N_A_LAYERS, bp, RWKV_HEADS, RWKV_HEAD, RWKV_HEAD), jnp.float32)
    y_prompt, wkv_p, shift_p, k_p, v_p = trunk(x_prompt, shift_p0, wkv_p0, None, None, dilated_attn_prompt)
    y_sample, wkv_s, shift_s, k_s, v_s = trunk(x_sample, state_shift, state_wkv, cache_k, cache_v, dilated_attn_sample)
    return (y_prompt, y_sample, wkv_p, shift_p, k_p, v_p, wkv_s, shift_s, k_s, v_s)
```

```python
import functools

import jax
import jax.numpy as jnp
from jax import lax
from jax.experimental import pallas as pl
from jax.experimental.pallas import tpu as pltpu

F32 = jnp.float32
BF16 = jnp.bfloat16

D_MODEL = 1024
HEAD = 64
N_HEADS = D_MODEL // HEAD
LANES = 128
PAIRS = D_MODEL // LANES
DEPTH = 4
N_A_LAYERS = DEPTH // 2
LORA_PAD = 128
GN_EPS = 64e-5
LN_EPS = 1e-5
DN_ALPHA = (2 * DEPTH) ** 0.25
GROUPS = ((128, 1), (512, 4), (2048, 16))
ATT_STEPS = 128
N_EXPERTS = 32
TOP_K = 4
SWIGLU_LIMIT = 7.0
SWIGLU_ALPHA = 1.702
MOE_ROWS = 256
TOKEN_TILE = 256
SCAN_CHUNK = 64
SCAN_T_BLOCK = 256
NEG_BIG = -1e30
VMEM_LIMIT = 56 * 1024 * 1024


def _cparams(*sem):
    return pltpu.CompilerParams(dimension_semantics=sem, vmem_limit_bytes=VMEM_LIMIT)


def _mm(a, b):
    return jnp.dot(a.astype(BF16), b.astype(BF16), preferred_element_type=F32)


def _mm_nt(a, b):
    return lax.dot_general(a.astype(BF16), b.astype(BF16), (((1,), (1,)), ((), ())),
                           preferred_element_type=F32)


def _mm_tn(a, b):
    return lax.dot_general(a.astype(BF16), b.astype(BF16), (((0,), (0,)), ((), ())),
                           preferred_element_type=F32)


def _split(x, parts):
    out = []
    for _ in range(parts):
        h = x.astype(BF16)
        out.append(h)
        x = x - h.astype(F32)
    return out


def _mm_sel_r(x, sel, parts=2):
    acc = None
    for h in _split(x, parts):
        t = jnp.dot(h, sel, preferred_element_type=F32)
        acc = t if acc is None else acc + t
    return acc


def _mm_sel_l(sel, x, parts=3):
    acc = None
    for h in _split(x, parts):
        t = jnp.dot(sel, h, preferred_element_type=F32)
        acc = t if acc is None else acc + t
    return acc


def _sigmoid(x):
    return 1.0 / (1.0 + jnp.exp(-x))


def _layer_norm(x, g, b):
    mu = jnp.mean(x, axis=-1, keepdims=True)
    xc = x - mu
    var = jnp.mean(xc * xc, axis=-1, keepdims=True)
    return xc * lax.rsqrt(var + LN_EPS) * g + b


def _each(fn, *lists):
    return [fn(*xs) for xs in zip(*lists)]


def _unit_lower_inverse(a_mats, eye, row, col, chunk):
    blk = (row // 8) == (col // 8)
    x = [jnp.where(blk, -a, 0.0) for a in a_mats]
    x2 = _each(_mm, x, x)
    x4 = _each(_mm, x2, x2)
    xx2 = _each(_mm, x, x2)
    y = _each(lambda x_, x2_, xx2_: eye + x_ + x2_ + xx2_, x, x2, xx2)
    yx4 = _each(_mm, y, x4)
    t = _each(jnp.add, y, yx4)
    s = 8
    while s < chunk:
        rb = row // s
        off = (rb == (col // s) + 1) & ((rb % 2) == 1)
        a_off = [jnp.where(off, a, 0.0) for a in a_mats]
        at = _each(_mm, a_off, t)
        tat = _each(_mm, t, at)
        t = _each(jnp.subtract, t, tat)
        s *= 2
    return t


def _wkv_chunk(s_mat, r, kr, v, a, ld, g, prm, cst, chunk):
    kk_p, ka_p, rk_p, gng, gnb = prm
    m_a, e_seg, tri, eye, row, col, strict, incl = cst
    c = chunk
    kkr = _each(jnp.multiply, kr, kk_p)
    ss = [_mm_sel_r(x * x, e_seg) for x in kkr]
    kk = _each(lambda x, s_: x / jnp.maximum(jnp.sqrt(s_), 1e-12), kkr, ss)
    k = _each(lambda kr_, a_, ka_: kr_ * (1.0 + (a_ - 1.0) * ka_), kr, a, ka_p)
    b = _each(jnp.multiply, kk, a)
    cl = [_mm_sel_l(tri, x) for x in ld]
    cl_end = [x[c - 1:c, :] for x in cl]

    def stack(x):
        return jnp.concatenate([jnp.where(m_a, x, 0.0), jnp.where(m_a, 0.0, x)], axis=0)

    kkg = _each(lambda kk_, cl_, ld_: stack(kk_ * jnp.exp(cl_ - ld_)), kk, cl, ld)
    rg = _each(lambda r_, cl_: stack(r_ * jnp.exp(cl_)), r, cl)
    g_inv = [jnp.exp(-x) for x in cl]
    bd = _each(lambda b_, gi: stack(b_ * gi), b, g_inv)
    kd = _each(lambda k_, gi: stack(k_ * gi), k, g_inv)
    g_end = _each(lambda ce, cl_: jnp.exp(ce - cl_), cl_end, cl)
    be = _each(lambda b_, ge: stack(b_ * ge), b, g_end)
    ke = _each(lambda k_, ge: stack(k_ * ge), k, g_end)
    vs = [stack(x) for x in v]

    a_mat = [jnp.where(strict, x, 0.0) for x in _each(_mm_nt, kkg, bd)]
    b_mat = [jnp.where(strict, x, 0.0) for x in _each(_mm_nt, kkg, kd)]
    rb = [jnp.where(incl, x, 0.0) for x in _each(_mm_nt, rg, bd)]
    rk = [jnp.where(incl, x, 0.0) for x in _each(_mm_nt, rg, kd)]
    t_inv = _unit_lower_inverse(a_mat, eye, row, col, c)

    bv = _each(_mm, b_mat, vs)
    gu = _each(lambda t_, kkg_, bv_: _mm(t_, jnp.concatenate([kkg_, bv_], axis=1)), t_inv, kkg, bv)
    g_s = [x[:, :LANES] for x in gu]
    u1 = [x[:, LANES:] for x in gu]
    p_s = _each(lambda rg_, rb_, gs_: rg_ - _mm(rb_, gs_), rg, rb, g_s)
    q_s = _each(lambda rk_, vs_, rb_, u1_: _mm(rk_, vs_) - _mm(rb_, u1_), rk, vs, rb, u1)
    p = [x[:c] + x[c:] for x in p_s]
    q = [x[:c] + x[c:] for x in q_s]
    o = _each(lambda p_, s_, q_: _mm_nt(p_, s_) + q_, p, s_mat, q)
    gb = _each(_mm_tn, g_s, be)
    s_new = _each(lambda s_, ce, gb_, vs_, ke_, u1_, be_:
                  s_ * jnp.exp(ce) - _mm(s_, gb_) + _mm_tn(vs_, ke_) - _mm_tn(u1_, be_),
                  s_mat, cl_end, gb, vs, ke, u1, be)

    mean = [_mm_sel_r(x, e_seg) * (1.0 / HEAD) for x in o]
    d = _each(jnp.subtract, o, mean)
    var = [_mm_sel_r(x * x, e_seg) * (1.0 / HEAD) for x in d]
    bonus = _each(lambda r_, k_, rk_, v_: _mm_sel_r(r_ * k_ * rk_, e_seg) * v_, r, k, rk_p, v)
    out = _each(lambda d_, var_, gg, gb_, bo, g_: (d_ * lax.rsqrt(var_ + GN_EPS) * gg + gb_ + bo) * g_,
                d, var, gng, gnb, bonus, g)
    return s_new, out


def _wkv_kernel(r_ref, k_ref, v_ref, a_ref, ld_ref, g_ref, prm_ref, s0_ref, *rest, chunk, n_chunks, n_pairs):
    o_ref, s_out_ref, s_scr = rest[-3:]
    tb = pl.program_id(2)

    @pl.when(tb == 0)
    def _():
        s_scr[...] = s0_ref[0]

    c2 = 2 * chunk
    lane = lax.broadcasted_iota(jnp.int32, (1, LANES), 1)
    m_a = lane < HEAD
    er = lax.broadcasted_iota(jnp.int32, (LANES, LANES), 0)
    ec = lax.broadcasted_iota(jnp.int32, (LANES, LANES), 1)
    e_seg = ((er // HEAD) == (ec // HEAD)).astype(BF16)
    tr = lax.broadcasted_iota(jnp.int32, (chunk, chunk), 0)
    tc = lax.broadcasted_iota(jnp.int32, (chunk, chunk), 1)
    tri = (tr >= tc).astype(BF16)
    row = lax.broadcasted_iota(jnp.int32, (c2, c2), 0)
    col = lax.broadcasted_iota(jnp.int32, (c2, c2), 1)
    same = (row // chunk) == (col // chunk)
    strict = same & (col < row)
    incl = same & (col <= row)
    eye = (row == col).astype(F32)
    cst = (m_a, e_seg, tri, eye, row, col, strict, incl)

    def body(ci, carry):
        sl = pl.ds(pl.multiple_of(ci * chunk, chunk), chunk)
        lanes = [slice(p * LANES, (p + 1) * LANES) for p in range(n_pairs)]
        prm = tuple([prm_ref[i:i + 1, ln] for ln in lanes] for i in range(5))
        seqs = [[ref[sl, ln] for ln in lanes] for ref in (r_ref, k_ref, v_ref, a_ref, ld_ref, g_ref)]
        s_new, out = _wkv_chunk([s_scr[p] for p in range(n_pairs)], *seqs, prm, cst, chunk)
        for p in range(n_pairs):
            s_scr[p] = s_new[p]
            o_ref[sl, lanes[p]] = out[p]
        return carry

    lax.fori_loop(0, n_chunks, body, 0)

    @pl.when(tb == pl.num_programs(2) - 1)
    def _():
        s_out_ref[0] = s_scr[...]


def wkv_scan(seqs, prm, s0, *, row0, t_len, chunk, t_block, n_pairs=PAIRS, out=None):
    m_rows, d = seqs[0].shape
    bsz = s0.shape[0]
    assert d == D_MODEL and t_len % t_block == 0 and t_block % chunk == 0 and row0 % t_block == 0
    assert PAIRS % n_pairs == 0 and row0 + bsz * t_len <= m_rows
    width = n_pairs * LANES
    nt = t_len // t_block
    blk0 = row0 // t_block
    seq = pl.BlockSpec((t_block, width), lambda b, p, t: (blk0 + b * nt + t, p))
    st = pl.BlockSpec((1, n_pairs, LANES, LANES), lambda b, p, t: (b, p, 0, 0))
    in_specs = [seq] * 6 + [pl.BlockSpec((5, width), lambda b, p, t: (0, p)), st]
    args = list(seqs) + [prm, s0]
    aliases = {}
    if out is not None:
        in_specs.append(pl.BlockSpec(memory_space=pl.ANY))
        args.append(out)
        aliases = {len(args) - 1: 0}
    return pl.pallas_call(
        functools.partial(_wkv_kernel, chunk=chunk, n_chunks=t_block // chunk, n_pairs=n_pairs),
        grid=(bsz, PAIRS // n_pairs, nt),
        in_specs=in_specs,
        out_specs=[seq, st],
        out_shape=[jax.ShapeDtypeStruct((m_rows, d), F32),
                   jax.ShapeDtypeStruct((bsz, PAIRS, LANES, LANES), F32)],
        scratch_shapes=[pltpu.VMEM((n_pairs, LANES, LANES), F32)],
        input_output_aliases=aliases,
        compiler_params=_cparams("parallel", "parallel", "arbitrary"),
        name="wkv_scan",
    )(*args)


def pair_states(s):
    bsz = s.shape[0]
    s = s.reshape(bsz, PAIRS, 2, HEAD, HEAD)
    z = jnp.zeros_like(s[:, :, 0])
    top = jnp.concatenate([s[:, :, 0], z], axis=-1)
    bot = jnp.concatenate([z, s[:, :, 1]], axis=-1)
    return jnp.concatenate([top, bot], axis=-2)


def unpair_states(sp):
    bsz = sp.shape[0]
    s = jnp.stack([sp[:, :, :HEAD, :HEAD], sp[:, :, HEAD:, HEAD:]], axis=2)
    return s.reshape(bsz, N_HEADS, HEAD, HEAD)


def _a_proj_kernel(*refs, has_vres):
    if has_vres:
        (x_ref, xp_ref, vf_ref, mu_ref, vec_ref, wr, wk, wv, w1, w2, a1, a2, g1, g2, v1, v2,
         r_o, k_o, v_o, a_o, ld_o, g_o) = refs
    else:
        (x_ref, xp_ref, mu_ref, vec_ref, wr, wk, wv, w1, w2, a1, a2, g1, g2,
         r_o, k_o, v_o, a_o, ld_o, g_o) = refs
    x = x_ref[...]
    xx = xp_ref[...] - x
    xr, xw, xk, xv, xa, xg = [(x + xx * mu_ref[i:i + 1, :]).astype(BF16) for i in range(6)]
    r_o[...] = _mm(xr, wr[...])
    k_o[...] = _mm(xk, wk[...])
    v = _mm(xv, wv[...])
    z = vec_ref[0:1, :] + _mm(jnp.tanh(_mm(xw, w1[...])), w2[...])
    softplus_neg = jnp.maximum(-z, 0.0) + jnp.log(1.0 + jnp.exp(-jnp.abs(z)))
    ld_o[...] = -jnp.exp(-softplus_neg - 0.5)
    if has_vres:
        mix = _sigmoid(vec_ref[2:3, :] + _mm(_mm(xv, v1[...]), v2[...]))
        v = v + (vf_ref[...] - v) * mix
    v_o[...] = v
    a_o[...] = _sigmoid(vec_ref[1:2, :] + _mm(_mm(xa, a1[...]), a2[...]))
    g_o[...] = _mm(_sigmoid(_mm(xg, g1[...])), g2[...])


def a_proj(x, x_prev, v_first, mu, vec, mats, *, tm):
    m_rows = x.shape[0]
    assert m_rows % tm == 0
    tok = pl.BlockSpec((tm, D_MODEL), lambda i: (i, 0))
    full = lambda a: pl.BlockSpec(a.shape, lambda i: (0, 0))
    has_vres = v_first is not None
    acts = [x, x_prev] + ([v_first] if has_vres else [])
    consts = [mu, vec] + list(mats)
    return pl.pallas_call(
        functools.partial(_a_proj_kernel, has_vres=has_vres),
        grid=(m_rows // tm,),
        in_specs=[tok] * len(acts) + [full(c) for c in consts],
        out_specs=[tok] * 6,
        out_shape=[jax.ShapeDtypeStruct((m_rows, D_MODEL), F32)] * 6,
        compiler_params=_cparams("parallel"),
        name="a_proj",
    )(*acts, *consts)


def _post_kernel(y_ref, x_ref, wo_ref, ln_ref, wrh_ref, wrl_ref, br_ref, x1_ref, x1b_ref, lg_ref):
    y = _mm(y_ref[...], wo_ref[...])
    x1 = _layer_norm(DN_ALPHA * x_ref[...] + y, ln_ref[0:1, :], ln_ref[1:2, :])
    x1_ref[...] = x1
    x1b_ref[...] = x1.astype(BF16)
    parts = _split(x1, 3)
    acc = br_ref[...]
    for h in parts:
        acc = acc + jnp.dot(h, wrh_ref[...], preferred_element_type=F32)
    for h in parts[:2]:
        acc = acc + jnp.dot(h, wrl_ref[...], preferred_element_type=F32)
    lg_ref[...] = acc


def post_mix(y, x, wo, ln, wr_hi, wr_lo, br, *, tm):
    m_rows = x.shape[0]
    tok = pl.BlockSpec((tm, D_MODEL), lambda i: (i, 0))
    full = lambda a: pl.BlockSpec(a.shape, lambda i: (0, 0))
    return pl.pallas_call(
        _post_kernel,
        grid=(m_rows // tm,),
        in_specs=[tok, tok] + [full(c) for c in (wo, ln, wr_hi, wr_lo, br)],
        out_specs=[tok, tok, pl.BlockSpec((tm, LANES), lambda i: (i, 0))],
        out_shape=[jax.ShapeDtypeStruct((m_rows, D_MODEL), F32),
                   jax.ShapeDtypeStruct((m_rows, D_MODEL), BF16),
                   jax.ShapeDtypeStruct((m_rows, LANES), F32)],
        compiler_params=_cparams("parallel"),
        name="post_mix",
    )(y, x, wo, ln, wr_hi, wr_lo, br)


def _moe_kernel(be_ref, first_ref, nact_ref, xb_ref, win_ref, bin_ref, wout_ref, bout_ref, y_ref, win_s, wout_s):
    i = pl.program_id(0)
    slab = 128

    @pl.when(first_ref[i] == 1)
    def _():
        for j in range(D_MODEL // slab):
            rows = slice(j * slab, (j + 1) * slab)
            win_s[rows, :] = win_ref[0, rows, :].astype(BF16)
            wout_s[rows, :] = wout_ref[0, rows, :].astype(BF16)

    @pl.when(i < nact_ref[0])
    def _():
        h = jnp.dot(xb_ref[...], win_s[...], preferred_element_type=F32) + bin_ref[0]
        h_gate = jnp.minimum(h[:, :D_MODEL], SWIGLU_LIMIT)
        h_up = jnp.clip(h[:, D_MODEL:], -SWIGLU_LIMIT, SWIGLU_LIMIT)
        act = (h_up + 1.0) * h_gate * _sigmoid(SWIGLU_ALPHA * h_gate)
        y_ref[...] = jnp.dot(act.astype(BF16), wout_s[...], preferred_element_type=F32) + bout_ref[0]


def moe_experts(block_e, first, n_act, xb, w_in, b_in, w_out, b_out):
    rows = xb.shape[0]
    n_blocks = rows // MOE_ROWS
    grid_spec = pltpu.PrefetchScalarGridSpec(
        num_scalar_prefetch=3,
        grid=(n_blocks,),
        in_specs=[
            pl.BlockSpec((MOE_ROWS, D_MODEL), lambda i, be, fi, na: (i, 0)),
            pl.BlockSpec((1, D_MODEL, 2 * D_MODEL), lambda i, be, fi, na: (be[i], 0, 0)),
            pl.BlockSpec((1, 1, 2 * D_MODEL), lambda i, be, fi, na: (be[i], 0, 0)),
            pl.BlockSpec((1, D_MODEL, D_MODEL), lambda i, be, fi, na: (be[i], 0, 0)),
            pl.BlockSpec((1, 1, D_MODEL), lambda i, be, fi, na: (be[i], 0, 0)),
        ],
        out_specs=pl.BlockSpec((MOE_ROWS, D_MODEL), lambda i, be, fi, na: (i, 0)),
        scratch_shapes=[pltpu.VMEM((D_MODEL, 2 * D_MODEL), BF16), pltpu.VMEM((D_MODEL, D_MODEL), BF16)],
    )
    return pl.pallas_call(
        _moe_kernel,
        grid_spec=grid_spec,
        out_shape=jax.ShapeDtypeStruct((rows, D_MODEL), F32),
        compiler_params=_cparams("arbitrary"),
        name="moe_experts",
    )(block_e, first, n_act, xb, w_in, b_in.reshape(N_EXPERTS, 1, -1), w_out, b_out.reshape(N_EXPERTS, 1, -1))


def _combine_kernel(yg_ref, gate_ref, x_ref, ln_ref, o_ref):
    gate = gate_ref[...]
    ffn = gate[:, 0:1] * yg_ref[0]
    for k in range(1, TOP_K):
        ffn = ffn + gate[:, k:k + 1] * yg_ref[k]
    o_ref[...] = _layer_norm(DN_ALPHA * x_ref[...] + ffn, ln_ref[0:1, :], ln_ref[1:2, :])


def moe_combine(yg, gate, x, ln, *, tm):
    m_rows = x.shape[0]
    tok = pl.BlockSpec((tm, D_MODEL), lambda i: (i, 0))
    return pl.pallas_call(
        _combine_kernel,
        grid=(m_rows // tm,),
        in_specs=[pl.BlockSpec((TOP_K, tm, D_MODEL), lambda i: (0, i, 0)),
                  pl.BlockSpec((tm, TOP_K), lambda i: (i, 0)), tok,
                  pl.BlockSpec(ln.shape, lambda i: (0, 0))],
        out_specs=tok,
        out_shape=jax.ShapeDtypeStruct((m_rows, D_MODEL), F32),
        compiler_params=_cparams("parallel"),
        name="moe_combine",
    )(yg, gate, x, ln)


def _route(logits, m_rows):
    top_logit, top_e = lax.top_k(logits[:, :N_EXPERTS], TOP_K)
    gate = jax.nn.softmax(top_logit, axis=-1)
    mk = m_rows * TOP_K
    flat_e = top_e.reshape(-1).astype(jnp.int32)
    order = jnp.argsort(flat_e).astype(jnp.int32)
    e_sorted = flat_e[order]
    counts = jnp.zeros((N_EXPERTS,), jnp.int32).at[flat_e].add(1)
    padded = (counts + MOE_ROWS - 1) // MOE_ROWS * MOE_ROWS
    pad_end = jnp.cumsum(padded)
    pad_start = pad_end - padded
    start = jnp.cumsum(counts) - counts
    dest = pad_start[e_sorted] + jnp.arange(mk, dtype=jnp.int32) - start[e_sorted]
    n_blocks = -(-mk // MOE_ROWS) + N_EXPERTS
    rows = n_blocks * MOE_ROWS
    row_tok = jnp.full((rows,), m_rows, jnp.int32).at[dest].set(order // TOP_K)
    pos = jnp.zeros((mk,), jnp.int32).at[order].set(dest)
    blk_start = jnp.arange(n_blocks, dtype=jnp.int32) * MOE_ROWS
    block_e = jnp.minimum(jnp.searchsorted(pad_end, blk_start, side='right'), N_EXPERTS - 1).astype(jnp.int32)
    first = ((blk_start == pad_start[block_e]) & (blk_start < pad_end[-1])).astype(jnp.int32)
    n_act = (pad_end[-1:] // MOE_ROWS).astype(jnp.int32)
    return gate, row_tok, pos, block_e, first, n_act


def moe_layer(x1, x1b, logits, w_in, b_in, w_out, b_out, ln, *, tm):
    m_rows = x1.shape[0]
    gate, row_tok, pos, block_e, first, n_act = _route(logits, m_rows)
    x_pad = jnp.concatenate([x1b, jnp.zeros((1, D_MODEL), BF16)], axis=0)
    xb = x_pad[row_tok]
    yb = moe_experts(block_e, first, n_act, xb, w_in, b_in, w_out, b_out)
    yg = yb[pos.reshape(m_rows, TOP_K).T]
    return moe_combine(yg, gate, x1, ln, tm=tm)


def _dense_kernel(x_ref, w_ref, o_ref):
    o_ref[...] = _mm(x_ref[...], w_ref[...]).astype(o_ref.dtype)


def dense(x, w, *, tm, tn, out_dtype=F32):
    m_rows, k_dim = x.shape
    n_dim = w.shape[1]
    assert m_rows % tm == 0 and n_dim % tn == 0
    return pl.pallas_call(
        _dense_kernel,
        grid=(n_dim // tn, m_rows // tm),
        in_specs=[pl.BlockSpec((tm, k_dim), lambda j, i: (i, 0)),
                  pl.BlockSpec((k_dim, tn), lambda j, i: (0, j))],
        out_specs=pl.BlockSpec((tm, tn), lambda j, i: (i, j)),
        out_shape=jax.ShapeDtypeStruct((m_rows, n_dim), out_dtype),
        compiler_params=_cparams("parallel", "parallel"),
        name="dense",
    )(x, w)


def _slope(head):
    return 2.0 ** (-8.0 * (head + 1) / N_HEADS)


def _attn_prompt_kernel(q_ref, kp_ref, kc_ref, vp_ref, vc_ref, o_ref, lse_ref, *, dil):
    n = pl.program_id(2)
    nk = ATT_STEPS
    qi = lax.broadcasted_iota(jnp.int32, (nk, 2 * nk), 0)
    kj = lax.broadcasted_iota(jnp.int32, (nk, 2 * nk), 1)
    delta = qi + nk - kj
    valid = (delta >= 0) & (delta <= nk) & ((kj >= nk) | (n > 0))
    dist = (delta * dil).astype(F32)
    lane = lax.broadcasted_iota(jnp.int32, (1, LANES), 1)
    m_a = lane < HEAD

    def scores(p):
        ln = slice(p * LANES, (p + 1) * LANES)
        q = q_ref[:, ln].astype(F32) * (HEAD ** -0.5)
        k = jnp.concatenate([kp_ref[:, ln], kc_ref[:, ln]], axis=0).astype(BF16)
        return [_mm_nt(jnp.where(m_a, q, 0.0), k), _mm_nt(jnp.where(m_a, 0.0, q), k)]

    s_next = scores(0)
    for p in range(PAIRS):
        s_cur = s_next
        if p + 1 < PAIRS:
            s_next = scores(p + 1)
        ln = slice(p * LANES, (p + 1) * LANES)
        v = jnp.concatenate([vp_ref[:, ln], vc_ref[:, ln]], axis=0).astype(BF16)
        outs, lses = [], []
        for hh in range(2):
            s = jnp.where(valid, s_cur[hh] - _slope(2 * p + hh) * dist, NEG_BIG)
            m = jnp.max(s, axis=-1, keepdims=True)
            e = jnp.exp(s - m)
            l = jnp.sum(e, axis=-1, keepdims=True)
            outs.append(jnp.dot(e.astype(BF16), v, preferred_element_type=F32) / l)
            lses.append(m + jnp.log(l))
        o_ref[:, ln] = jnp.where(m_a, outs[0], outs[1])
        lse_ref[:, ln] = jnp.where(m_a, lses[0], lses[1])


def attn_prompt_group(q, k, v, *, group, bsz, seq_len, dil):
    nk = ATT_STEPS
    sub_len = seq_len // dil
    assert seq_len % dil == 0 and sub_len % nk == 0
    n_tiles = sub_len // nk
    q4 = q[:bsz * seq_len].reshape(bsz, sub_len, dil * 3 * D_MODEL)
    k4 = k.reshape(bsz, sub_len, dil * D_MODEL)
    v4 = v.reshape(bsz, sub_len, dil * D_MODEL)
    cur = pl.BlockSpec((None, nk, D_MODEL), lambda b, r, n: (b, n, r))
    prev = pl.BlockSpec((None, nk, D_MODEL), lambda b, r, n: (b, jnp.maximum(n - 1, 0), r))
    o, lse = pl.pallas_call(
        functools.partial(_attn_prompt_kernel, dil=dil),
        grid=(bsz, dil, n_tiles),
        in_specs=[pl.BlockSpec((None, nk, D_MODEL), lambda b, r, n: (b, n, 3 * r + group)),
                  prev, cur, prev, cur],
        out_specs=[cur, cur],
        out_shape=[jax.ShapeDtypeStruct((bsz, sub_len, dil * D_MODEL), F32)] * 2,
        compiler_params=_cparams("parallel", "parallel", "arbitrary"),
        name="attn_prompt",
    )(q4, k4, k4, v4, v4)
    return o.reshape(bsz * seq_len, D_MODEL), lse.reshape(bsz * seq_len, D_MODEL)


def _attn_merge_kernel(o0, l0, o1, l1, o2, l2, out_ref):
    m = jnp.maximum(jnp.maximum(l0[...], l1[...]), l2[...])
    w0 = jnp.exp(l0[...] - m)
    w1 = jnp.exp(l1[...] - m)
    w2 = jnp.exp(l2[...] - m)
    out_ref[...] = (w0 * o0[...] + w1 * o1[...] + w2 * o2[...]) / (w0 + w1 + w2)


def attn_merge(parts, m_rows, *, tm):
    mp = parts[0].shape[0]
    tok = pl.BlockSpec((tm, D_MODEL), lambda i: (i, 0))
    return pl.pallas_call(
        _attn_merge_kernel,
        grid=(mp // tm,),
        in_specs=[tok] * 6,
        out_specs=tok,
        out_shape=jax.ShapeDtypeStruct((m_rows, D_MODEL), F32),
        compiler_params=_cparams("parallel"),
        name="attn_merge",
    )(*parts)


def _attn_sample_kernel(q0_ref, q1_ref, q2_ref, kc_ref, kn_ref, vc_ref, vn_ref, alias_ref, o_ref, *, t_len, kv_buf):
    del alias_ref
    p = pl.program_id(1)
    lane = lax.broadcasted_iota(jnp.int32, (1, LANES), 1)
    m_a = lane < HEAD
    lhs = []
    for q_ref in (q0_ref, q1_ref, q2_ref):
        q = q_ref[...].astype(F32) * (HEAD ** -0.5)
        lhs += [jnp.where(m_a, q, 0.0), jnp.where(m_a, 0.0, q)]
    lhs = jnp.concatenate(lhs, axis=0)
    n_rows = 6 * t_len
    pad = jnp.zeros((LANES - t_len, LANES), F32)
    kn = jnp.concatenate([kn_ref[...], pad], axis=0)
    vn = jnp.concatenate([vn_ref[...], pad], axis=0)
    s_c = _mm_nt(lhs, kc_ref[...])
    s_n = _mm_nt(lhs, kn)

    ri = lax.broadcasted_iota(jnp.int32, (n_rows, 1), 0)
    t = ri % t_len
    grp = ri // (2 * t_len)
    hh = (ri // t_len) % 2
    dmask = jnp.where(grp == 0, GROUPS[0][1] - 1, jnp.where(grp == 1, GROUPS[1][1] - 1, GROUPS[2][1] - 1))
    win = jnp.where(grp == 0, GROUPS[0][0], jnp.where(grp == 1, GROUPS[1][0], GROUPS[2][0]))
    head = (2 * p + hh).astype(F32)
    slope = jnp.exp2(-8.0 * (head + 1.0) / N_HEADS)

    def masked(s, dist):
        ok = (dist >= 0) & (dist <= win) & ((dist & dmask) == 0)
        return jnp.where(ok, s - slope * dist.astype(F32), NEG_BIG)

    jc = lax.broadcasted_iota(jnp.int32, (n_rows, kv_buf), 1)
    jn = lax.broadcasted_iota(jnp.int32, (n_rows, LANES), 1)
    s_c = masked(s_c, kv_buf + t - jc)
    s_n = masked(s_n, t - jn)
    m = jnp.maximum(jnp.max(s_c, axis=-1, keepdims=True), jnp.max(s_n, axis=-1, keepdims=True))
    e_c = jnp.exp(s_c - m)
    e_n = jnp.exp(s_n - m)
    l = jnp.sum(e_c, axis=-1, keepdims=True) + jnp.sum(e_n, axis=-1, keepdims=True)
    acc = _mm(e_c, vc_ref[...]) + _mm(e_n, vn)

    blk = 2 * t_len
    m_g = [m[g * blk:(g + 1) * blk] for g in range(3)]
    m_all = jnp.maximum(jnp.maximum(m_g[0], m_g[1]), m_g[2])
    num = 0.0
    den = 0.0
    for g in range(3):
        w = jnp.exp(m_g[g] - m_all)
        num = num + w * acc[g * blk:(g + 1) * blk]
        den = den + w * l[g * blk:(g + 1) * blk]
    res = num / den
    o_ref[...] = jnp.where(m_a, res[:t_len], res[t_len:])


def attn_sample(q, cache_k, cache_v, k_new, v_new, out, *, row0, bsz, t_len):
    kv_buf = cache_k.shape[1]
    assert kv_buf >= GROUPS[-1][0] and kv_buf % LANES == 0 and row0 % t_len == 0 and t_len % 8 == 0
    blk0 = row0 // t_len
    qs = [pl.BlockSpec((t_len, LANES), lambda b, p, g=g: (blk0 + b, g * PAIRS + p)) for g in range(3)]
    cache = pl.BlockSpec((None, kv_buf, LANES), lambda b, p: (b, 0, p))
    new = pl.BlockSpec((None, t_len, LANES), lambda b, p: (b, 0, p))
    return pl.pallas_call(
        functools.partial(_attn_sample_kernel, t_len=t_len, kv_buf=kv_buf),
        grid=(bsz, PAIRS),
        in_specs=qs + [cache, new, cache, new, pl.BlockSpec(memory_space=pl.ANY)],
        out_specs=pl.BlockSpec((t_len, LANES), lambda b, p: (blk0 + b, p)),
        out_shape=jax.ShapeDtypeStruct(out.shape, F32),
        input_output_aliases={7: 0},
        compiler_params=_cparams("parallel", "parallel"),
        name="attn_sample",
    )(q, q, q, cache_k, k_new, cache_v, v_new, out)


def _pad_cols(w):
    return jnp.pad(w, ((0, 0), (0, LORA_PAD - w.shape[1]))).astype(BF16)


def _pad_rows(w):
    return jnp.pad(w, ((0, LORA_PAD - w.shape[0]), (0, 0))).astype(BF16)


def kernel(x_prompt, x_sample, state_wkv, state_shift, cache_k, cache_v, ln_g, ln_b, rw_mu, rw_wr, rw_wk, rw_wv,
           rw_wo, rw_w0, rw_w1, rw_w2, rw_a0, rw_a1, rw_a2, rw_v0, rw_v1, rw_v2, rw_g1, rw_g2, rw_kk, rw_ka,
           rw_rk, rw_gn_g, rw_gn_b, kv_w, att_wq, att_wo, moe_wr, moe_br, moe_win, moe_bin, moe_wout, moe_bout):
    bp, seq_len, d = x_prompt.shape
    bs, dec_len, _ = x_sample.shape
    kv_buf = cache_k.shape[1]
    mp = bp * seq_len
    ms = bs * dec_len
    m_rows = mp + ms
    tm = TOKEN_TILE
    assert d == D_MODEL and m_rows % tm == 0 and mp % tm == 0
    t_block = min(SCAN_T_BLOCK, seq_len)

    x = jnp.concatenate([x_prompt.reshape(mp, d), x_sample.reshape(ms, d)], axis=0)
    wkv_p, wkv_s, shift_p, shift_s = [], [], [], []
    v_first = None
    k_p = v_p = k_new = v_new = None

    def moe(layer, x1, x1b, logits):
        return moe_layer(x1, x1b, logits, moe_win[layer], moe_bin[layer], moe_wout[layer], moe_bout[layer],
                         jnp.stack([ln_g[layer, 1], ln_b[layer, 1]]), tm=tm)

    def post(layer, y, x_in, wo):
        wr = moe_wr[layer]
        wr_hi = wr.astype(BF16)
        wr_lo = (wr - wr_hi.astype(F32)).astype(BF16)
        padc = lambda w: jnp.pad(w, ((0, 0), (0, LANES - N_EXPERTS)))
        br = jnp.pad(moe_br[layer], (0, LANES - N_EXPERTS)).reshape(1, LANES)
        return post_mix(y, x_in, wo.astype(BF16), jnp.stack([ln_g[layer, 0], ln_b[layer, 0]]),
                        padc(wr_hi), padc(wr_lo), br, tm=tm)

    for layer in range(DEPTH):
        if layer < N_A_LAYERS:
            i = layer
            xp3 = x[:mp].reshape(bp, seq_len, d)
            xs3 = x[mp:].reshape(bs, dec_len, d)
            shift_p.append(xp3[:, -1])
            shift_s.append(xs3[:, -1])
            prev_p = jnp.concatenate([jnp.zeros((bp, 1, d), F32), xp3[:, :-1]], axis=1)
            prev_s = jnp.concatenate([state_shift[i][:, None, :], xs3[:, :-1]], axis=1)
            x_prev = jnp.concatenate([prev_p.reshape(mp, d), prev_s.reshape(ms, d)], axis=0)
            vec = jnp.stack([rw_w0[i], rw_a0[i], rw_v0[i - 1] if i > 0 else jnp.zeros((d,), F32)])
            mats = [rw_wr[i].astype(BF16), rw_wk[i].astype(BF16), rw_wv[i].astype(BF16),
                    _pad_cols(rw_w1[i]), _pad_rows(rw_w2[i]), _pad_cols(rw_a1[i]), _pad_rows(rw_a2[i]),
                    _pad_cols(rw_g1[i]), _pad_rows(rw_g2[i])]
            if i > 0:
                mats += [_pad_cols(rw_v1[i - 1]), _pad_rows(rw_v2[i - 1])]
            r, k, v, a, ld, g = a_proj(x, x_prev, v_first, rw_mu[i], vec, mats, tm=tm)
            if i == 0:
                v_first = v
            prm = jnp.stack([rw_kk[i], rw_ka[i], rw_rk[i].reshape(d), rw_gn_g[i], rw_gn_b[i]])
            seqs = (r, k, v, a, ld, g)
            y, sp = wkv_scan(seqs, prm, jnp.zeros((bp, PAIRS, LANES, LANES), F32), row0=0, t_len=seq_len,
                             chunk=SCAN_CHUNK, t_block=t_block)
            y, ss = wkv_scan(seqs, prm, pair_states(state_wkv[i]), row0=mp, t_len=dec_len,
                             chunk=dec_len, t_block=dec_len, out=y)
            wkv_p.append(unpair_states(sp))
            wkv_s.append(unpair_states(ss))
            x1, x1b, logits = post(layer, y, x, rw_wo[i])
        else:
            j = layer - N_A_LAYERS
            q = dense(x, att_wq[j].astype(BF16), tm=tm, tn=D_MODEL)
            parts = []
            for gi, (window, dil) in enumerate(GROUPS):
                assert window // dil == ATT_STEPS
                parts += list(attn_prompt_group(q, k_p, v_p, group=gi, bsz=bp, seq_len=seq_len, dil=dil))
            y = attn_merge(parts, m_rows, tm=tm)
            y = attn_sample(q, cache_k.reshape(bs, kv_buf, d), cache_v.reshape(bs, kv_buf, d), k_new, v_new, y,
                            row0=mp, bsz=bs, t_len=dec_len)
            x1, x1b, logits = post(layer, y, x, att_wo[j])
        x = moe(layer, x1, x1b, logits)
        if layer == N_A_LAYERS - 1:
            kv = dense(x, kv_w.astype(BF16), tm=tm, tn=D_MODEL)
            k_p = kv[:mp, :d]
            v_p = kv[:mp, d:]
            k_new = kv[mp:, :d].reshape(bs, dec_len, d)
            v_new = kv[mp:, d:].reshape(bs, dec_len, d)

    buf_p = min(GROUPS[-1][0], seq_len)
    heads = lambda t, n: t.reshape(t.shape[0], n, N_HEADS, HEAD)
    k_p_out = heads(k_p.reshape(bp, seq_len, d)[:, -buf_p:], buf_p)
    v_p_out = heads(v_p.reshape(bp, seq_len, d)[:, -buf_p:], buf_p)
    k_s_out = jnp.concatenate([cache_k, heads(k_new, dec_len)], axis=1)[:, -kv_buf:]
    v_s_out = jnp.concatenate([cache_v, heads(v_new, dec_len)], axis=1)[:, -kv_buf:]
    return (x[:mp].reshape(bp, seq_len, d), x[mp:].reshape(bs, dec_len, d),
            jnp.stack(wkv_p), jnp.stack(shift_p), k_p_out, v_p_out,
            jnp.stack(wkv_s), jnp.stack(shift_s), k_s_out, v_s_out)
```

```python
import functools

import jax
import jax.numpy as jnp
from jax import lax
from jax.experimental import pallas as pl
from jax.experimental.pallas import tpu as pltpu

F32 = jnp.float32
BF16 = jnp.bfloat16

D_MODEL = 1024
HEAD = 64
N_HEADS = D_MODEL // HEAD
LANES = 128
PAIRS = D_MODEL // LANES
DEPTH = 4
N_A_LAYERS = DEPTH // 2
LORA_PAD = 128
GN_EPS = 64e-5
LN_EPS = 1e-5
DN_ALPHA = (2 * DEPTH) ** 0.25
GROUPS = ((128, 1), (512, 4), (2048, 16))
ATT_STEPS = 128
N_EXPERTS = 32
TOP_K = 4
SWIGLU_LIMIT = 7.0
SWIGLU_ALPHA = 1.702
MOE_ROWS = 256
TOKEN_TILE = 256
SCAN_CHUNK = 64
SCAN_T_BLOCK = 256
NEG_BIG = -1e30
VMEM_LIMIT = 56 * 1024 * 1024


def _cparams(*sem):
    return pltpu.CompilerParams(dimension_semantics=sem, vmem_limit_bytes=VMEM_LIMIT)


def _mm(a, b):
    return jnp.dot(a.astype(BF16), b.astype(BF16), preferred_element_type=F32)


def _mm_nt(a, b):
    return lax.dot_general(a.astype(BF16), b.astype(BF16), (((1,), (1,)), ((), ())),
                           preferred_element_type=F32)


def _mm_tn(a, b):
    return lax.dot_general(a.astype(BF16), b.astype(BF16), (((0,), (0,)), ((), ())),
                           preferred_element_type=F32)


def _split(x, parts):
    out = []
    for _ in range(parts):
        h = x.astype(BF16)
        out.append(h)
        x = x - h.astype(F32)
    return out


def _mm_sel_r(x, sel, parts=2):
    acc = None
    for h in _split(x, parts):
        t = jnp.dot(h, sel, preferred_element_type=F32)
        acc = t if acc is None else acc + t
    return acc


def _mm_sel_l(sel, x, parts=3):
    acc = None
    for h in _split(x, parts):
        t = jnp.dot(sel, h, preferred_element_type=F32)
        acc = t if acc is None else acc + t
    return acc


def _sigmoid(x):
    return 1.0 / (1.0 + jnp.exp(-x))


def _layer_norm(x, g, b):
    mu = jnp.mean(x, axis=-1, keepdims=True)
    xc = x - mu
    var = jnp.mean(xc * xc, axis=-1, keepdims=True)
    return xc * lax.rsqrt(var + LN_EPS) * g + b


def _each(fn, *lists):
    return [fn(*xs) for xs in zip(*lists)]


def _unit_lower_inverse(a_mats, eye, row, col, chunk):
    blk = (row // 8) == (col // 8)
    x = [jnp.where(blk, -a, 0.0) for a in a_mats]
    x2 = _each(_mm, x, x)
    x4 = _each(_mm, x2, x2)
    xx2 = _each(_mm, x, x2)
    y = _each(lambda x_, x2_, xx2_: eye + x_ + x2_ + xx2_, x, x2, xx2)
    yx4 = _each(_mm, y, x4)
    t = _each(jnp.add, y, yx4)
    s = 8
    while s < chunk:
        rb = row // s
        off = (rb == (col // s) + 1) & ((rb % 2) == 1)
        a_off = [jnp.where(off, a, 0.0) for a in a_mats]
        at = _each(_mm, a_off, t)
        tat = _each(_mm, t, at)
        t = _each(jnp.subtract, t, tat)
        s *= 2
    return t


def _wkv_chunk(s_mat, r, kr, v, a, ld, g, prm, cst, chunk):
    kk_p, ka_p, rk_p, gng, gnb = prm
    m_a, e_seg, tri, eye, row, col, strict, incl = cst
    c = chunk
    kkr = _each(jnp.multiply, kr, kk_p)
    ss = [_mm_sel_r(x * x, e_seg) for x in kkr]
    kk = _each(lambda x, s_: x / jnp.maximum(jnp.sqrt(s_), 1e-12), kkr, ss)
    k = _each(lambda kr_, a_, ka_: kr_ * (1.0 + (a_ - 1.0) * ka_), kr, a, ka_p)
    b = _each(jnp.multiply, kk, a)
    cl = [_mm_sel_l(tri, x) for x in ld]
    cl_end = [x[c - 1:c, :] for x in cl]

    def stack(x):
        return jnp.concatenate([jnp.where(m_a, x, 0.0), jnp.where(m_a, 0.0, x)], axis=0)

    kkg = _each(lambda kk_, cl_, ld_: stack(kk_ * jnp.exp(cl_ - ld_)), kk, cl, ld)
    rg = _each(lambda r_, cl_: stack(r_ * jnp.exp(cl_)), r, cl)
    g_inv = [jnp.exp(-x) for x in cl]
    bd = _each(lambda b_, gi: stack(b_ * gi), b, g_inv)
    kd = _each(lambda k_, gi: stack(k_ * gi), k, g_inv)
    g_end = _each(lambda ce, cl_: jnp.exp(ce - cl_), cl_end, cl)
    be = _each(lambda b_, ge: stack(b_ * ge), b, g_end)
    ke = _each(lambda k_, ge: stack(k_ * ge), k, g_end)
    vs = [stack(x) for x in v]

    a_mat = [jnp.where(strict, x, 0.0) for x in _each(_mm_nt, kkg, bd)]
    b_mat = [jnp.where(strict, x, 0.0) for x in _each(_mm_nt, kkg, kd)]
    rb = [jnp.where(incl, x, 0.0) for x in _each(_mm_nt, rg, bd)]
    rk = [jnp.where(incl, x, 0.0) for x in _each(_mm_nt, rg, kd)]
    t_inv = _unit_lower_inverse(a_mat, eye, row, col, c)

    bv = _each(_mm, b_mat, vs)
    gu = _each(lambda t_, kkg_, bv_: _mm(t_, jnp.concatenate([kkg_, bv_], axis=1)), t_inv, kkg, bv)
    g_s = [x[:, :LANES] for x in gu]
    u1 = [x[:, LANES:] for x in gu]
    p_s = _each(lambda rg_, rb_, gs_: rg_ - _mm(rb_, gs_), rg, rb, g_s)
    q_s = _each(lambda rk_, vs_, rb_, u1_: _mm(rk_, vs_) - _mm(rb_, u1_), rk, vs, rb, u1)
    p = [x[:c] + x[c:] for x in p_s]
    q = [x[:c] + x[c:] for x in q_s]
    o = _each(lambda p_, s_, q_: _mm_nt(p_, s_) + q_, p, s_mat, q)
    gb = _each(_mm_tn, g_s, be)
    s_new = _each(lambda s_, ce, gb_, vs_, ke_, u1_, be_:
                  s_ * jnp.exp(ce) - _mm(s_, gb_) + _mm_tn(vs_, ke_) - _mm_tn(u1_, be_),
                  s_mat, cl_end, gb, vs, ke, u1, be)

    mean = [_mm_sel_r(x, e_seg) * (1.0 / HEAD) for x in o]
    d = _each(jnp.subtract, o, mean)
    var = [_mm_sel_r(x * x, e_seg) * (1.0 / HEAD) for x in d]
    bonus = _each(lambda r_, k_, rk_, v_: _mm_sel_r(r_ * k_ * rk_, e_seg) * v_, r, k, rk_p, v)
    out = _each(lambda d_, var_, gg, gb_, bo, g_: (d_ * lax.rsqrt(var_ + GN_EPS) * gg + gb_ + bo) * g_,
                d, var, gng, gnb, bonus, g)
    return s_new, out


def _wkv_kernel(r_ref, k_ref, v_ref, a_ref, ld_ref, g_ref, prm_ref, s0_ref, *rest, chunk, n_chunks, n_pairs):
    o_ref, s_out_ref, s_scr = rest[-3:]
    tb = pl.program_id(2)

    @pl.when(tb == 0)
    def _():
        s_scr[...] = s0_ref[0]

    c2 = 2 * chunk
    lane = lax.broadcasted_iota(jnp.int32, (1, LANES), 1)
    m_a = lane < HEAD
    er = lax.broadcasted_iota(jnp.int32, (LANES, LANES), 0)
    ec = lax.broadcasted_iota(jnp.int32, (LANES, LANES), 1)
    e_seg = ((er // HEAD) == (ec // HEAD)).astype(BF16)
    tr = lax.broadcasted_iota(jnp.int32, (chunk, chunk), 0)
    tc = lax.broadcasted_iota(jnp.int32, (chunk, chunk), 1)
    tri = (tr >= tc).astype(BF16)
    row = lax.broadcasted_iota(jnp.int32, (c2, c2), 0)
    col = lax.broadcasted_iota(jnp.int32, (c2, c2), 1)
    same = (row // chunk) == (col // chunk)
    strict = same & (col < row)
    incl = same & (col <= row)
    eye = (row == col).astype(F32)
    cst = (m_a, e_seg, tri, eye, row, col, strict, incl)

    def body(ci, carry):
        sl = pl.ds(pl.multiple_of(ci * chunk, chunk), chunk)
        lanes = [slice(p * LANES, (p + 1) * LANES) for p in range(n_pairs)]
        prm = tuple([prm_ref[i:i + 1, ln] for ln in lanes] for i in range(5))
        seqs = [[ref[sl, ln] for ln in lanes] for ref in (r_ref, k_ref, v_ref, a_ref, ld_ref, g_ref)]
        s_new, out = _wkv_chunk([s_scr[p] for p in range(n_pairs)], *seqs, prm, cst, chunk)
        for p in range(n_pairs):
            s_scr[p] = s_new[p]
            o_ref[sl, lanes[p]] = out[p]
        return carry

    lax.fori_loop(0, n_chunks, body, 0)

    @pl.when(tb == pl.num_programs(2) - 1)
    def _():
        s_out_ref[0] = s_scr[...]


def wkv_scan(seqs, prm, s0, *, row0, t_len, chunk, t_block, n_pairs=PAIRS, out=None):
    m_rows, d = seqs[0].shape
    bsz = s0.shape[0]
    assert d == D_MODEL and t_len % t_block == 0 and t_block % chunk == 0 and row0 % t_block == 0
    assert PAIRS % n_pairs == 0 and row0 + bsz * t_len <= m_rows
    width = n_pairs * LANES
    nt = t_len // t_block
    blk0 = row0 // t_block
    seq = pl.BlockSpec((t_block, width), lambda b, p, t: (blk0 + b * nt + t, p))
    st = pl.BlockSpec((1, n_pairs, LANES, LANES), lambda b, p, t: (b, p, 0, 0))
    in_specs = [seq] * 6 + [pl.BlockSpec((5, width), lambda b, p, t: (0, p)), st]
    args = list(seqs) + [prm, s0]
    aliases = {}
    if out is not None:
        in_specs.append(pl.BlockSpec(memory_space=pl.ANY))
        args.append(out)
        aliases = {len(args) - 1: 0}
    return pl.pallas_call(
        functools.partial(_wkv_kernel, chunk=chunk, n_chunks=t_block // chunk, n_pairs=n_pairs),
        grid=(bsz, PAIRS // n_pairs, nt),
        in_specs=in_specs,
        out_specs=[seq, st],
        out_shape=[jax.ShapeDtypeStruct((m_rows, d), F32),
                   jax.ShapeDtypeStruct((bsz, PAIRS, LANES, LANES), F32)],
        scratch_shapes=[pltpu.VMEM((n_pairs, LANES, LANES), F32)],
        input_output_aliases=aliases,
        compiler_params=_cparams("parallel", "parallel", "arbitrary"),
        name="wkv_scan",
    )(*args)


def pair_states(s):
    bsz = s.shape[0]
    s = s.reshape(bsz, PAIRS, 2, HEAD, HEAD)
    z = jnp.zeros_like(s[:, :, 0])
    top = jnp.concatenate([s[:, :, 0], z], axis=-1)
    bot = jnp.concatenate([z, s[:, :, 1]], axis=-1)
    return jnp.concatenate([top, bot], axis=-2)


def unpair_states(sp):
    bsz = sp.shape[0]
    s = jnp.stack([sp[:, :, :HEAD, :HEAD], sp[:, :, HEAD:, HEAD:]], axis=2)
    return s.reshape(bsz, N_HEADS, HEAD, HEAD)


def _a_proj_kernel(*refs, has_vres):
    if has_vres:
        (x_ref, xp_ref, vf_ref, mu_ref, vec_ref, wr, wk, wv, w1, w2, a1, a2, g1, g2, v1, v2,
         r_o, k_o, v_o, a_o, ld_o, g_o) = refs
    else:
        (x_ref, xp_ref, mu_ref, vec_ref, wr, wk, wv, w1, w2, a1, a2, g1, g2,
         r_o, k_o, v_o, a_o, ld_o, g_o) = refs
    x = x_ref[...]
    xx = xp_ref[...] - x
    xr, xw, xk, xv, xa, xg = [(x + xx * mu_ref[i:i + 1, :]).astype(BF16) for i in range(6)]
    r_o[...] = _mm(xr, wr[...])
    k_o[...] = _mm(xk, wk[...])
    v = _mm(xv, wv[...])
    z = vec_ref[0:1, :] + _mm(jnp.tanh(_mm(xw, w1[...])), w2[...])
    softplus_neg = jnp.maximum(-z, 0.0) + jnp.log(1.0 + jnp.exp(-jnp.abs(z)))
    ld_o[...] = -jnp.exp(-softplus_neg - 0.5)
    if has_vres:
        mix = _sigmoid(vec_ref[2:3, :] + _mm(_mm(xv, v1[...]), v2[...]))
        v = v + (vf_ref[...] - v) * mix
    v_o[...] = v
    a_o[...] = _sigmoid(vec_ref[1:2, :] + _mm(_mm(xa, a1[...]), a2[...]))
    g_o[...] = _mm(_sigmoid(_mm(xg, g1[...])), g2[...])


def a_proj(x, x_prev, v_first, mu, vec, mats, *, tm):
    m_rows = x.shape[0]
    assert m_rows % tm == 0
    tok = pl.BlockSpec((tm, D_MODEL), lambda i: (i, 0))
    full = lambda a: pl.BlockSpec(a.shape, lambda i: (0, 0))
    has_vres = v_first is not None
    acts = [x, x_prev] + ([v_first] if has_vres else [])
    consts = [mu, vec] + list(mats)
    return pl.pallas_call(
        functools.partial(_a_proj_kernel, has_vres=has_vres),
        grid=(m_rows // tm,),
        in_specs=[tok] * len(acts) + [full(c) for c in consts],
        out_specs=[tok] * 6,
        out_shape=[jax.ShapeDtypeStruct((m_rows, D_MODEL), F32)] * 6,
        compiler_params=_cparams("parallel"),
        name="a_proj",
    )(*acts, *consts)


def _post_kernel(y_ref, x_ref, wo_ref, ln_ref, wrh_ref, wrl_ref, br_ref, x1_ref, x1b_ref, lg_ref):
    y = _mm(y_ref[...], wo_ref[...])
    x1 = _layer_norm(DN_ALPHA * x_ref[...] + y, ln_ref[0:1, :], ln_ref[1:2, :])
    x1_ref[...] = x1
    x1b_ref[...] = x1.astype(BF16)
    parts = _split(x1, 3)
    acc = br_ref[...]
    for h in parts:
        acc = acc + jnp.dot(h, wrh_ref[...], preferred_element_type=F32)
    for h in parts[:2]:
        acc = acc + jnp.dot(h, wrl_ref[...], preferred_element_type=F32)
    lg_ref[...] = acc


def post_mix(y, x, wo, ln, wr_hi, wr_lo, br, *, tm):
    m_rows = x.shape[0]
    tok = pl.BlockSpec((tm, D_MODEL), lambda i: (i, 0))
    full = lambda a: pl.BlockSpec(a.shape, lambda i: (0, 0))
    return pl.pallas_call(
        _post_kernel,
        grid=(m_rows // tm,),
        in_specs=[tok, tok] + [full(c) for c in (wo, ln, wr_hi, wr_lo, br)],
        out_specs=[tok, tok, pl.BlockSpec((tm, LANES), lambda i: (i, 0))],
        out_shape=[jax.ShapeDtypeStruct((m_rows, D_MODEL), F32),
                   jax.ShapeDtypeStruct((m_rows, D_MODEL), BF16),
                   jax.ShapeDtypeStruct((m_rows, LANES), F32)],
        compiler_params=_cparams("parallel"),
        name="post_mix",
    )(y, x, wo, ln, wr_hi, wr_lo, br)


def _moe_kernel(be_ref, first_ref, nact_ref, xb_ref, win_ref, bin_ref, wout_ref, bout_ref, y_ref, win_s, wout_s):
    i = pl.program_id(0)
    slab = 128

    @pl.when(first_ref[i] == 1)
    def _():
        for j in range(D_MODEL // slab):
            rows = slice(j * slab, (j + 1) * slab)
            win_s[rows, :] = win_ref[0, rows, :].astype(BF16)
            wout_s[rows, :] = wout_ref[0, rows, :].astype(BF16)

    @pl.when(i < nact_ref[0])
    def _():
        h = jnp.dot(xb_ref[...], win_s[...], preferred_element_type=F32) + bin_ref[0]
        h_gate = jnp.minimum(h[:, :D_MODEL], SWIGLU_LIMIT)
        h_up = jnp.clip(h[:, D_MODEL:], -SWIGLU_LIMIT, SWIGLU_LIMIT)
        act = (h_up + 1.0) * h_gate * _sigmoid(SWIGLU_ALPHA * h_gate)
        y_ref[...] = jnp.dot(act.astype(BF16), wout_s[...], preferred_element_type=F32) + bout_ref[0]


def moe_experts(block_e, first, n_act, xb, w_in, b_in, w_out, b_out):
    rows = xb.shape[0]
    n_blocks = rows // MOE_ROWS
    grid_spec = pltpu.PrefetchScalarGridSpec(
        num_scalar_prefetch=3,
        grid=(n_blocks,),
        in_specs=[
            pl.BlockSpec((MOE_ROWS, D_MODEL), lambda i, be, fi, na: (i, 0)),
            pl.BlockSpec((1, D_MODEL, 2 * D_MODEL), lambda i, be, fi, na: (be[i], 0, 0)),
            pl.BlockSpec((1, 1, 2 * D_MODEL), lambda i, be, fi, na: (be[i], 0, 0)),
            pl.BlockSpec((1, D_MODEL, D_MODEL), lambda i, be, fi, na: (be[i], 0, 0)),
            pl.BlockSpec((1, 1, D_MODEL), lambda i, be, fi, na: (be[i], 0, 0)),
        ],
        out_specs=pl.BlockSpec((MOE_ROWS, D_MODEL), lambda i, be, fi, na: (i, 0)),
        scratch_shapes=[pltpu.VMEM((D_MODEL, 2 * D_MODEL), BF16), pltpu.VMEM((D_MODEL, D_MODEL), BF16)],
    )
    return pl.pallas_call(
        _moe_kernel,
        grid_spec=grid_spec,
        out_shape=jax.ShapeDtypeStruct((rows, D_MODEL), F32),
        compiler_params=_cparams("arbitrary"),
        name="moe_experts",
    )(block_e, first, n_act, xb, w_in, b_in.reshape(N_EXPERTS, 1, -1), w_out, b_out.reshape(N_EXPERTS, 1, -1))


def _combine_kernel(yg_ref, gate_ref, x_ref, ln_ref, o_ref):
    gate = gate_ref[...]
    ffn = gate[:, 0:1] * yg_ref[:, 0:D_MODEL]
    for k in range(1, TOP_K):
        ffn = ffn + gate[:, k:k + 1] * yg_ref[:, k * D_MODEL:(k + 1) * D_MODEL]
    o_ref[...] = _layer_norm(DN_ALPHA * x_ref[...] + ffn, ln_ref[0:1, :], ln_ref[1:2, :])


def moe_combine(yg, gate, x, ln, *, tm):
    m_rows = x.shape[0]
    tok = pl.BlockSpec((tm, D_MODEL), lambda i: (i, 0))
    return pl.pallas_call(
        _combine_kernel,
        grid=(m_rows // tm,),
        in_specs=[pl.BlockSpec((tm, TOP_K * D_MODEL), lambda i: (i, 0)),
                  pl.BlockSpec((tm, TOP_K), lambda i: (i, 0)), tok,
                  pl.BlockSpec(ln.shape, lambda i: (0, 0))],
        out_specs=tok,
        out_shape=jax.ShapeDtypeStruct((m_rows, D_MODEL), F32),
        compiler_params=_cparams("parallel"),
        name="moe_combine",
    )(yg, gate, x, ln)


def _route(logits, m_rows):
    top_logit, top_e = lax.top_k(logits[:, :N_EXPERTS], TOP_K)
    gate = jax.nn.softmax(top_logit, axis=-1)
    mk = m_rows * TOP_K
    flat_e = top_e.reshape(-1).astype(jnp.int32)
    onehot = (flat_e[:, None] == jnp.arange(N_EXPERTS, dtype=jnp.int32)[None, :]).astype(jnp.int32)
    csum = jnp.cumsum(onehot, axis=0)
    counts = csum[-1]
    rank = jnp.sum((csum - onehot) * onehot, axis=1)
    padded = (counts + MOE_ROWS - 1) // MOE_ROWS * MOE_ROWS
    pad_end = jnp.cumsum(padded)
    pad_start = pad_end - padded
    start = jnp.cumsum(counts) - counts
    pos = jnp.sum(onehot * pad_start[None, :], axis=1) + rank
    order = jnp.argsort(flat_e).astype(jnp.int32)
    n_blocks = -(-mk // MOE_ROWS) + N_EXPERTS
    blk_start = jnp.arange(n_blocks, dtype=jnp.int32) * MOE_ROWS
    block_e = jnp.minimum(jnp.searchsorted(pad_end, blk_start, side='right'), N_EXPERTS - 1).astype(jnp.int32)
    first = ((blk_start == pad_start[block_e]) & (blk_start < pad_end[-1])).astype(jnp.int32)
    n_act = (pad_end[-1:] // MOE_ROWS).astype(jnp.int32)
    e_row = jnp.repeat(block_e, MOE_ROWS)
    j_row = jnp.arange(n_blocks * MOE_ROWS, dtype=jnp.int32) - pad_start[e_row]
    compact = jnp.clip(start[e_row] + j_row, 0, mk - 1)
    row_tok = jnp.where(j_row < counts[e_row], order[compact] // TOP_K, m_rows)
    return gate, row_tok, pos, block_e, first, n_act


def moe_layer(x1, x1b, logits, w_in, b_in, w_out, b_out, ln, *, tm):
    m_rows = x1.shape[0]
    gate, row_tok, pos, block_e, first, n_act = _route(logits, m_rows)
    x_pad = jnp.concatenate([x1b, jnp.zeros((1, D_MODEL), BF16)], axis=0)
    xb = x_pad[row_tok]
    yb = moe_experts(block_e, first, n_act, xb, w_in, b_in, w_out, b_out)
    yg = yb[pos].reshape(m_rows, TOP_K * D_MODEL)
    return moe_combine(yg, gate, x1, ln, tm=tm)


def _dense_kernel(x_ref, w_ref, o_ref):
    o_ref[...] = _mm(x_ref[...], w_ref[...]).astype(o_ref.dtype)


def dense(x, w, *, tm, tn, out_dtype=F32):
    m_rows, k_dim = x.shape
    n_dim = w.shape[1]
    assert m_rows % tm == 0 and n_dim % tn == 0
    return pl.pallas_call(
        _dense_kernel,
        grid=(n_dim // tn, m_rows // tm),
        in_specs=[pl.BlockSpec((tm, k_dim), lambda j, i: (i, 0)),
                  pl.BlockSpec((k_dim, tn), lambda j, i: (0, j))],
        out_specs=pl.BlockSpec((tm, tn), lambda j, i: (i, j)),
        out_shape=jax.ShapeDtypeStruct((m_rows, n_dim), out_dtype),
        compiler_params=_cparams("parallel", "parallel"),
        name="dense",
    )(x, w)


def _slope(head):
    return 2.0 ** (-8.0 * (head + 1) / N_HEADS)


def _attn_prompt_kernel(q_ref, kp_ref, kc_ref, vp_ref, vc_ref, o_ref, lse_ref, *, dil):
    n = pl.program_id(2)
    nk = ATT_STEPS
    qi = lax.broadcasted_iota(jnp.int32, (nk, 2 * nk), 0)
    kj = lax.broadcasted_iota(jnp.int32, (nk, 2 * nk), 1)
    delta = qi + nk - kj
    valid = (delta >= 0) & (delta <= nk) & ((kj >= nk) | (n > 0))
    dist = (delta * dil).astype(F32)
    lane = lax.broadcasted_iota(jnp.int32, (1, LANES), 1)
    m_a = lane < HEAD

    def scores(p):
        ln = slice(p * LANES, (p + 1) * LANES)
        q = q_ref[:, ln].astype(F32) * (HEAD ** -0.5)
        k = jnp.concatenate([kp_ref[:, ln], kc_ref[:, ln]], axis=0).astype(BF16)
        return [_mm_nt(jnp.where(m_a, q, 0.0), k), _mm_nt(jnp.where(m_a, 0.0, q), k)]

    s_next = scores(0)
    for p in range(PAIRS):
        s_cur = s_next
        if p + 1 < PAIRS:
            s_next = scores(p + 1)
        ln = slice(p * LANES, (p + 1) * LANES)
        v = jnp.concatenate([vp_ref[:, ln], vc_ref[:, ln]], axis=0).astype(BF16)
        outs, lses = [], []
        for hh in range(2):
            s = jnp.where(valid, s_cur[hh] - _slope(2 * p + hh) * dist, NEG_BIG)
            m = jnp.max(s, axis=-1, keepdims=True)
            e = jnp.exp(s - m)
            l = jnp.sum(e, axis=-1, keepdims=True)
            outs.append(jnp.dot(e.astype(BF16), v, preferred_element_type=F32) / l)
            lses.append(m + jnp.log(l))
        o_ref[:, ln] = jnp.where(m_a, outs[0], outs[1])
        lse_ref[:, ln] = jnp.where(m_a, lses[0], lses[1])


def attn_prompt_group(q, kv, *, group, bsz, seq_len, dil):
    nk = ATT_STEPS
    m_rows = q.shape[0]
    sub_len = seq_len // dil
    assert seq_len % dil == 0 and sub_len % nk == 0 and m_rows % dil == 0
    n_tiles = sub_len // nk
    q2 = q.reshape(m_rows // dil, dil * 3 * D_MODEL)
    kv2 = kv.reshape(m_rows // dil, dil * 2 * D_MODEL)

    def spec(col, back):
        return pl.BlockSpec((nk, D_MODEL), lambda b, r, n: (b * n_tiles + jnp.maximum(n - back, 0), col(r)))

    out = pl.BlockSpec((nk, D_MODEL), lambda b, r, n: (b * n_tiles + n, r))
    o, lse = pl.pallas_call(
        functools.partial(_attn_prompt_kernel, dil=dil),
        grid=(bsz, dil, n_tiles),
        in_specs=[spec(lambda r: 3 * r + group, 0),
                  spec(lambda r: 2 * r, 1), spec(lambda r: 2 * r, 0),
                  spec(lambda r: 2 * r + 1, 1), spec(lambda r: 2 * r + 1, 0)],
        out_specs=[out, out],
        out_shape=[jax.ShapeDtypeStruct((bsz * sub_len, dil * D_MODEL), F32)] * 2,
        compiler_params=_cparams("parallel", "parallel", "arbitrary"),
        name="attn_prompt",
    )(q2, kv2, kv2, kv2, kv2)
    return o.reshape(bsz * seq_len, D_MODEL), lse.reshape(bsz * seq_len, D_MODEL)


def _attn_merge_kernel(o0, l0, o1, l1, o2, l2, out_ref):
    m = jnp.maximum(jnp.maximum(l0[...], l1[...]), l2[...])
    w0 = jnp.exp(l0[...] - m)
    w1 = jnp.exp(l1[...] - m)
    w2 = jnp.exp(l2[...] - m)
    out_ref[...] = (w0 * o0[...] + w1 * o1[...] + w2 * o2[...]) / (w0 + w1 + w2)


def attn_merge(parts, m_rows, *, tm):
    mp = parts[0].shape[0]
    tok = pl.BlockSpec((tm, D_MODEL), lambda i: (i, 0))
    return pl.pallas_call(
        _attn_merge_kernel,
        grid=(mp // tm,),
        in_specs=[tok] * 6,
        out_specs=tok,
        out_shape=jax.ShapeDtypeStruct((m_rows, D_MODEL), F32),
        compiler_params=_cparams("parallel"),
        name="attn_merge",
    )(*parts)


def _attn_sample_kernel(q0_ref, q1_ref, q2_ref, kc_ref, kn_ref, vc_ref, vn_ref, alias_ref, o_ref, *, t_len, kv_buf):
    del alias_ref
    p = pl.program_id(1)
    lane = lax.broadcasted_iota(jnp.int32, (1, LANES), 1)
    m_a = lane < HEAD
    lhs = []
    for q_ref in (q0_ref, q1_ref, q2_ref):
        q = q_ref[...].astype(F32) * (HEAD ** -0.5)
        lhs += [jnp.where(m_a, q, 0.0), jnp.where(m_a, 0.0, q)]
    lhs = jnp.concatenate(lhs, axis=0)
    n_rows = 6 * t_len
    pad = jnp.zeros((LANES - t_len, LANES), F32)
    kn = jnp.concatenate([kn_ref[...], pad], axis=0)
    vn = jnp.concatenate([vn_ref[...], pad], axis=0)
    s_c = _mm_nt(lhs, kc_ref[...])
    s_n = _mm_nt(lhs, kn)

    ri = lax.broadcasted_iota(jnp.int32, (n_rows, 1), 0)
    t = ri % t_len
    grp = ri // (2 * t_len)
    hh = (ri // t_len) % 2
    dmask = jnp.where(grp == 0, GROUPS[0][1] - 1, jnp.where(grp == 1, GROUPS[1][1] - 1, GROUPS[2][1] - 1))
    win = jnp.where(grp == 0, GROUPS[0][0], jnp.where(grp == 1, GROUPS[1][0], GROUPS[2][0]))
    head = (2 * p + hh).astype(F32)
    slope = jnp.exp2(-8.0 * (head + 1.0) / N_HEADS)

    def masked(s, dist):
        ok = (dist >= 0) & (dist <= win) & ((dist & dmask) == 0)
        return jnp.where(ok, s - slope * dist.astype(F32), NEG_BIG)

    jc = lax.broadcasted_iota(jnp.int32, (n_rows, kv_buf), 1)
    jn = lax.broadcasted_iota(jnp.int32, (n_rows, LANES), 1)
    s_c = masked(s_c, kv_buf + t - jc)
    s_n = masked(s_n, t - jn)
    m = jnp.maximum(jnp.max(s_c, axis=-1, keepdims=True), jnp.max(s_n, axis=-1, keepdims=True))
    e_c = jnp.exp(s_c - m)
    e_n = jnp.exp(s_n - m)
    l = jnp.sum(e_c, axis=-1, keepdims=True) + jnp.sum(e_n, axis=-1, keepdims=True)
    acc = _mm(e_c, vc_ref[...]) + _mm(e_n, vn)

    blk = 2 * t_len
    m_g = [m[g * blk:(g + 1) * blk] for g in range(3)]
    m_all = jnp.maximum(jnp.maximum(m_g[0], m_g[1]), m_g[2])
    num = 0.0
    den = 0.0
    for g in range(3):
        w = jnp.exp(m_g[g] - m_all)
        num = num + w * acc[g * blk:(g + 1) * blk]
        den = den + w * l[g * blk:(g + 1) * blk]
    res = num / den
    o_ref[...] = jnp.where(m_a, res[:t_len], res[t_len:])


def attn_sample(q, kv, cache_k, cache_v, out, *, row0, bsz, t_len):
    kv_buf = cache_k.shape[1]
    assert kv_buf >= GROUPS[-1][0] and kv_buf % LANES == 0 and row0 % t_len == 0 and t_len % 8 == 0
    blk0 = row0 // t_len
    qs = [pl.BlockSpec((t_len, LANES), lambda b, p, g=g: (blk0 + b, g * PAIRS + p)) for g in range(3)]
    cache = pl.BlockSpec((None, kv_buf, LANES), lambda b, p: (b, 0, p))
    k_new = pl.BlockSpec((t_len, LANES), lambda b, p: (blk0 + b, p))
    v_new = pl.BlockSpec((t_len, LANES), lambda b, p: (blk0 + b, PAIRS + p))
    return pl.pallas_call(
        functools.partial(_attn_sample_kernel, t_len=t_len, kv_buf=kv_buf),
        grid=(bsz, PAIRS),
        in_specs=qs + [cache, k_new, cache, v_new, pl.BlockSpec(memory_space=pl.ANY)],
        out_specs=pl.BlockSpec((t_len, LANES), lambda b, p: (blk0 + b, p)),
        out_shape=jax.ShapeDtypeStruct(out.shape, F32),
        input_output_aliases={7: 0},
        compiler_params=_cparams("parallel", "parallel"),
        name="attn_sample",
    )(q, q, q, cache_k, kv, cache_v, kv, out)


def _pad_cols(w):
    return jnp.pad(w, ((0, 0), (0, LORA_PAD - w.shape[1]))).astype(BF16)


def _pad_rows(w):
    return jnp.pad(w, ((0, LORA_PAD - w.shape[0]), (0, 0))).astype(BF16)


def kernel(x_prompt, x_sample, state_wkv, state_shift, cache_k, cache_v, ln_g, ln_b, rw_mu, rw_wr, rw_wk, rw_wv,
           rw_wo, rw_w0, rw_w1, rw_w2, rw_a0, rw_a1, rw_a2, rw_v0, rw_v1, rw_v2, rw_g1, rw_g2, rw_kk, rw_ka,
           rw_rk, rw_gn_g, rw_gn_b, kv_w, att_wq, att_wo, moe_wr, moe_br, moe_win, moe_bin, moe_wout, moe_bout):
    bp, seq_len, d = x_prompt.shape
    bs, dec_len, _ = x_sample.shape
    kv_buf = cache_k.shape[1]
    mp = bp * seq_len
    ms = bs * dec_len
    m_rows = mp + ms
    tm = TOKEN_TILE
    assert d == D_MODEL and m_rows % tm == 0 and mp % tm == 0
    t_block = min(SCAN_T_BLOCK, seq_len)

    x = jnp.concatenate([x_prompt.reshape(mp, d), x_sample.reshape(ms, d)], axis=0)
    wkv_p, wkv_s, shift_p, shift_s = [], [], [], []
    v_first = None
    kv = None

    def moe(layer, x1, x1b, logits):
        return moe_layer(x1, x1b, logits, moe_win[layer], moe_bin[layer], moe_wout[layer], moe_bout[layer],
                         jnp.stack([ln_g[layer, 1], ln_b[layer, 1]]), tm=tm)

    def post(layer, y, x_in, wo):
        wr = moe_wr[layer]
        wr_hi = wr.astype(BF16)
        wr_lo = (wr - wr_hi.astype(F32)).astype(BF16)
        padc = lambda w: jnp.pad(w, ((0, 0), (0, LANES - N_EXPERTS)))
        br = jnp.pad(moe_br[layer], (0, LANES - N_EXPERTS)).reshape(1, LANES)
        return post_mix(y, x_in, wo.astype(BF16), jnp.stack([ln_g[layer, 0], ln_b[layer, 0]]),
                        padc(wr_hi), padc(wr_lo), br, tm=tm)

    for layer in range(DEPTH):
        if layer < N_A_LAYERS:
            i = layer
            xp3 = x[:mp].reshape(bp, seq_len, d)
            xs3 = x[mp:].reshape(bs, dec_len, d)
            shift_p.append(xp3[:, -1])
            shift_s.append(xs3[:, -1])
            prev_p = jnp.concatenate([jnp.zeros((bp, 1, d), F32), xp3[:, :-1]], axis=1)
            prev_s = jnp.concatenate([state_shift[i][:, None, :], xs3[:, :-1]], axis=1)
            x_prev = jnp.concatenate([prev_p.reshape(mp, d), prev_s.reshape(ms, d)], axis=0)
            vec = jnp.stack([rw_w0[i], rw_a0[i], rw_v0[i - 1] if i > 0 else jnp.zeros((d,), F32)])
            mats = [rw_wr[i].astype(BF16), rw_wk[i].astype(BF16), rw_wv[i].astype(BF16),
                    _pad_cols(rw_w1[i]), _pad_rows(rw_w2[i]), _pad_cols(rw_a1[i]), _pad_rows(rw_a2[i]),
                    _pad_cols(rw_g1[i]), _pad_rows(rw_g2[i])]
            if i > 0:
                mats += [_pad_cols(rw_v1[i - 1]), _pad_rows(rw_v2[i - 1])]
            r, k, v, a, ld, g = a_proj(x, x_prev, v_first, rw_mu[i], vec, mats, tm=tm)
            if i == 0:
                v_first = v
            prm = jnp.stack([rw_kk[i], rw_ka[i], rw_rk[i].reshape(d), rw_gn_g[i], rw_gn_b[i]])
            seqs = (r, k, v, a, ld, g)
            y, sp = wkv_scan(seqs, prm, jnp.zeros((bp, PAIRS, LANES, LANES), F32), row0=0, t_len=seq_len,
                             chunk=SCAN_CHUNK, t_block=t_block)
            y, ss = wkv_scan(seqs, prm, pair_states(state_wkv[i]), row0=mp, t_len=dec_len,
                             chunk=dec_len, t_block=dec_len, out=y)
            wkv_p.append(unpair_states(sp))
            wkv_s.append(unpair_states(ss))
            x1, x1b, logits = post(layer, y, x, rw_wo[i])
        else:
            j = layer - N_A_LAYERS
            q = dense(x, att_wq[j].astype(BF16), tm=tm, tn=D_MODEL)
            parts = []
            for gi, (window, dil) in enumerate(GROUPS):
                assert window // dil == ATT_STEPS
                parts += list(attn_prompt_group(q, kv, group=gi, bsz=bp, seq_len=seq_len, dil=dil))
            y = attn_merge(parts, m_rows, tm=tm)
            y = attn_sample(q, kv, cache_k.reshape(bs, kv_buf, d), cache_v.reshape(bs, kv_buf, d), y,
                            row0=mp, bsz=bs, t_len=dec_len)
            x1, x1b, logits = post(layer, y, x, att_wo[j])
        x = moe(layer, x1, x1b, logits)
        if layer == N_A_LAYERS - 1:
            kv = dense(x, kv_w.astype(BF16), tm=tm, tn=D_MODEL)

    buf_p = min(GROUPS[-1][0], seq_len)
    heads = lambda t, n: t.reshape(t.shape[0], n, N_HEADS, HEAD)
    kv_p = kv[:mp].reshape(bp, seq_len, 2 * d)[:, -buf_p:]
    kv_s = kv[mp:].reshape(bs, dec_len, 2 * d)
    k_p_out = heads(kv_p[:, :, :d], buf_p)
    v_p_out = heads(kv_p[:, :, d:], buf_p)
    k_s_out = jnp.concatenate([cache_k, heads(kv_s[:, :, :d], dec_len)], axis=1)[:, -kv_buf:]
    v_s_out = jnp.concatenate([cache_v, heads(kv_s[:, :, d:], dec_len)], axis=1)[:, -kv_buf:]
    return (x[:mp].reshape(bp, seq_len, d), x[mp:].reshape(bs, dec_len, d),
            jnp.stack(wkv_p), jnp.stack(shift_p), k_p_out, v_p_out,
            jnp.stack(wkv_s), jnp.stack(shift_s), k_s_out, v_s_out)
```

```python
import functools

import jax
import jax.numpy as jnp
from jax import lax
from jax.experimental import pallas as pl
from jax.experimental.pallas import tpu as pltpu

F32 = jnp.float32
BF16 = jnp.bfloat16

D_MODEL = 1024
HEAD = 64
N_HEADS = D_MODEL // HEAD
LANES = 128
PAIRS = D_MODEL // LANES
DEPTH = 4
N_A_LAYERS = DEPTH // 2
LORA_PAD = 128
GN_EPS = 64e-5
LN_EPS = 1e-5
DN_ALPHA = (2 * DEPTH) ** 0.25
GROUPS = ((128, 1), (512, 4), (2048, 16))
ATT_STEPS = 128
N_EXPERTS = 32
TOP_K = 4
SWIGLU_LIMIT = 7.0
SWIGLU_ALPHA = 1.702
MOE_ROWS = 256
TOKEN_TILE = 256
SCAN_CHUNK = 64
SCAN_T_BLOCK = 256
NEG_BIG = -1e30
VMEM_LIMIT = 56 * 1024 * 1024


def _cparams(*sem):
    return pltpu.CompilerParams(dimension_semantics=sem, vmem_limit_bytes=VMEM_LIMIT)


def _mm(a, b):
    return jnp.dot(a.astype(BF16), b.astype(BF16), preferred_element_type=F32)


def _mm_nt(a, b):
    return lax.dot_general(a.astype(BF16), b.astype(BF16), (((1,), (1,)), ((), ())),
                           preferred_element_type=F32)


def _mm_tn(a, b):
    return lax.dot_general(a.astype(BF16), b.astype(BF16), (((0,), (0,)), ((), ())),
                           preferred_element_type=F32)


def _split(x, parts):
    out = []
    for _ in range(parts):
        h = x.astype(BF16)
        out.append(h)
        x = x - h.astype(F32)
    return out


def _mm_sel_r(x, sel, parts=2):
    acc = None
    for h in _split(x, parts):
        t = jnp.dot(h, sel, preferred_element_type=F32)
        acc = t if acc is None else acc + t
    return acc


def _mm_sel_l(sel, x, parts=3):
    acc = None
    for h in _split(x, parts):
        t = jnp.dot(sel, h, preferred_element_type=F32)
        acc = t if acc is None else acc + t
    return acc


def _sigmoid(x):
    return 1.0 / (1.0 + jnp.exp(-x))


def _layer_norm(x, g, b):
    mu = jnp.mean(x, axis=-1, keepdims=True)
    xc = x - mu
    var = jnp.mean(xc * xc, axis=-1, keepdims=True)
    return xc * lax.rsqrt(var + LN_EPS) * g + b


def _each(fn, *lists):
    return [fn(*xs) for xs in zip(*lists)]


def _unit_lower_inverse(a_mats, eye, row, col, chunk):
    blk = (row // 8) == (col // 8)
    x = [jnp.where(blk, -a, 0.0) for a in a_mats]
    x2 = _each(_mm, x, x)
    x4 = _each(_mm, x2, x2)
    xx2 = _each(_mm, x, x2)
    y = _each(lambda x_, x2_, xx2_: eye + x_ + x2_ + xx2_, x, x2, xx2)
    yx4 = _each(_mm, y, x4)
    t = _each(jnp.add, y, yx4)
    s = 8
    while s < chunk:
        rb = row // s
        off = (rb == (col // s) + 1) & ((rb % 2) == 1)
        a_off = [jnp.where(off, a, 0.0) for a in a_mats]
        at = _each(_mm, a_off, t)
        tat = _each(_mm, t, at)
        t = _each(jnp.subtract, t, tat)
        s *= 2
    return t


def _wkv_chunk(s_mat, r, kr, v, a, ld, g, prm, cst, chunk):
    kk_p, ka_p, rk_p, gng, gnb = prm
    m_a, e_seg, tri, eye, row, col, strict, incl = cst
    c = chunk
    kkr = _each(jnp.multiply, kr, kk_p)
    ss = [_mm_sel_r(x * x, e_seg) for x in kkr]
    kk = _each(lambda x, s_: x / jnp.maximum(jnp.sqrt(s_), 1e-12), kkr, ss)
    k = _each(lambda kr_, a_, ka_: kr_ * (1.0 + (a_ - 1.0) * ka_), kr, a, ka_p)
    b = _each(jnp.multiply, kk, a)
    cl = [_mm_sel_l(tri, x) for x in ld]
    cl_end = [x[c - 1:c, :] for x in cl]

    def stack(x):
        return jnp.concatenate([jnp.where(m_a, x, 0.0), jnp.where(m_a, 0.0, x)], axis=0)

    kkg = _each(lambda kk_, cl_, ld_: stack(kk_ * jnp.exp(cl_ - ld_)), kk, cl, ld)
    rg = _each(lambda r_, cl_: stack(r_ * jnp.exp(cl_)), r, cl)
    g_inv = [jnp.exp(-x) for x in cl]
    bd = _each(lambda b_, gi: stack(b_ * gi), b, g_inv)
    kd = _each(lambda k_, gi: stack(k_ * gi), k, g_inv)
    g_end = _each(lambda ce, cl_: jnp.exp(ce - cl_), cl_end, cl)
    be = _each(lambda b_, ge: stack(b_ * ge), b, g_end)
    ke = _each(lambda k_, ge: stack(k_ * ge), k, g_end)
    vs = [stack(x) for x in v]

    a_mat = [jnp.where(strict, x, 0.0) for x in _each(_mm_nt, kkg, bd)]
    b_mat = [jnp.where(strict, x, 0.0) for x in _each(_mm_nt, kkg, kd)]
    rb = [jnp.where(incl, x, 0.0) for x in _each(_mm_nt, rg, bd)]
    rk = [jnp.where(incl, x, 0.0) for x in _each(_mm_nt, rg, kd)]
    t_inv = _unit_lower_inverse(a_mat, eye, row, col, c)

    bv = _each(_mm, b_mat, vs)
    gu = _each(lambda t_, kkg_, bv_: _mm(t_, jnp.concatenate([kkg_, bv_], axis=1)), t_inv, kkg, bv)
    g_s = [x[:, :LANES] for x in gu]
    u1 = [x[:, LANES:] for x in gu]
    p_s = _each(lambda rg_, rb_, gs_: rg_ - _mm(rb_, gs_), rg, rb, g_s)
    q_s = _each(lambda rk_, vs_, rb_, u1_: _mm(rk_, vs_) - _mm(rb_, u1_), rk, vs, rb, u1)
    p = [x[:c] + x[c:] for x in p_s]
    q = [x[:c] + x[c:] for x in q_s]
    o = _each(lambda p_, s_, q_: _mm_nt(p_, s_) + q_, p, s_mat, q)
    gb = _each(_mm_tn, g_s, be)
    s_new = _each(lambda s_, ce, gb_, vs_, ke_, u1_, be_:
                  s_ * jnp.exp(ce) - _mm(s_, gb_) + _mm_tn(vs_, ke_) - _mm_tn(u1_, be_),
                  s_mat, cl_end, gb, vs, ke, u1, be)

    mean = [_mm_sel_r(x, e_seg) * (1.0 / HEAD) for x in o]
    d = _each(jnp.subtract, o, mean)
    var = [_mm_sel_r(x * x, e_seg) * (1.0 / HEAD) for x in d]
    bonus = _each(lambda r_, k_, rk_, v_: _mm_sel_r(r_ * k_ * rk_, e_seg) * v_, r, k, rk_p, v)
    out = _each(lambda d_, var_, gg, gb_, bo, g_: (d_ * lax.rsqrt(var_ + GN_EPS) * gg + gb_ + bo) * g_,
                d, var, gng, gnb, bonus, g)
    return s_new, out


def _wkv_kernel(r_ref, k_ref, v_ref, a_ref, ld_ref, g_ref, prm_ref, s0_ref, *rest, chunk, n_chunks, n_pairs):
    o_ref, s_out_ref, s_scr = rest[-3:]
    tb = pl.program_id(2)

    @pl.when(tb == 0)
    def _():
        s_scr[...] = s0_ref[0]

    c2 = 2 * chunk
    lane = lax.broadcasted_iota(jnp.int32, (1, LANES), 1)
    m_a = lane < HEAD
    er = lax.broadcasted_iota(jnp.int32, (LANES, LANES), 0)
    ec = lax.broadcasted_iota(jnp.int32, (LANES, LANES), 1)
    e_seg = ((er // HEAD) == (ec // HEAD)).astype(BF16)
    tr = lax.broadcasted_iota(jnp.int32, (chunk, chunk), 0)
    tc = lax.broadcasted_iota(jnp.int32, (chunk, chunk), 1)
    tri = (tr >= tc).astype(BF16)
    row = lax.broadcasted_iota(jnp.int32, (c2, c2), 0)
    col = lax.broadcasted_iota(jnp.int32, (c2, c2), 1)
    same = (row // chunk) == (col // chunk)
    strict = same & (col < row)
    incl = same & (col <= row)
    eye = (row == col).astype(F32)
    cst = (m_a, e_seg, tri, eye, row, col, strict, incl)

    def body(ci, carry):
        sl = pl.ds(pl.multiple_of(ci * chunk, chunk), chunk)
        lanes = [slice(p * LANES, (p + 1) * LANES) for p in range(n_pairs)]
        prm = tuple([prm_ref[i:i + 1, ln] for ln in lanes] for i in range(5))
        seqs = [[ref[sl, ln] for ln in lanes] for ref in (r_ref, k_ref, v_ref, a_ref, ld_ref, g_ref)]
        s_new, out = _wkv_chunk([s_scr[p] for p in range(n_pairs)], *seqs, prm, cst, chunk)
        for p in range(n_pairs):
            s_scr[p] = s_new[p]
            o_ref[sl, lanes[p]] = out[p]
        return carry

    lax.fori_loop(0, n_chunks, body, 0)

    @pl.when(tb == pl.num_programs(2) - 1)
    def _():
        s_out_ref[0] = s_scr[...]


def wkv_scan(seqs, prm, s0, *, row0, t_len, chunk, t_block, n_pairs=PAIRS, out=None):
    m_rows, d = seqs[0].shape
    bsz = s0.shape[0]
    assert d == D_MODEL and t_len % t_block == 0 and t_block % chunk == 0 and row0 % t_block == 0
    assert PAIRS % n_pairs == 0 and row0 + bsz * t_len <= m_rows
    width = n_pairs * LANES
    nt = t_len // t_block
    blk0 = row0 // t_block
    seq = pl.BlockSpec((t_block, width), lambda b, p, t: (blk0 + b * nt + t, p))
    st = pl.BlockSpec((1, n_pairs, LANES, LANES), lambda b, p, t: (b, p, 0, 0))
    in_specs = [seq] * 6 + [pl.BlockSpec((5, width), lambda b, p, t: (0, p)), st]
    args = list(seqs) + [prm, s0]
    aliases = {}
    if out is not None:
        in_specs.append(pl.BlockSpec(memory_space=pl.ANY))
        args.append(out)
        aliases = {len(args) - 1: 0}
    return pl.pallas_call(
        functools.partial(_wkv_kernel, chunk=chunk, n_chunks=t_block // chunk, n_pairs=n_pairs),
        grid=(bsz, PAIRS // n_pairs, nt),
        in_specs=in_specs,
        out_specs=[seq, st],
        out_shape=[jax.ShapeDtypeStruct((m_rows, d), F32),
                   jax.ShapeDtypeStruct((bsz, PAIRS, LANES, LANES), F32)],
        scratch_shapes=[pltpu.VMEM((n_pairs, LANES, LANES), F32)],
        input_output_aliases=aliases,
        compiler_params=_cparams("parallel", "parallel", "arbitrary"),
        name="wkv_scan",
    )(*args)


def pair_states(s):
    bsz = s.shape[0]
    s = s.reshape(bsz, PAIRS, 2, HEAD, HEAD)
    z = jnp.zeros_like(s[:, :, 0])
    top = jnp.concatenate([s[:, :, 0], z], axis=-1)
    bot = jnp.concatenate([z, s[:, :, 1]], axis=-1)
    return jnp.concatenate([top, bot], axis=-2)


def unpair_states(sp):
    bsz = sp.shape[0]
    s = jnp.stack([sp[:, :, :HEAD, :HEAD], sp[:, :, HEAD:, HEAD:]], axis=2)
    return s.reshape(bsz, N_HEADS, HEAD, HEAD)


def _a_proj_kernel(*refs, has_vres):
    if has_vres:
        (x_ref, xp_ref, vf_ref, mu_ref, vec_ref, wr, wk, wv, w1, w2, a1, a2, g1, g2, v1, v2,
         r_o, k_o, v_o, a_o, ld_o, g_o) = refs
    else:
        (x_ref, xp_ref, mu_ref, vec_ref, wr, wk, wv, w1, w2, a1, a2, g1, g2,
         r_o, k_o, v_o, a_o, ld_o, g_o) = refs
    x = x_ref[...]
    xx = xp_ref[...] - x
    xr, xw, xk, xv, xa, xg = [(x + xx * mu_ref[i:i + 1, :]).astype(BF16) for i in range(6)]
    r_o[...] = _mm(xr, wr[...])
    k_o[...] = _mm(xk, wk[...])
    v = _mm(xv, wv[...])
    z = vec_ref[0:1, :] + _mm(jnp.tanh(_mm(xw, w1[...])), w2[...])
    softplus_neg = jnp.maximum(-z, 0.0) + jnp.log(1.0 + jnp.exp(-jnp.abs(z)))
    ld_o[...] = -jnp.exp(-softplus_neg - 0.5)
    if has_vres:
        mix = _sigmoid(vec_ref[2:3, :] + _mm(_mm(xv, v1[...]), v2[...]))
        v = v + (vf_ref[...] - v) * mix
    v_o[...] = v
    a_o[...] = _sigmoid(vec_ref[1:2, :] + _mm(_mm(xa, a1[...]), a2[...]))
    g_o[...] = _mm(_sigmoid(_mm(xg, g1[...])), g2[...])


def a_proj(x, x_prev, v_first, mu, vec, mats, *, tm):
    m_rows = x.shape[0]
    assert m_rows % tm == 0
    tok = pl.BlockSpec((tm, D_MODEL), lambda i: (i, 0))
    full = lambda a: pl.BlockSpec(a.shape, lambda i: (0, 0))
    has_vres = v_first is not None
    acts = [x, x_prev] + ([v_first] if has_vres else [])
    consts = [mu, vec] + list(mats)
    return pl.pallas_call(
        functools.partial(_a_proj_kernel, has_vres=has_vres),
        grid=(m_rows // tm,),
        in_specs=[tok] * len(acts) + [full(c) for c in consts],
        out_specs=[tok] * 6,
        out_shape=[jax.ShapeDtypeStruct((m_rows, D_MODEL), F32)] * 6,
        compiler_params=_cparams("parallel"),
        name="a_proj",
    )(*acts, *consts)


def _post_kernel(y_ref, x_ref, wo_ref, ln_ref, wrh_ref, wrl_ref, br_ref, x1_ref, lg_ref):
    y = _mm(y_ref[...], wo_ref[...])
    x1 = _layer_norm(DN_ALPHA * x_ref[...] + y, ln_ref[0:1, :], ln_ref[1:2, :])
    x1_ref[...] = x1
    parts = _split(x1, 3)
    acc = br_ref[...]
    for h in parts:
        acc = acc + jnp.dot(h, wrh_ref[...], preferred_element_type=F32)
    for h in parts[:2]:
        acc = acc + jnp.dot(h, wrl_ref[...], preferred_element_type=F32)
    lg_ref[...] = acc


def post_mix(y, x, wo, ln, wr_hi, wr_lo, br, *, tm):
    m_rows = x.shape[0]
    tok = pl.BlockSpec((tm, D_MODEL), lambda i: (i, 0))
    full = lambda a: pl.BlockSpec(a.shape, lambda i: (0, 0))
    return pl.pallas_call(
        _post_kernel,
        grid=(m_rows // tm,),
        in_specs=[tok, tok] + [full(c) for c in (wo, ln, wr_hi, wr_lo, br)],
        out_specs=[tok, pl.BlockSpec((tm, LANES), lambda i: (i, 0))],
        out_shape=[jax.ShapeDtypeStruct((m_rows, D_MODEL), F32),
                   jax.ShapeDtypeStruct((m_rows, LANES), F32)],
        compiler_params=_cparams("parallel"),
        name="post_mix",
    )(y, x, wo, ln, wr_hi, wr_lo, br)


def _moe_kernel(be_ref, first_ref, nact_ref, xb_ref, win_ref, bin_ref, wout_ref, bout_ref, y_ref, win_s, wout_s):
    i = pl.program_id(0)
    slab = 128

    @pl.when(first_ref[i] == 1)
    def _():
        for j in range(D_MODEL // slab):
            rows = slice(j * slab, (j + 1) * slab)
            win_s[rows, :] = win_ref[0, rows, :].astype(BF16)
            wout_s[rows, :] = wout_ref[0, rows, :].astype(BF16)

    @pl.when(i < nact_ref[0])
    def _():
        h = jnp.dot(xb_ref[...].astype(BF16), win_s[...], preferred_element_type=F32) + bin_ref[0]
        h_gate = jnp.minimum(h[:, :D_MODEL], SWIGLU_LIMIT)
        h_up = jnp.clip(h[:, D_MODEL:], -SWIGLU_LIMIT, SWIGLU_LIMIT)
        act = (h_up + 1.0) * h_gate * _sigmoid(SWIGLU_ALPHA * h_gate)
        y_ref[...] = jnp.dot(act.astype(BF16), wout_s[...], preferred_element_type=F32) + bout_ref[0]


def moe_experts(block_e, first, n_act, xb, w_in, b_in, w_out, b_out):
    rows = xb.shape[0]
    n_blocks = rows // MOE_ROWS
    grid_spec = pltpu.PrefetchScalarGridSpec(
        num_scalar_prefetch=3,
        grid=(n_blocks,),
        in_specs=[
            pl.BlockSpec((MOE_ROWS, D_MODEL), lambda i, be, fi, na: (i, 0)),
            pl.BlockSpec((1, D_MODEL, 2 * D_MODEL), lambda i, be, fi, na: (be[i], 0, 0)),
            pl.BlockSpec((1, 1, 2 * D_MODEL), lambda i, be, fi, na: (be[i], 0, 0)),
            pl.BlockSpec((1, D_MODEL, D_MODEL), lambda i, be, fi, na: (be[i], 0, 0)),
            pl.BlockSpec((1, 1, D_MODEL), lambda i, be, fi, na: (be[i], 0, 0)),
        ],
        out_specs=pl.BlockSpec((MOE_ROWS, D_MODEL), lambda i, be, fi, na: (i, 0)),
        scratch_shapes=[pltpu.VMEM((D_MODEL, 2 * D_MODEL), BF16), pltpu.VMEM((D_MODEL, D_MODEL), BF16)],
    )
    return pl.pallas_call(
        _moe_kernel,
        grid_spec=grid_spec,
        out_shape=jax.ShapeDtypeStruct((rows, D_MODEL), F32),
        compiler_params=_cparams("arbitrary"),
        name="moe_experts",
    )(block_e, first, n_act, xb, w_in, b_in.reshape(N_EXPERTS, 1, -1), w_out, b_out.reshape(N_EXPERTS, 1, -1))


def _combine_kernel(yg_ref, gate_ref, x_ref, ln_ref, o_ref):
    gate = gate_ref[...]
    ffn = gate[:, 0:1] * yg_ref[0]
    for k in range(1, TOP_K):
        ffn = ffn + gate[:, k:k + 1] * yg_ref[k]
    o_ref[...] = _layer_norm(DN_ALPHA * x_ref[...] + ffn, ln_ref[0:1, :], ln_ref[1:2, :])


def moe_combine(yg, gate, x, ln, *, tm):
    m_rows = x.shape[0]
    tok = pl.BlockSpec((tm, D_MODEL), lambda i: (i, 0))
    return pl.pallas_call(
        _combine_kernel,
        grid=(m_rows // tm,),
        in_specs=[pl.BlockSpec((TOP_K, tm, D_MODEL), lambda i: (0, i, 0)),
                  pl.BlockSpec((tm, TOP_K), lambda i: (i, 0)), tok,
                  pl.BlockSpec(ln.shape, lambda i: (0, 0))],
        out_specs=tok,
        out_shape=jax.ShapeDtypeStruct((m_rows, D_MODEL), F32),
        compiler_params=_cparams("parallel"),
        name="moe_combine",
    )(yg, gate, x, ln)


def _route(logits, m_rows):
    top_logit, top_e = lax.top_k(logits[:, :N_EXPERTS], TOP_K)
    gate = jax.nn.softmax(top_logit, axis=-1)
    mk = m_rows * TOP_K
    flat_e = top_e.reshape(-1).astype(jnp.int32)
    onehot = (flat_e[:, None] == jnp.arange(N_EXPERTS, dtype=jnp.int32)[None, :]).astype(jnp.int32)
    csum = jnp.cumsum(onehot, axis=0)
    counts = csum[-1]
    rank = jnp.sum((csum - onehot) * onehot, axis=1)
    padded = (counts + MOE_ROWS - 1) // MOE_ROWS * MOE_ROWS
    pad_end = jnp.cumsum(padded)
    pad_start = pad_end - padded
    start = jnp.cumsum(counts) - counts
    pos = jnp.sum(onehot * pad_start[None, :], axis=1) + rank
    order = jnp.argsort(flat_e).astype(jnp.int32)
    n_blocks = -(-mk // MOE_ROWS) + N_EXPERTS
    blk_start = jnp.arange(n_blocks, dtype=jnp.int32) * MOE_ROWS
    block_e = jnp.minimum(jnp.searchsorted(pad_end, blk_start, side='right'), N_EXPERTS - 1).astype(jnp.int32)
    first = ((blk_start == pad_start[block_e]) & (blk_start < pad_end[-1])).astype(jnp.int32)
    n_act = (pad_end[-1:] // MOE_ROWS).astype(jnp.int32)
    e_row = jnp.repeat(block_e, MOE_ROWS)
    j_row = jnp.arange(n_blocks * MOE_ROWS, dtype=jnp.int32) - pad_start[e_row]
    compact = jnp.clip(start[e_row] + j_row, 0, mk - 1)
    row_tok = jnp.where(j_row < counts[e_row], order[compact] // TOP_K, 0)
    return gate, row_tok, pos, block_e, first, n_act


def moe_layer(x1, logits, w_in, b_in, w_out, b_out, ln, *, tm):
    m_rows = x1.shape[0]
    gate, row_tok, pos, block_e, first, n_act = _route(logits, m_rows)
    xb = x1[row_tok]
    yb = moe_experts(block_e, first, n_act, xb, w_in, b_in, w_out, b_out)
    yg = yb[pos.reshape(m_rows, TOP_K).T]
    return moe_combine(yg, gate, x1, ln, tm=tm)


def _dense_kernel(x_ref, w_ref, o_ref):
    o_ref[...] = _mm(x_ref[...], w_ref[...]).astype(o_ref.dtype)


def dense(x, w, *, tm, tn, out_dtype=F32):
    m_rows, k_dim = x.shape
    n_dim = w.shape[1]
    assert m_rows % tm == 0 and n_dim % tn == 0
    return pl.pallas_call(
        _dense_kernel,
        grid=(n_dim // tn, m_rows // tm),
        in_specs=[pl.BlockSpec((tm, k_dim), lambda j, i: (i, 0)),
                  pl.BlockSpec((k_dim, tn), lambda j, i: (0, j))],
        out_specs=pl.BlockSpec((tm, tn), lambda j, i: (i, j)),
        out_shape=jax.ShapeDtypeStruct((m_rows, n_dim), out_dtype),
        compiler_params=_cparams("parallel", "parallel"),
        name="dense",
    )(x, w)


def _slope(head):
    return 2.0 ** (-8.0 * (head + 1) / N_HEADS)


def _attn_prompt_kernel(slope_ref, q_ref, kp_ref, kc_ref, vp_ref, vc_ref, o_ref, lse_ref, *, dil, n_pairs):
    lb = pl.program_id(1)
    n = pl.program_id(2)
    nk = ATT_STEPS
    qi = lax.broadcasted_iota(jnp.int32, (nk, 2 * nk), 0)
    kj = lax.broadcasted_iota(jnp.int32, (nk, 2 * nk), 1)
    delta = qi + nk - kj
    valid = (delta >= 0) & (delta <= nk) & ((kj >= nk) | (n > 0))
    dist = (delta * dil).astype(F32)
    lane = lax.broadcasted_iota(jnp.int32, (1, LANES), 1)
    m_a = lane < HEAD

    def scores(item):
        rows, p = item
        ln = slice(p * LANES, (p + 1) * LANES)
        q = q_ref[rows, ln] * (HEAD ** -0.5)
        k = jnp.concatenate([kp_ref[rows, ln], kc_ref[rows, ln]], axis=0).astype(BF16)
        return [_mm_nt(jnp.where(m_a, q, 0.0), k), _mm_nt(jnp.where(m_a, 0.0, q), k)]

    def run(items):
        s_next = scores(items[0])
        for idx, (rows, p) in enumerate(items):
            s_cur = s_next
            if idx + 1 < len(items):
                s_next = scores(items[idx + 1])
            ln = slice(p * LANES, (p + 1) * LANES)
            v = jnp.concatenate([vp_ref[rows, ln], vc_ref[rows, ln]], axis=0).astype(BF16)
            outs, lses = [], []
            for hh in range(2):
                slope = slope_ref[(lb * n_pairs + p) * 2 + hh]
                s = jnp.where(valid, s_cur[hh] - slope * dist, NEG_BIG)
                m = jnp.max(s, axis=-1, keepdims=True)
                e = jnp.exp(s - m)
                l = jnp.sum(e, axis=-1, keepdims=True)
                outs.append(jnp.dot(e.astype(BF16), v, preferred_element_type=F32) / l)
                lses.append(m + jnp.log(l))
            o_ref[rows, ln] = jnp.where(m_a, outs[0], outs[1])
            lse_ref[rows, ln] = jnp.where(m_a, lses[0], lses[1])

    if dil == 1:
        run([(slice(None), p) for p in range(n_pairs)])
    else:
        group = min(dil, ATT_CLASS_UNROLL)

        def body(gi, carry):
            run([(pl.ds(gi * group + u, nk, stride=dil), 0) for u in range(group)])
            return carry

        lax.fori_loop(0, dil // group, body, 0)


ATT_CLASS_UNROLL = 4


def attn_prompt_group(q, kv, *, group, bsz, seq_len, dil):
    tile = ATT_STEPS * dil
    assert seq_len % tile == 0
    n_tiles = seq_len // tile
    n_pairs = PAIRS if dil == 1 else 1
    width = n_pairs * LANES
    n_lb = D_MODEL // width
    slopes = jnp.asarray([_slope(h) for h in range(N_HEADS)], F32)

    def spec(col0, back):
        return pl.BlockSpec((tile, width),
                            lambda b, lb, n, sl: (b * n_tiles + jnp.maximum(n - back, 0), col0 * n_lb + lb))

    out = pl.BlockSpec((tile, width), lambda b, lb, n, sl: (b * n_tiles + n, lb))
    grid_spec = pltpu.PrefetchScalarGridSpec(
        num_scalar_prefetch=1,
        grid=(bsz, n_lb, n_tiles),
        in_specs=[spec(group, 0), spec(0, 1), spec(0, 0), spec(1, 1), spec(1, 0)],
        out_specs=[out, out],
    )
    return pl.pallas_call(
        functools.partial(_attn_prompt_kernel, dil=dil, n_pairs=n_pairs),
        grid_spec=grid_spec,
        out_shape=[jax.ShapeDtypeStruct((bsz * seq_len, D_MODEL), F32)] * 2,
        compiler_params=_cparams("parallel", "parallel", "arbitrary"),
        name="attn_prompt",
    )(slopes, q, kv, kv, kv, kv)


def _attn_merge_kernel(o0, l0, o1, l1, o2, l2, out_ref):
    m = jnp.maximum(jnp.maximum(l0[...], l1[...]), l2[...])
    w0 = jnp.exp(l0[...] - m)
    w1 = jnp.exp(l1[...] - m)
    w2 = jnp.exp(l2[...] - m)
    out_ref[...] = (w0 * o0[...] + w1 * o1[...] + w2 * o2[...]) / (w0 + w1 + w2)


def attn_merge(parts, m_rows, *, tm):
    mp = parts[0].shape[0]
    tok = pl.BlockSpec((tm, D_MODEL), lambda i: (i, 0))
    return pl.pallas_call(
        _attn_merge_kernel,
        grid=(mp // tm,),
        in_specs=[tok] * 6,
        out_specs=tok,
        out_shape=jax.ShapeDtypeStruct((m_rows, D_MODEL), F32),
        compiler_params=_cparams("parallel"),
        name="attn_merge",
    )(*parts)


def _attn_sample_kernel(q0_ref, q1_ref, q2_ref, kc_ref, kn_ref, vc_ref, vn_ref, alias_ref, o_ref, *, t_len, kv_buf):
    del alias_ref
    p = pl.program_id(1)
    lane = lax.broadcasted_iota(jnp.int32, (1, LANES), 1)
    m_a = lane < HEAD
    lhs = []
    for q_ref in (q0_ref, q1_ref, q2_ref):
        q = q_ref[...].astype(F32) * (HEAD ** -0.5)
        lhs += [jnp.where(m_a, q, 0.0), jnp.where(m_a, 0.0, q)]
    lhs = jnp.concatenate(lhs, axis=0)
    n_rows = 6 * t_len
    pad = jnp.zeros((LANES - t_len, LANES), F32)
    kn = jnp.concatenate([kn_ref[...], pad], axis=0)
    vn = jnp.concatenate([vn_ref[...], pad], axis=0)
    s_c = _mm_nt(lhs, kc_ref[...])
    s_n = _mm_nt(lhs, kn)

    ri = lax.broadcasted_iota(jnp.int32, (n_rows, 1), 0)
    t = ri % t_len
    grp = ri // (2 * t_len)
    hh = (ri // t_len) % 2
    dmask = jnp.where(grp == 0, GROUPS[0][1] - 1, jnp.where(grp == 1, GROUPS[1][1] - 1, GROUPS[2][1] - 1))
    win = jnp.where(grp == 0, GROUPS[0][0], jnp.where(grp == 1, GROUPS[1][0], GROUPS[2][0]))
    head = (2 * p + hh).astype(F32)
    slope = jnp.exp2(-8.0 * (head + 1.0) / N_HEADS)

    def masked(s, dist):
        ok = (dist >= 0) & (dist <= win) & ((dist & dmask) == 0)
        return jnp.where(ok, s - slope * dist.astype(F32), NEG_BIG)

    jc = lax.broadcasted_iota(jnp.int32, (n_rows, kv_buf), 1)
    jn = lax.broadcasted_iota(jnp.int32, (n_rows, LANES), 1)
    s_c = masked(s_c, kv_buf + t - jc)
    s_n = masked(s_n, t - jn)
    m = jnp.maximum(jnp.max(s_c, axis=-1, keepdims=True), jnp.max(s_n, axis=-1, keepdims=True))
    e_c = jnp.exp(s_c - m)
    e_n = jnp.exp(s_n - m)
    l = jnp.sum(e_c, axis=-1, keepdims=True) + jnp.sum(e_n, axis=-1, keepdims=True)
    acc = _mm(e_c, vc_ref[...]) + _mm(e_n, vn)

    blk = 2 * t_len
    m_g = [m[g * blk:(g + 1) * blk] for g in range(3)]
    m_all = jnp.maximum(jnp.maximum(m_g[0], m_g[1]), m_g[2])
    num = 0.0
    den = 0.0
    for g in range(3):
        w = jnp.exp(m_g[g] - m_all)
        num = num + w * acc[g * blk:(g + 1) * blk]
        den = den + w * l[g * blk:(g + 1) * blk]
    res = num / den
    o_ref[...] = jnp.where(m_a, res[:t_len], res[t_len:])


def attn_sample(q, kv, cache_k, cache_v, out, *, row0, bsz, t_len):
    kv_buf = cache_k.shape[1]
    assert kv_buf >= GROUPS[-1][0] and kv_buf % LANES == 0 and row0 % t_len == 0 and t_len % 8 == 0
    blk0 = row0 // t_len
    qs = [pl.BlockSpec((t_len, LANES), lambda b, p, g=g: (blk0 + b, g * PAIRS + p)) for g in range(3)]
    cache = pl.BlockSpec((None, kv_buf, LANES), lambda b, p: (b, 0, p))
    k_new = pl.BlockSpec((t_len, LANES), lambda b, p: (blk0 + b, p))
    v_new = pl.BlockSpec((t_len, LANES), lambda b, p: (blk0 + b, PAIRS + p))
    return pl.pallas_call(
        functools.partial(_attn_sample_kernel, t_len=t_len, kv_buf=kv_buf),
        grid=(bsz, PAIRS),
        in_specs=qs + [cache, k_new, cache, v_new, pl.BlockSpec(memory_space=pl.ANY)],
        out_specs=pl.BlockSpec((t_len, LANES), lambda b, p: (blk0 + b, p)),
        out_shape=jax.ShapeDtypeStruct(out.shape, F32),
        input_output_aliases={7: 0},
        compiler_params=_cparams("parallel", "parallel"),
        name="attn_sample",
    )(q, q, q, cache_k, kv, cache_v, kv, out)


def _pad_cols(w):
    return jnp.pad(w, ((0, 0), (0, LORA_PAD - w.shape[1]))).astype(BF16)


def _pad_rows(w):
    return jnp.pad(w, ((0, LORA_PAD - w.shape[0]), (0, 0))).astype(BF16)


def kernel(x_prompt, x_sample, state_wkv, state_shift, cache_k, cache_v, ln_g, ln_b, rw_mu, rw_wr, rw_wk, rw_wv,
           rw_wo, rw_w0, rw_w1, rw_w2, rw_a0, rw_a1, rw_a2, rw_v0, rw_v1, rw_v2, rw_g1, rw_g2, rw_kk, rw_ka,
           rw_rk, rw_gn_g, rw_gn_b, kv_w, att_wq, att_wo, moe_wr, moe_br, moe_win, moe_bin, moe_wout, moe_bout):
    bp, seq_len, d = x_prompt.shape
    bs, dec_len, _ = x_sample.shape
    kv_buf = cache_k.shape[1]
    mp = bp * seq_len
    ms = bs * dec_len
    m_rows = mp + ms
    tm = TOKEN_TILE
    assert d == D_MODEL and m_rows % tm == 0 and mp % tm == 0
    t_block = min(SCAN_T_BLOCK, seq_len)

    x = jnp.concatenate([x_prompt.reshape(mp, d), x_sample.reshape(ms, d)], axis=0)
    wkv_p, wkv_s, shift_p, shift_s = [], [], [], []
    v_first = None
    kv = None

    def moe(layer, x1, logits):
        return moe_layer(x1, logits, moe_win[layer], moe_bin[layer], moe_wout[layer], moe_bout[layer],
                         jnp.stack([ln_g[layer, 1], ln_b[layer, 1]]), tm=tm)

    def post(layer, y, x_in, wo):
        wr = moe_wr[layer]
        wr_hi = wr.astype(BF16)
        wr_lo = (wr - wr_hi.astype(F32)).astype(BF16)
        padc = lambda w: jnp.pad(w, ((0, 0), (0, LANES - N_EXPERTS)))
        br = jnp.pad(moe_br[layer], (0, LANES - N_EXPERTS)).reshape(1, LANES)
        return post_mix(y, x_in, wo.astype(BF16), jnp.stack([ln_g[layer, 0], ln_b[layer, 0]]),
                        padc(wr_hi), padc(wr_lo), br, tm=tm)

    for layer in range(DEPTH):
        if layer < N_A_LAYERS:
            i = layer
            xp3 = x[:mp].reshape(bp, seq_len, d)
            xs3 = x[mp:].reshape(bs, dec_len, d)
            shift_p.append(xp3[:, -1])
            shift_s.append(xs3[:, -1])
            prev_p = jnp.concatenate([jnp.zeros((bp, 1, d), F32), xp3[:, :-1]], axis=1)
            prev_s = jnp.concatenate([state_shift[i][:, None, :], xs3[:, :-1]], axis=1)
            x_prev = jnp.concatenate([prev_p.reshape(mp, d), prev_s.reshape(ms, d)], axis=0)
            vec = jnp.stack([rw_w0[i], rw_a0[i], rw_v0[i - 1] if i > 0 else jnp.zeros((d,), F32)])
            mats = [rw_wr[i].astype(BF16), rw_wk[i].astype(BF16), rw_wv[i].astype(BF16),
                    _pad_cols(rw_w1[i]), _pad_rows(rw_w2[i]), _pad_cols(rw_a1[i]), _pad_rows(rw_a2[i]),
                    _pad_cols(rw_g1[i]), _pad_rows(rw_g2[i])]
            if i > 0:
                mats += [_pad_cols(rw_v1[i - 1]), _pad_rows(rw_v2[i - 1])]
            r, k, v, a, ld, g = a_proj(x, x_prev, v_first, rw_mu[i], vec, mats, tm=tm)
            if i == 0:
                v_first = v
            prm = jnp.stack([rw_kk[i], rw_ka[i], rw_rk[i].reshape(d), rw_gn_g[i], rw_gn_b[i]])
            seqs = (r, k, v, a, ld, g)
            y, sp = wkv_scan(seqs, prm, jnp.zeros((bp, PAIRS, LANES, LANES), F32), row0=0, t_len=seq_len,
                             chunk=SCAN_CHUNK, t_block=t_block)
            y, ss = wkv_scan(seqs, prm, pair_states(state_wkv[i]), row0=mp, t_len=dec_len,
                             chunk=dec_len, t_block=dec_len, out=y)
            wkv_p.append(unpair_states(sp))
            wkv_s.append(unpair_states(ss))
            x1, logits = post(layer, y, x, rw_wo[i])
        else:
            j = layer - N_A_LAYERS
            q = dense(x, att_wq[j].astype(BF16), tm=tm, tn=D_MODEL)
            parts = []
            for gi, (window, dil) in enumerate(GROUPS):
                assert window // dil == ATT_STEPS
                parts += list(attn_prompt_group(q, kv, group=gi, bsz=bp, seq_len=seq_len, dil=dil))
            y = attn_merge(parts, m_rows, tm=tm)
            y = attn_sample(q, kv, cache_k.reshape(bs, kv_buf, d), cache_v.reshape(bs, kv_buf, d), y,
                            row0=mp, bsz=bs, t_len=dec_len)
            x1, logits = post(layer, y, x, att_wo[j])
        x = moe(layer, x1, logits)
        if layer == N_A_LAYERS - 1:
            kv = dense(x, kv_w.astype(BF16), tm=tm, tn=D_MODEL)

    buf_p = min(GROUPS[-1][0], seq_len)
    heads = lambda t, n: t.reshape(t.shape[0], n, N_HEADS, HEAD)
    kv_p = kv[:mp].reshape(bp, seq_len, 2 * d)[:, -buf_p:]
    kv_s = kv[mp:].reshape(bs, dec_len, 2 * d)
    k_p_out = heads(kv_p[:, :, :d], buf_p)
    v_p_out = heads(kv_p[:, :, d:], buf_p)
    k_s_out = jnp.concatenate([cache_k, heads(kv_s[:, :, :d], dec_len)], axis=1)[:, -kv_buf:]
    v_s_out = jnp.concatenate([cache_v, heads(kv_s[:, :, d:], dec_len)], axis=1)[:, -kv_buf:]
    return (x[:mp].reshape(bp, seq_len, d), x[mp:].reshape(bs, dec_len, d),
            jnp.stack(wkv_p), jnp.stack(shift_p), k_p_out, v_p_out,
            jnp.stack(wkv_s), jnp.stack(shift_s), k_s_out, v_s_out)
```

```python
import functools

import jax
import jax.numpy as jnp
from jax import lax
from jax.experimental import pallas as pl
from jax.experimental.pallas import tpu as pltpu

F32 = jnp.float32
BF16 = jnp.bfloat16

D_MODEL = 1024
HEAD = 64
N_HEADS = D_MODEL // HEAD
LANES = 128
PAIRS = D_MODEL // LANES
DEPTH = 4
N_A_LAYERS = DEPTH // 2
LORA_PAD = 128
GN_EPS = 64e-5
LN_EPS = 1e-5
DN_ALPHA = (2 * DEPTH) ** 0.25
GROUPS = ((128, 1), (512, 4), (2048, 16))
ATT_STEPS = 128
N_EXPERTS = 32
TOP_K = 4
SWIGLU_LIMIT = 7.0
SWIGLU_ALPHA = 1.702
MOE_ROWS = 256
TOKEN_TILE = 256
SCAN_CHUNK = 64
SCAN_T_BLOCK = 256
NEG_BIG = -1e30
VMEM_LIMIT = 56 * 1024 * 1024


def _cparams(*sem):
    return pltpu.CompilerParams(dimension_semantics=sem, vmem_limit_bytes=VMEM_LIMIT)


def _mm(a, b):
    return jnp.dot(a.astype(BF16), b.astype(BF16), preferred_element_type=F32)


def _mm_nt(a, b):
    return lax.dot_general(a.astype(BF16), b.astype(BF16), (((1,), (1,)), ((), ())),
                           preferred_element_type=F32)


def _mm_tn(a, b):
    return lax.dot_general(a.astype(BF16), b.astype(BF16), (((0,), (0,)), ((), ())),
                           preferred_element_type=F32)


def _split(x, parts):
    out = []
    for _ in range(parts):
        h = x.astype(BF16)
        out.append(h)
        x = x - h.astype(F32)
    return out


def _mm_sel_r(x, sel, parts=2):
    acc = None
    for h in _split(x, parts):
        t = jnp.dot(h, sel, preferred_element_type=F32)
        acc = t if acc is None else acc + t
    return acc


def _mm_sel_l(sel, x, parts=3):
    acc = None
    for h in _split(x, parts):
        t = jnp.dot(sel, h, preferred_element_type=F32)
        acc = t if acc is None else acc + t
    return acc


def _sigmoid(x):
    return 1.0 / (1.0 + jnp.exp(-x))


def _layer_norm(x, g, b):
    mu = jnp.mean(x, axis=-1, keepdims=True)
    xc = x - mu
    var = jnp.mean(xc * xc, axis=-1, keepdims=True)
    return xc * lax.rsqrt(var + LN_EPS) * g + b


def _each(fn, *lists):
    return [fn(*xs) for xs in zip(*lists)]


def _unit_lower_inverse(a_side, cst, chunk):
    eye, row, scol, bd = cst
    blk = (row // 8) == (scol // 8)

    def mul(xs, ys):
        return _each(_mm, xs, [bd(y) for y in ys])

    x = [jnp.where(blk, -a, 0.0) for a in a_side]
    x2 = mul(x, x)
    x4 = mul(x2, x2)
    xx2 = mul(x, x2)
    y = _each(lambda x_, x2_, xx2_: eye + x_ + x2_ + xx2_, x, x2, xx2)
    t = _each(jnp.add, y, mul(y, x4))
    s = 8
    while s < chunk:
        rb = row // s
        off = (rb == (scol // s) + 1) & ((rb % 2) == 1)
        a_off = [jnp.where(off, a, 0.0) for a in a_side]
        t = _each(jnp.subtract, t, mul(t, mul(a_off, t)))
        s *= 2
    return t


def _wkv_chunk(s_mat, r, kr, v, a, ld, g, prm, cst, chunk):
    kk_p, ka_p, rk_p, gng, gnb = prm
    m_a, e_seg, e_seg2, tri, eye, row, scol, strict, incl, bd = cst
    c = chunk
    kkr = _each(jnp.multiply, kr, kk_p)
    ss = [_mm_sel_r(x * x, e_seg) for x in kkr]
    kk = _each(lambda x, s_: x / jnp.maximum(jnp.sqrt(s_), 1e-12), kkr, ss)
    k = _each(lambda kr_, a_, ka_: kr_ * (1.0 + (a_ - 1.0) * ka_), kr, a, ka_p)
    b = _each(jnp.multiply, kk, a)
    cl = [_mm_sel_l(tri, x) for x in ld]
    cl_end = [x[c - 1:c, :] for x in cl]

    def stack(x):
        return jnp.concatenate([jnp.where(m_a, x, 0.0), jnp.where(m_a, 0.0, x)], axis=0)

    kkg = _each(lambda kk_, cl_, ld_: kk_ * jnp.exp(cl_ - ld_), kk, cl, ld)
    rg = _each(lambda r_, cl_: r_ * jnp.exp(cl_), r, cl)
    g_inv = [jnp.exp(-x) for x in cl]
    bd_s = _each(lambda b_, gi: stack(b_ * gi), b, g_inv)
    kd_s = _each(lambda k_, gi: stack(k_ * gi), k, g_inv)
    g_end = _each(lambda ce, cl_: jnp.exp(ce - cl_), cl_end, cl)
    be_s = _each(lambda b_, ge: stack(b_ * ge), b, g_end)
    ke_s = _each(lambda k_, ge: stack(k_ * ge), k, g_end)
    v_s = [stack(x) for x in v]

    if (2 * c) % LANES == 0:
        bk_s = _each(lambda x_, y_: jnp.concatenate([x_, y_], axis=0), bd_s, kd_s)
        ab = _each(_mm_nt, kkg, bk_s)
        rbk = _each(_mm_nt, rg, bk_s)
        a_side, b_side = [x[:, :2 * c] for x in ab], [x[:, 2 * c:] for x in ab]
        rb_side, rk_side = [x[:, :2 * c] for x in rbk], [x[:, 2 * c:] for x in rbk]
    else:
        a_side, b_side = _each(_mm_nt, kkg, bd_s), _each(_mm_nt, kkg, kd_s)
        rb_side, rk_side = _each(_mm_nt, rg, bd_s), _each(_mm_nt, rg, kd_s)
    a_side = [jnp.where(strict, x, 0.0) for x in a_side]
    b_side = [jnp.where(strict, x, 0.0) for x in b_side]
    rb_side = [jnp.where(incl, x, 0.0) for x in rb_side]
    rk_side = [jnp.where(incl, x, 0.0) for x in rk_side]
    t_side = _unit_lower_inverse(a_side, (eye, row, scol, bd), c)

    bv = _each(_mm, b_side, v_s)
    gu = _each(lambda t_, kkg_, bv_: _mm(t_, jnp.concatenate([stack(kkg_), stack(bv_)], axis=1)), t_side, kkg, bv)
    g_s = [stack(x[:, :LANES]) for x in gu]
    u1_s = [stack(x[:, LANES:]) for x in gu]
    pq = _each(lambda rb_, gs_, u1_: _mm(rb_, jnp.concatenate([gs_, u1_], axis=1)), rb_side, g_s, u1_s)
    rkv = _each(_mm, rk_side, v_s)
    p = _each(lambda rg_, pq_: rg_ - pq_[:, :LANES], rg, pq)
    q = _each(lambda rkv_, pq_: rkv_ - pq_[:, LANES:], rkv, pq)
    o = _each(lambda p_, s_, q_: _mm_nt(p_, s_) + q_, p, s_mat, q)
    gb = _each(_mm_tn, g_s, be_s)
    nt = _each(lambda vs_, u1_, ke_, be_: _mm_tn(jnp.concatenate([vs_, -u1_], axis=0),
                                                   jnp.concatenate([ke_, be_], axis=0)), v_s, u1_s, ke_s, be_s)
    s_new = _each(lambda s_, ce, gb_, nt_: s_ * jnp.exp(ce) - _mm(s_, gb_) + nt_, s_mat, cl_end, gb, nt)

    mb = _each(lambda o_, r_, k_, rk_: _mm_sel_r(jnp.concatenate([o_, r_ * k_ * rk_], axis=1), e_seg2),
               o, r, k, rk_p)
    d = _each(lambda o_, mb_: o_ - mb_[:, :LANES] * (1.0 / HEAD), o, mb)
    var = [_mm_sel_r(x * x, e_seg) * (1.0 / HEAD) for x in d]
    out = _each(lambda d_, var_, gg, gb_, mb_, v_, g_:
                (d_ * lax.rsqrt(var_ + GN_EPS) * gg + gb_ + mb_[:, LANES:] * v_) * g_,
                d, var, gng, gnb, mb, v, g)
    return s_new, out


def _wkv_kernel(r_ref, k_ref, v_ref, a_ref, ld_ref, g_ref, prm_ref, s0_ref, *rest, chunk, n_chunks, n_pairs):
    o_ref, s_out_ref, s_scr = rest[-3:]
    tb = pl.program_id(2)

    @pl.when(tb == 0)
    def _():
        s_scr[...] = s0_ref[0]

    c2 = 2 * chunk
    lane = lax.broadcasted_iota(jnp.int32, (1, LANES), 1)
    m_a = lane < HEAD
    er = lax.broadcasted_iota(jnp.int32, (2 * LANES, 2 * LANES), 0)
    ec = lax.broadcasted_iota(jnp.int32, (2 * LANES, 2 * LANES), 1)
    e_seg2 = ((er // HEAD) == (ec // HEAD)).astype(BF16)
    e_seg = e_seg2[:LANES, :LANES]
    tr = lax.broadcasted_iota(jnp.int32, (chunk, chunk), 0)
    tc = lax.broadcasted_iota(jnp.int32, (chunk, chunk), 1)
    tri = (tr >= tc).astype(BF16)
    row = lax.broadcasted_iota(jnp.int32, (chunk, c2), 0)
    col = lax.broadcasted_iota(jnp.int32, (chunk, c2), 1)
    scol = col % chunk
    strict = scol < row
    incl = scol <= row
    eye = (row == scol).astype(F32)
    left = col < chunk

    def bd(x):
        return jnp.concatenate([jnp.where(left, x, 0.0), jnp.where(left, 0.0, x)], axis=0)

    cst = (m_a, e_seg, e_seg2, tri, eye, row, scol, strict, incl, bd)

    def body(ci, carry):
        sl = pl.ds(pl.multiple_of(ci * chunk, chunk), chunk)
        lanes = [slice(p * LANES, (p + 1) * LANES) for p in range(n_pairs)]
        prm = tuple([prm_ref[i:i + 1, ln] for ln in lanes] for i in range(5))
        seqs = [[ref[sl, ln] for ln in lanes] for ref in (r_ref, k_ref, v_ref, a_ref, ld_ref, g_ref)]
        s_new, out = _wkv_chunk([s_scr[p] for p in range(n_pairs)], *seqs, prm, cst, chunk)
        for p in range(n_pairs):
            s_scr[p] = s_new[p]
            o_ref[sl, lanes[p]] = out[p]
        return carry

    lax.fori_loop(0, n_chunks, body, 0)

    @pl.when(tb == pl.num_programs(2) - 1)
    def _():
        s_out_ref[0] = s_scr[...]


def wkv_scan(seqs, prm, s0, *, row0, t_len, chunk, t_block, n_pairs=PAIRS, out=None):
    m_rows, d = seqs[0].shape
    bsz = s0.shape[0]
    assert d == D_MODEL and t_len % t_block == 0 and t_block % chunk == 0 and row0 % t_block == 0
    assert PAIRS % n_pairs == 0 and row0 + bsz * t_len <= m_rows
    width = n_pairs * LANES
    nt = t_len // t_block
    blk0 = row0 // t_block
    seq = pl.BlockSpec((t_block, width), lambda b, p, t: (blk0 + b * nt + t, p))
    st = pl.BlockSpec((1, n_pairs, LANES, LANES), lambda b, p, t: (b, p, 0, 0))
    in_specs = [seq] * 6 + [pl.BlockSpec((5, width), lambda b, p, t: (0, p)), st]
    args = list(seqs) + [prm, s0]
    aliases = {}
    if out is not None:
        in_specs.append(pl.BlockSpec(memory_space=pl.ANY))
        args.append(out)
        aliases = {len(args) - 1: 0}
    return pl.pallas_call(
        functools.partial(_wkv_kernel, chunk=chunk, n_chunks=t_block // chunk, n_pairs=n_pairs),
        grid=(bsz, PAIRS // n_pairs, nt),
        in_specs=in_specs,
        out_specs=[seq, st],
        out_shape=[jax.ShapeDtypeStruct((m_rows, d), F32),
                   jax.ShapeDtypeStruct((bsz, PAIRS, LANES, LANES), F32)],
        scratch_shapes=[pltpu.VMEM((n_pairs, LANES, LANES), F32)],
        input_output_aliases=aliases,
        compiler_params=_cparams("parallel", "parallel", "arbitrary"),
        name="wkv_scan",
    )(*args)


def pair_states(s):
    bsz = s.shape[0]
    s = s.reshape(bsz, PAIRS, 2, HEAD, HEAD)
    z = jnp.zeros_like(s[:, :, 0])
    top = jnp.concatenate([s[:, :, 0], z], axis=-1)
    bot = jnp.concatenate([z, s[:, :, 1]], axis=-1)
    return jnp.concatenate([top, bot], axis=-2)


def unpair_states(sp):
    bsz = sp.shape[0]
    s = jnp.stack([sp[:, :, :HEAD, :HEAD], sp[:, :, HEAD:, HEAD:]], axis=2)
    return s.reshape(bsz, N_HEADS, HEAD, HEAD)


def _a_proj_kernel(*refs, has_vres):
    if has_vres:
        (x_ref, xp_ref, vf_ref, mu_ref, vec_ref, wr, wk, wv, w1, w2, a1, a2, g1, g2, v1, v2,
         r_o, k_o, v_o, a_o, ld_o, g_o) = refs
    else:
        (x_ref, xp_ref, mu_ref, vec_ref, wr, wk, wv, w1, w2, a1, a2, g1, g2,
         r_o, k_o, v_o, a_o, ld_o, g_o) = refs
    x = x_ref[...]
    xx = xp_ref[...] - x
    xr, xw, xk, xv, xa, xg = [(x + xx * mu_ref[i:i + 1, :]).astype(BF16) for i in range(6)]
    r_o[...] = _mm(xr, wr[...])
    k_o[...] = _mm(xk, wk[...])
    v = _mm(xv, wv[...])
    z = vec_ref[0:1, :] + _mm(jnp.tanh(_mm(xw, w1[...])), w2[...])
    softplus_neg = jnp.maximum(-z, 0.0) + jnp.log(1.0 + jnp.exp(-jnp.abs(z)))
    ld_o[...] = -jnp.exp(-softplus_neg - 0.5)
    if has_vres:
        mix = _sigmoid(vec_ref[2:3, :] + _mm(_mm(xv, v1[...]), v2[...]))
        v = v + (vf_ref[...] - v) * mix
    v_o[...] = v
    a_o[...] = _sigmoid(vec_ref[1:2, :] + _mm(_mm(xa, a1[...]), a2[...]))
    g_o[...] = _mm(_sigmoid(_mm(xg, g1[...])), g2[...])


def a_proj(x, x_prev, v_first, mu, vec, mats, *, tm):
    m_rows = x.shape[0]
    assert m_rows % tm == 0
    tok = pl.BlockSpec((tm, D_MODEL), lambda i: (i, 0))
    full = lambda a: pl.BlockSpec(a.shape, lambda i: (0, 0))
    has_vres = v_first is not None
    acts = [x, x_prev] + ([v_first] if has_vres else [])
    consts = [mu, vec] + list(mats)
    return pl.pallas_call(
        functools.partial(_a_proj_kernel, has_vres=has_vres),
        grid=(m_rows // tm,),
        in_specs=[tok] * len(acts) + [full(c) for c in consts],
        out_specs=[tok] * 6,
        out_shape=[jax.ShapeDtypeStruct((m_rows, D_MODEL), F32)] * 6,
        compiler_params=_cparams("parallel"),
        name="a_proj",
    )(*acts, *consts)


def _post_kernel(y_ref, x_ref, wo_ref, ln_ref, wrh_ref, wrl_ref, br_ref, x1_ref, lg_ref):
    y = _mm(y_ref[...], wo_ref[...])
    x1 = _layer_norm(DN_ALPHA * x_ref[...] + y, ln_ref[0:1, :], ln_ref[1:2, :])
    x1_ref[...] = x1
    parts = _split(x1, 3)
    acc = br_ref[...]
    for h in parts:
        acc = acc + jnp.dot(h, wrh_ref[...], preferred_element_type=F32)
    for h in parts[:2]:
        acc = acc + jnp.dot(h, wrl_ref[...], preferred_element_type=F32)
    lg_ref[...] = acc


def post_mix(y, x, wo, ln, wr_hi, wr_lo, br, *, tm):
    m_rows = x.shape[0]
    tok = pl.BlockSpec((tm, D_MODEL), lambda i: (i, 0))
    full = lambda a: pl.BlockSpec(a.shape, lambda i: (0, 0))
    return pl.pallas_call(
        _post_kernel,
        grid=(m_rows // tm,),
        in_specs=[tok, tok] + [full(c) for c in (wo, ln, wr_hi, wr_lo, br)],
        out_specs=[tok, pl.BlockSpec((tm, LANES), lambda i: (i, 0))],
        out_shape=[jax.ShapeDtypeStruct((m_rows, D_MODEL), F32),
                   jax.ShapeDtypeStruct((m_rows, LANES), F32)],
        compiler_params=_cparams("parallel"),
        name="post_mix",
    )(y, x, wo, ln, wr_hi, wr_lo, br)


def _moe_kernel(be_ref, first_ref, nact_ref, xb_ref, win_ref, bin_ref, wout_ref, bout_ref, y_ref, win_s, wout_s):
    i = pl.program_id(0)
    slab = 128

    @pl.when(first_ref[i] == 1)
    def _():
        for j in range(D_MODEL // slab):
            rows = slice(j * slab, (j + 1) * slab)
            win_s[rows, :] = win_ref[0, rows, :].astype(BF16)
            wout_s[rows, :] = wout_ref[0, rows, :].astype(BF16)

    @pl.when(i < nact_ref[0])
    def _():
        h = jnp.dot(xb_ref[...].astype(BF16), win_s[...], preferred_element_type=F32) + bin_ref[0]
        h_gate = jnp.minimum(h[:, :D_MODEL], SWIGLU_LIMIT)
        h_up = jnp.clip(h[:, D_MODEL:], -SWIGLU_LIMIT, SWIGLU_LIMIT)
        act = (h_up + 1.0) * h_gate * _sigmoid(SWIGLU_ALPHA * h_gate)
        y_ref[...] = jnp.dot(act.astype(BF16), wout_s[...], preferred_element_type=F32) + bout_ref[0]


def moe_experts(block_e, first, n_act, xb, w_in, b_in, w_out, b_out, *, layer):
    rows = xb.shape[0]
    n_blocks = rows // MOE_ROWS
    grid_spec = pltpu.PrefetchScalarGridSpec(
        num_scalar_prefetch=3,
        grid=(n_blocks,),
        in_specs=[
            pl.BlockSpec((MOE_ROWS, D_MODEL), lambda i, be, fi, na: (i, 0)),
            pl.BlockSpec((None, 1, D_MODEL, 2 * D_MODEL), lambda i, be, fi, na: (layer, be[i], 0, 0)),
            pl.BlockSpec((1, 1, 2 * D_MODEL), lambda i, be, fi, na: (be[i], 0, 0)),
            pl.BlockSpec((None, 1, D_MODEL, D_MODEL), lambda i, be, fi, na: (layer, be[i], 0, 0)),
            pl.BlockSpec((1, 1, D_MODEL), lambda i, be, fi, na: (be[i], 0, 0)),
        ],
        out_specs=pl.BlockSpec((MOE_ROWS, D_MODEL), lambda i, be, fi, na: (i, 0)),
        scratch_shapes=[pltpu.VMEM((D_MODEL, 2 * D_MODEL), BF16), pltpu.VMEM((D_MODEL, D_MODEL), BF16)],
    )
    return pl.pallas_call(
        _moe_kernel,
        grid_spec=grid_spec,
        out_shape=jax.ShapeDtypeStruct((rows, D_MODEL), F32),
        compiler_params=_cparams("arbitrary"),
        name="moe_experts",
    )(block_e, first, n_act, xb, w_in, b_in.reshape(N_EXPERTS, 1, -1), w_out, b_out.reshape(N_EXPERTS, 1, -1))


def _combine_kernel(yg_ref, gate_ref, x_ref, ln_ref, o_ref):
    gate = gate_ref[...]
    ffn = gate[:, 0:1] * yg_ref[0]
    for k in range(1, TOP_K):
        ffn = ffn + gate[:, k:k + 1] * yg_ref[k]
    o_ref[...] = _layer_norm(DN_ALPHA * x_ref[...] + ffn, ln_ref[0:1, :], ln_ref[1:2, :])


def moe_combine(yg, gate, x, ln, *, tm):
    m_rows = x.shape[0]
    tok = pl.BlockSpec((tm, D_MODEL), lambda i: (i, 0))
    return pl.pallas_call(
        _combine_kernel,
        grid=(m_rows // tm,),
        in_specs=[pl.BlockSpec((TOP_K, tm, D_MODEL), lambda i: (0, i, 0)),
                  pl.BlockSpec((tm, TOP_K), lambda i: (i, 0)), tok,
                  pl.BlockSpec(ln.shape, lambda i: (0, 0))],
        out_specs=tok,
        out_shape=jax.ShapeDtypeStruct((m_rows, D_MODEL), F32),
        compiler_params=_cparams("parallel"),
        name="moe_combine",
    )(yg, gate, x, ln)


def _route(logits, m_rows):
    top_logit, top_e = lax.top_k(logits[:, :N_EXPERTS], TOP_K)
    gate = jax.nn.softmax(top_logit, axis=-1)
    mk = m_rows * TOP_K
    flat_e = top_e.reshape(-1).astype(jnp.int32)
    onehot = (flat_e[:, None] == jnp.arange(N_EXPERTS, dtype=jnp.int32)[None, :]).astype(jnp.int32)
    csum = jnp.cumsum(onehot, axis=0)
    counts = csum[-1]
    rank = jnp.sum((csum - onehot) * onehot, axis=1)
    padded = (counts + MOE_ROWS - 1) // MOE_ROWS * MOE_ROWS
    pad_end = jnp.cumsum(padded)
    pad_start = pad_end - padded
    start = jnp.cumsum(counts) - counts
    pos = jnp.sum(onehot * pad_start[None, :], axis=1) + rank
    order = jnp.argsort(flat_e).astype(jnp.int32)
    n_blocks = -(-mk // MOE_ROWS) + N_EXPERTS
    blk_start = jnp.arange(n_blocks, dtype=jnp.int32) * MOE_ROWS
    block_e = jnp.sum((pad_end[None, :] <= blk_start[:, None]).astype(jnp.int32), axis=1)
    block_e = jnp.minimum(block_e, N_EXPERTS - 1)
    first = ((blk_start == pad_start[block_e]) & (blk_start < pad_end[-1])).astype(jnp.int32)
    n_act = (pad_end[-1:] // MOE_ROWS).astype(jnp.int32)
    e_row = jnp.repeat(block_e, MOE_ROWS)
    j_row = jnp.arange(n_blocks * MOE_ROWS, dtype=jnp.int32) - pad_start[e_row]
    compact = jnp.clip(start[e_row] + j_row, 0, mk - 1)
    row_tok = jnp.where(j_row < counts[e_row], order[compact] // TOP_K, 0)
    return gate, row_tok, pos, block_e, first, n_act


def moe_layer(x1, logits, w_in, b_in, w_out, b_out, ln, *, tm, layer):
    m_rows = x1.shape[0]
    gate, row_tok, pos, block_e, first, n_act = _route(logits, m_rows)
    xb = x1[row_tok]
    yb = moe_experts(block_e, first, n_act, xb, w_in, b_in, w_out, b_out, layer=layer)
    yg = yb[pos.reshape(m_rows, TOP_K).T]
    return moe_combine(yg, gate, x1, ln, tm=tm)


def _dense_kernel(x_ref, w_ref, o_ref):
    o_ref[...] = _mm(x_ref[...], w_ref[...]).astype(o_ref.dtype)


def dense(x, w, *, tm, tn, out_dtype=F32):
    m_rows, k_dim = x.shape
    n_dim = w.shape[1]
    assert m_rows % tm == 0 and n_dim % tn == 0
    return pl.pallas_call(
        _dense_kernel,
        grid=(n_dim // tn, m_rows // tm),
        in_specs=[pl.BlockSpec((tm, k_dim), lambda j, i: (i, 0)),
                  pl.BlockSpec((k_dim, tn), lambda j, i: (0, j))],
        out_specs=pl.BlockSpec((tm, tn), lambda j, i: (i, j)),
        out_shape=jax.ShapeDtypeStruct((m_rows, n_dim), out_dtype),
        compiler_params=_cparams("parallel", "parallel"),
        name="dense",
    )(x, w)


def _slope(head):
    return 2.0 ** (-8.0 * (head + 1) / N_HEADS)


def _attn_prompt_kernel(slope_ref, q_ref, kp_ref, kc_ref, vp_ref, vc_ref, o_ref, lse_ref, *, dil, n_pairs):
    lb = pl.program_id(1)
    n = pl.program_id(2)
    nk = ATT_STEPS
    qi = lax.broadcasted_iota(jnp.int32, (nk, 2 * nk), 0)
    kj = lax.broadcasted_iota(jnp.int32, (nk, 2 * nk), 1)
    delta = qi + nk - kj
    valid = (delta >= 0) & (delta <= nk) & ((kj >= nk) | (n > 0))
    dist = (delta * dil).astype(F32)
    lane = lax.broadcasted_iota(jnp.int32, (1, LANES), 1)
    m_a = lane < HEAD

    def scores(item):
        rows, p = item
        ln = slice(p * LANES, (p + 1) * LANES)
        q = q_ref[rows, ln] * (HEAD ** -0.5)
        k = jnp.concatenate([kp_ref[rows, ln], kc_ref[rows, ln]], axis=0).astype(BF16)
        return [_mm_nt(jnp.where(m_a, q, 0.0), k), _mm_nt(jnp.where(m_a, 0.0, q), k)]

    def run(items):
        s_next = scores(items[0])
        for idx, (rows, p) in enumerate(items):
            s_cur = s_next
            if idx + 1 < len(items):
                s_next = scores(items[idx + 1])
            ln = slice(p * LANES, (p + 1) * LANES)
            v = jnp.concatenate([vp_ref[rows, ln], vc_ref[rows, ln]], axis=0).astype(BF16)
            outs, lses = [], []
            for hh in range(2):
                slope = slope_ref[(lb * n_pairs + p) * 2 + hh]
                s = jnp.where(valid, s_cur[hh] - slope * dist, NEG_BIG)
                m = jnp.max(s, axis=-1, keepdims=True)
                e = jnp.exp(s - m)
                l = jnp.sum(e, axis=-1, keepdims=True)
                outs.append(jnp.dot(e.astype(BF16), v, preferred_element_type=F32) / l)
                lses.append(m + jnp.log(l))
            o_ref[rows, ln] = jnp.where(m_a, outs[0], outs[1])
            lse_ref[rows, ln] = jnp.where(m_a, lses[0], lses[1])

    if dil == 1:
        run([(slice(None), p) for p in range(n_pairs)])
    else:
        group = min(dil, ATT_CLASS_UNROLL)

        def body(gi, carry):
            run([(pl.ds(gi * group + u, nk, stride=dil), 0) for u in range(group)])
            return carry

        lax.fori_loop(0, dil // group, body, 0)


ATT_CLASS_UNROLL = 4


def attn_prompt_group(q, kv, *, group, bsz, seq_len, dil):
    tile = ATT_STEPS * dil
    assert seq_len % tile == 0
    n_tiles = seq_len // tile
    n_pairs = PAIRS if dil == 1 else 1
    width = n_pairs * LANES
    n_lb = D_MODEL // width
    slopes = jnp.asarray([_slope(h) for h in range(N_HEADS)], F32)

    def spec(col0, back):
        return pl.BlockSpec((tile, width),
                            lambda b, lb, n, sl: (b * n_tiles + jnp.maximum(n - back, 0), col0 * n_lb + lb))

    out = pl.BlockSpec((tile, width), lambda b, lb, n, sl: (b * n_tiles + n, lb))
    grid_spec = pltpu.PrefetchScalarGridSpec(
        num_scalar_prefetch=1,
        grid=(bsz, n_lb, n_tiles),
        in_specs=[spec(group, 0), spec(0, 1), spec(0, 0), spec(1, 1), spec(1, 0)],
        out_specs=[out, out],
    )
    return pl.pallas_call(
        functools.partial(_attn_prompt_kernel, dil=dil, n_pairs=n_pairs),
        grid_spec=grid_spec,
        out_shape=[jax.ShapeDtypeStruct((bsz * seq_len, D_MODEL), F32)] * 2,
        compiler_params=_cparams("parallel", "parallel", "arbitrary"),
        name="attn_prompt",
    )(slopes, q, kv, kv, kv, kv)


def _attn_merge_kernel(o0, l0, o1, l1, o2, l2, out_ref):
    m = jnp.maximum(jnp.maximum(l0[...], l1[...]), l2[...])
    w0 = jnp.exp(l0[...] - m)
    w1 = jnp.exp(l1[...] - m)
    w2 = jnp.exp(l2[...] - m)
    out_ref[...] = (w0 * o0[...] + w1 * o1[...] + w2 * o2[...]) / (w0 + w1 + w2)


def attn_merge(parts, m_rows, *, tm):
    mp = parts[0].shape[0]
    tok = pl.BlockSpec((tm, D_MODEL), lambda i: (i, 0))
    return pl.pallas_call(
        _attn_merge_kernel,
        grid=(mp // tm,),
        in_specs=[tok] * 6,
        out_specs=tok,
        out_shape=jax.ShapeDtypeStruct((m_rows, D_MODEL), F32),
        compiler_params=_cparams("parallel"),
        name="attn_merge",
    )(*parts)


def _attn_sample_kernel(q0_ref, q1_ref, q2_ref, kc_ref, kn_ref, vc_ref, vn_ref, alias_ref, o_ref, *, t_len, kv_buf):
    del alias_ref
    p = pl.program_id(1)
    lane = lax.broadcasted_iota(jnp.int32, (1, LANES), 1)
    m_a = lane < HEAD
    lhs = []
    for q_ref in (q0_ref, q1_ref, q2_ref):
        q = q_ref[...].astype(F32) * (HEAD ** -0.5)
        lhs += [jnp.where(m_a, q, 0.0), jnp.where(m_a, 0.0, q)]
    lhs = jnp.concatenate(lhs, axis=0)
    n_rows = 6 * t_len
    pad = jnp.zeros((LANES - t_len, LANES), F32)
    kn = jnp.concatenate([kn_ref[...], pad], axis=0)
    vn = jnp.concatenate([vn_ref[...], pad], axis=0)
    s_c = _mm_nt(lhs, kc_ref[...])
    s_n = _mm_nt(lhs, kn)

    ri = lax.broadcasted_iota(jnp.int32, (n_rows, 1), 0)
    t = ri % t_len
    grp = ri // (2 * t_len)
    hh = (ri // t_len) % 2
    dmask = jnp.where(grp == 0, GROUPS[0][1] - 1, jnp.where(grp == 1, GROUPS[1][1] - 1, GROUPS[2][1] - 1))
    win = jnp.where(grp == 0, GROUPS[0][0], jnp.where(grp == 1, GROUPS[1][0], GROUPS[2][0]))
    head = (2 * p + hh).astype(F32)
    slope = jnp.exp2(-8.0 * (head + 1.0) / N_HEADS)

    def masked(s, dist):
        ok = (dist >= 0) & (dist <= win) & ((dist & dmask) == 0)
        return jnp.where(ok, s - slope * dist.astype(F32), NEG_BIG)

    jc = lax.broadcasted_iota(jnp.int32, (n_rows, kv_buf), 1)
    jn = lax.broadcasted_iota(jnp.int32, (n_rows, LANES), 1)
    s_c = masked(s_c, kv_buf + t - jc)
    s_n = masked(s_n, t - jn)
    m = jnp.maximum(jnp.max(s_c, axis=-1, keepdims=True), jnp.max(s_n, axis=-1, keepdims=True))
    e_c = jnp.exp(s_c - m)
    e_n = jnp.exp(s_n - m)
    l = jnp.sum(e_c, axis=-1, keepdims=True) + jnp.sum(e_n, axis=-1, keepdims=True)
    acc = _mm(e_c, vc_ref[...]) + _mm(e_n, vn)

    blk = 2 * t_len
    m_g = [m[g * blk:(g + 1) * blk] for g in range(3)]
    m_all = jnp.maximum(jnp.maximum(m_g[0], m_g[1]), m_g[2])
    num = 0.0
    den = 0.0
    for g in range(3):
        w = jnp.exp(m_g[g] - m_all)
        num = num + w * acc[g * blk:(g + 1) * blk]
        den = den + w * l[g * blk:(g + 1) * blk]
    res = num / den
    o_ref[...] = jnp.where(m_a, res[:t_len], res[t_len:])


def attn_sample(q, kv, cache_k, cache_v, out, *, row0, bsz, t_len):
    kv_buf = cache_k.shape[1]
    assert kv_buf >= GROUPS[-1][0] and kv_buf % LANES == 0 and row0 % t_len == 0 and t_len % 8 == 0
    blk0 = row0 // t_len
    qs = [pl.BlockSpec((t_len, LANES), lambda b, p, g=g: (blk0 + b, g * PAIRS + p)) for g in range(3)]
    cache = pl.BlockSpec((None, kv_buf, LANES), lambda b, p: (b, 0, p))
    k_new = pl.BlockSpec((t_len, LANES), lambda b, p: (blk0 + b, p))
    v_new = pl.BlockSpec((t_len, LANES), lambda b, p: (blk0 + b, PAIRS + p))
    return pl.pallas_call(
        functools.partial(_attn_sample_kernel, t_len=t_len, kv_buf=kv_buf),
        grid=(bsz, PAIRS),
        in_specs=qs + [cache, k_new, cache, v_new, pl.BlockSpec(memory_space=pl.ANY)],
        out_specs=pl.BlockSpec((t_len, LANES), lambda b, p: (blk0 + b, p)),
        out_shape=jax.ShapeDtypeStruct(out.shape, F32),
        input_output_aliases={7: 0},
        compiler_params=_cparams("parallel", "parallel"),
        name="attn_sample",
    )(q, q, q, cache_k, kv, cache_v, kv, out)


def _pad_cols(w):
    return jnp.pad(w, ((0, 0), (0, LORA_PAD - w.shape[1]))).astype(BF16)


def _pad_rows(w):
    return jnp.pad(w, ((0, LORA_PAD - w.shape[0]), (0, 0))).astype(BF16)


def kernel(x_prompt, x_sample, state_wkv, state_shift, cache_k, cache_v, ln_g, ln_b, rw_mu, rw_wr, rw_wk, rw_wv,
           rw_wo, rw_w0, rw_w1, rw_w2, rw_a0, rw_a1, rw_a2, rw_v0, rw_v1, rw_v2, rw_g1, rw_g2, rw_kk, rw_ka,
           rw_rk, rw_gn_g, rw_gn_b, kv_w, att_wq, att_wo, moe_wr, moe_br, moe_win, moe_bin, moe_wout, moe_bout):
    bp, seq_len, d = x_prompt.shape
    bs, dec_len, _ = x_sample.shape
    kv_buf = cache_k.shape[1]
    mp = bp * seq_len
    ms = bs * dec_len
    m_rows = mp + ms
    tm = TOKEN_TILE
    assert d == D_MODEL and m_rows % tm == 0 and mp % tm == 0
    t_block = min(SCAN_T_BLOCK, seq_len)

    x = jnp.concatenate([x_prompt.reshape(mp, d), x_sample.reshape(ms, d)], axis=0)
    wkv_p, wkv_s, shift_p, shift_s = [], [], [], []
    v_first = None
    kv = None

    def moe(layer, x1, logits):
        return moe_layer(x1, logits, moe_win, moe_bin[layer], moe_wout, moe_bout[layer],
                         jnp.stack([ln_g[layer, 1], ln_b[layer, 1]]), tm=tm, layer=layer)

    def post(layer, y, x_in, wo):
        wr = moe_wr[layer]
        wr_hi = wr.astype(BF16)
        wr_lo = (wr - wr_hi.astype(F32)).astype(BF16)
        padc = lambda w: jnp.pad(w, ((0, 0), (0, LANES - N_EXPERTS)))
        br = jnp.pad(moe_br[layer], (0, LANES - N_EXPERTS)).reshape(1, LANES)
        return post_mix(y, x_in, wo.astype(BF16), jnp.stack([ln_g[layer, 0], ln_b[layer, 0]]),
                        padc(wr_hi), padc(wr_lo), br, tm=tm)

    for layer in range(DEPTH):
        if layer < N_A_LAYERS:
            i = layer
            xp3 = x[:mp].reshape(bp, seq_len, d)
            xs3 = x[mp:].reshape(bs, dec_len, d)
            shift_p.append(xp3[:, -1])
            shift_s.append(xs3[:, -1])
            prev_p = jnp.concatenate([jnp.zeros((bp, 1, d), F32), xp3[:, :-1]], axis=1)
            prev_s = jnp.concatenate([state_shift[i][:, None, :], xs3[:, :-1]], axis=1)
            x_prev = jnp.concatenate([prev_p.reshape(mp, d), prev_s.reshape(ms, d)], axis=0)
            vec = jnp.stack([rw_w0[i], rw_a0[i], rw_v0[i - 1] if i > 0 else jnp.zeros((d,), F32)])
            mats = [rw_wr[i].astype(BF16), rw_wk[i].astype(BF16), rw_wv[i].astype(BF16),
                    _pad_cols(rw_w1[i]), _pad_rows(rw_w2[i]), _pad_cols(rw_a1[i]), _pad_rows(rw_a2[i]),
                    _pad_cols(rw_g1[i]), _pad_rows(rw_g2[i])]
            if i > 0:
                mats += [_pad_cols(rw_v1[i - 1]), _pad_rows(rw_v2[i - 1])]
            r, k, v, a, ld, g = a_proj(x, x_prev, v_first, rw_mu[i], vec, mats, tm=tm)
            if i == 0:
                v_first = v
            prm = jnp.stack([rw_kk[i], rw_ka[i], rw_rk[i].reshape(d), rw_gn_g[i], rw_gn_b[i]])
            seqs = (r, k, v, a, ld, g)
            y, sp = wkv_scan(seqs, prm, jnp.zeros((bp, PAIRS, LANES, LANES), F32), row0=0, t_len=seq_len,
                             chunk=SCAN_CHUNK, t_block=t_block)
            y, ss = wkv_scan(seqs, prm, pair_states(state_wkv[i]), row0=mp, t_len=dec_len,
                             chunk=dec_len, t_block=dec_len, out=y)
            wkv_p.append(unpair_states(sp))
            wkv_s.append(unpair_states(ss))
            x1, logits = post(layer, y, x, rw_wo[i])
        else:
            j = layer - N_A_LAYERS
            q = dense(x, att_wq[j].astype(BF16), tm=tm, tn=D_MODEL)
            parts = []
            for gi, (window, dil) in enumerate(GROUPS):
                assert window // dil == ATT_STEPS
                parts += list(attn_prompt_group(q, kv, group=gi, bsz=bp, seq_len=seq_len, dil=dil))
            y = attn_merge(parts, m_rows, tm=tm)
            y = attn_sample(q, kv, cache_k.reshape(bs, kv_buf, d), cache_v.reshape(bs, kv_buf, d), y,
                            row0=mp, bsz=bs, t_len=dec_len)
            x1, logits = post(layer, y, x, att_wo[j])
        x = moe(layer, x1, logits)
        if layer == N_A_LAYERS - 1:
            kv = dense(x, kv_w.astype(BF16), tm=tm, tn=D_MODEL)

    buf_p = min(GROUPS[-1][0], seq_len)
    heads = lambda t, n: t.reshape(t.shape[0], n, N_HEADS, HEAD)
    kv_p = kv[:mp].reshape(bp, seq_len, 2 * d)[:, -buf_p:]
    kv_s = kv[mp:].reshape(bs, dec_len, 2 * d)
    k_p_out = heads(kv_p[:, :, :d], buf_p)
    v_p_out = heads(kv_p[:, :, d:], buf_p)
    k_s_out = jnp.concatenate([cache_k, heads(kv_s[:, :, :d], dec_len)], axis=1)[:, -kv_buf:]
    v_s_out = jnp.concatenate([cache_v, heads(kv_s[:, :, d:], dec_len)], axis=1)[:, -kv_buf:]
    return (x[:mp].reshape(bp, seq_len, d), x[mp:].reshape(bs, dec_len, d),
            jnp.stack(wkv_p), jnp.stack(shift_p), k_p_out, v_p_out,
            jnp.stack(wkv_s), jnp.stack(shift_s), k_s_out, v_s_out)
```

```python
import functools

import jax
import jax.numpy as jnp
from jax import lax
from jax.experimental import pallas as pl
from jax.experimental.pallas import tpu as pltpu

F32 = jnp.float32
BF16 = jnp.bfloat16

D_MODEL = 1024
HEAD = 64
N_HEADS = D_MODEL // HEAD
LANES = 128
PAIRS = D_MODEL // LANES
DEPTH = 4
N_A_LAYERS = DEPTH // 2
LORA_PAD = 128
GN_EPS = 64e-5
LN_EPS = 1e-5
DN_ALPHA = (2 * DEPTH) ** 0.25
GROUPS = ((128, 1), (512, 4), (2048, 16))
ATT_STEPS = 128
N_EXPERTS = 32
TOP_K = 4
SWIGLU_LIMIT = 7.0
SWIGLU_ALPHA = 1.702
MOE_ROWS = 256
TOKEN_TILE = 256
SCAN_CHUNK = 64
SCAN_T_BLOCK = 256
NEG_BIG = -1e30
VMEM_LIMIT = 56 * 1024 * 1024


def _cparams(*sem):
    return pltpu.CompilerParams(dimension_semantics=sem, vmem_limit_bytes=VMEM_LIMIT)


def _mm(a, b):
    return jnp.dot(a.astype(BF16), b.astype(BF16), preferred_element_type=F32)


def _mm_nt(a, b):
    return lax.dot_general(a.astype(BF16), b.astype(BF16), (((1,), (1,)), ((), ())),
                           preferred_element_type=F32)


def _mm_tn(a, b):
    return lax.dot_general(a.astype(BF16), b.astype(BF16), (((0,), (0,)), ((), ())),
                           preferred_element_type=F32)


def _split(x, parts):
    out = []
    for _ in range(parts):
        h = x.astype(BF16)
        out.append(h)
        x = x - h.astype(F32)
    return out


def _mm_sel_r(x, sel, parts=2):
    acc = None
    for h in _split(x, parts):
        t = jnp.dot(h, sel, preferred_element_type=F32)
        acc = t if acc is None else acc + t
    return acc


def _mm_sel_l(sel, x, parts=3):
    acc = None
    for h in _split(x, parts):
        t = jnp.dot(sel, h, preferred_element_type=F32)
        acc = t if acc is None else acc + t
    return acc


def _sigmoid(x):
    return 1.0 / (1.0 + jnp.exp(-x))


def _layer_norm(x, g, b):
    mu = jnp.mean(x, axis=-1, keepdims=True)
    xc = x - mu
    var = jnp.mean(xc * xc, axis=-1, keepdims=True)
    return xc * lax.rsqrt(var + LN_EPS) * g + b


def _each(fn, *lists):
    return [fn(*xs) for xs in zip(*lists)]


def _unit_lower_inverse(a_side, cst, chunk):
    eye, row, scol, bd = cst
    blk = (row // 8) == (scol // 8)

    def mul(xs, ys):
        return _each(_mm, xs, [bd(y) for y in ys])

    x = [jnp.where(blk, -a, 0.0) for a in a_side]
    x2 = mul(x, x)
    x4 = mul(x2, x2)
    xx2 = mul(x, x2)
    y = _each(lambda x_, x2_, xx2_: eye + x_ + x2_ + xx2_, x, x2, xx2)
    t = _each(jnp.add, y, mul(y, x4))
    s = 8
    while s < chunk:
        rb = row // s
        off = (rb == (scol // s) + 1) & ((rb % 2) == 1)
        a_off = [jnp.where(off, a, 0.0) for a in a_side]
        t = _each(jnp.subtract, t, mul(t, mul(a_off, t)))
        s *= 2
    return t


def _wkv_chunk(s_mat, r, kr, v, a, ld, g, prm, cst, chunk):
    kk_p, ka_p, rk_p, gng, gnb = prm
    m_a, e_seg, e_seg2, tri, eye, row, scol, strict, incl, bd = cst
    c = chunk
    kkr = _each(jnp.multiply, kr, kk_p)
    ss = [_mm_sel_r(x * x, e_seg) for x in kkr]
    kk = _each(lambda x, s_: x / jnp.maximum(jnp.sqrt(s_), 1e-12), kkr, ss)
    k = _each(lambda kr_, a_, ka_: kr_ * (1.0 + (a_ - 1.0) * ka_), kr, a, ka_p)
    b = _each(jnp.multiply, kk, a)
    cl = [_mm_sel_l(tri, x) for x in ld]
    cl_end = [x[c - 1:c, :] for x in cl]

    def stack(x):
        return jnp.concatenate([jnp.where(m_a, x, 0.0), jnp.where(m_a, 0.0, x)], axis=0)

    kkg = _each(lambda kk_, cl_, ld_: kk_ * jnp.exp(cl_ - ld_), kk, cl, ld)
    rg = _each(lambda r_, cl_: r_ * jnp.exp(cl_), r, cl)
    g_inv = [jnp.exp(-x) for x in cl]
    bd_s = _each(lambda b_, gi: stack(b_ * gi), b, g_inv)
    kd_s = _each(lambda k_, gi: stack(k_ * gi), k, g_inv)
    g_end = _each(lambda ce, cl_: jnp.exp(ce - cl_), cl_end, cl)
    be_s = _each(lambda b_, ge: stack(b_ * ge), b, g_end)
    ke_s = _each(lambda k_, ge: stack(k_ * ge), k, g_end)
    v_s = [stack(x) for x in v]

    if (2 * c) % LANES == 0:
        bk_s = _each(lambda x_, y_: jnp.concatenate([x_, y_], axis=0), bd_s, kd_s)
        ab = _each(_mm_nt, kkg, bk_s)
        rbk = _each(_mm_nt, rg, bk_s)
        a_side, b_side = [x[:, :2 * c] for x in ab], [x[:, 2 * c:] for x in ab]
        rb_side, rk_side = [x[:, :2 * c] for x in rbk], [x[:, 2 * c:] for x in rbk]
    else:
        a_side, b_side = _each(_mm_nt, kkg, bd_s), _each(_mm_nt, kkg, kd_s)
        rb_side, rk_side = _each(_mm_nt, rg, bd_s), _each(_mm_nt, rg, kd_s)
    a_side = [jnp.where(strict, x, 0.0) for x in a_side]
    b_side = [jnp.where(strict, x, 0.0) for x in b_side]
    rb_side = [jnp.where(incl, x, 0.0) for x in rb_side]
    rk_side = [jnp.where(incl, x, 0.0) for x in rk_side]
    t_side = _unit_lower_inverse(a_side, (eye, row, scol, bd), c)

    bv = _each(_mm, b_side, v_s)
    gu = _each(lambda t_, kkg_, bv_: _mm(t_, jnp.concatenate([stack(kkg_), stack(bv_)], axis=1)), t_side, kkg, bv)
    g_s = [stack(x[:, :LANES]) for x in gu]
    u1_s = [stack(x[:, LANES:]) for x in gu]
    pq = _each(lambda rb_, gs_, u1_: _mm(rb_, jnp.concatenate([gs_, u1_], axis=1)), rb_side, g_s, u1_s)
    rkv = _each(_mm, rk_side, v_s)
    p = _each(lambda rg_, pq_: rg_ - pq_[:, :LANES], rg, pq)
    q = _each(lambda rkv_, pq_: rkv_ - pq_[:, LANES:], rkv, pq)
    o = _each(lambda p_, s_, q_: _mm_nt(p_, s_) + q_, p, s_mat, q)
    gb = _each(_mm_tn, g_s, be_s)
    nt = _each(lambda vs_, u1_, ke_, be_: _mm_tn(jnp.concatenate([vs_, -u1_], axis=0),
                                                   jnp.concatenate([ke_, be_], axis=0)), v_s, u1_s, ke_s, be_s)
    s_new = _each(lambda s_, ce, gb_, nt_: s_ * jnp.exp(ce) - _mm(s_, gb_) + nt_, s_mat, cl_end, gb, nt)

    mb = _each(lambda o_, r_, k_, rk_: _mm_sel_r(jnp.concatenate([o_, r_ * k_ * rk_], axis=1), e_seg2),
               o, r, k, rk_p)
    d = _each(lambda o_, mb_: o_ - mb_[:, :LANES] * (1.0 / HEAD), o, mb)
    var = [_mm_sel_r(x * x, e_seg) * (1.0 / HEAD) for x in d]
    out = _each(lambda d_, var_, gg, gb_, mb_, v_, g_:
                (d_ * lax.rsqrt(var_ + GN_EPS) * gg + gb_ + mb_[:, LANES:] * v_) * g_,
                d, var, gng, gnb, mb, v, g)
    return s_new, out


def _wkv_kernel(r_ref, k_ref, v_ref, a_ref, ld_ref, g_ref, prm_ref, s0_ref, *rest, chunk, n_chunks, n_pairs):
    o_ref, s_out_ref, s_scr = rest[-3:]
    tb = pl.program_id(2)

    @pl.when(tb == 0)
    def _():
        s_scr[...] = s0_ref[0]

    c2 = 2 * chunk
    lane = lax.broadcasted_iota(jnp.int32, (1, LANES), 1)
    m_a = lane < HEAD
    er = lax.broadcasted_iota(jnp.int32, (2 * LANES, 2 * LANES), 0)
    ec = lax.broadcasted_iota(jnp.int32, (2 * LANES, 2 * LANES), 1)
    e_seg2 = ((er // HEAD) == (ec // HEAD)).astype(BF16)
    e_seg = e_seg2[:LANES, :LANES]
    tr = lax.broadcasted_iota(jnp.int32, (chunk, chunk), 0)
    tc = lax.broadcasted_iota(jnp.int32, (chunk, chunk), 1)
    tri = (tr >= tc).astype(BF16)
    row = lax.broadcasted_iota(jnp.int32, (chunk, c2), 0)
    col = lax.broadcasted_iota(jnp.int32, (chunk, c2), 1)
    scol = col % chunk
    strict = scol < row
    incl = scol <= row
    eye = (row == scol).astype(F32)
    left = col < chunk

    def bd(x):
        return jnp.concatenate([jnp.where(left, x, 0.0), jnp.where(left, 0.0, x)], axis=0)

    cst = (m_a, e_seg, e_seg2, tri, eye, row, scol, strict, incl, bd)

    def body(ci, carry):
        sl = pl.ds(pl.multiple_of(ci * chunk, chunk), chunk)
        lanes = [slice(p * LANES, (p + 1) * LANES) for p in range(n_pairs)]
        prm = tuple([prm_ref[i:i + 1, ln] for ln in lanes] for i in range(5))
        seqs = [[ref[sl, ln] for ln in lanes] for ref in (r_ref, k_ref, v_ref, a_ref, ld_ref, g_ref)]
        s_new, out = _wkv_chunk([s_scr[p] for p in range(n_pairs)], *seqs, prm, cst, chunk)
        for p in range(n_pairs):
            s_scr[p] = s_new[p]
            o_ref[sl, lanes[p]] = out[p]
        return carry

    lax.fori_loop(0, n_chunks, body, 0)

    @pl.when(tb == pl.num_programs(2) - 1)
    def _():
        s_out_ref[0] = s_scr[...]


def wkv_scan(seqs, prm, s0, *, row0, t_len, chunk, t_block, n_pairs=PAIRS, out=None):
    m_rows, d = seqs[0].shape
    bsz = s0.shape[0]
    assert d == D_MODEL and t_len % t_block == 0 and t_block % chunk == 0 and row0 % t_block == 0
    assert PAIRS % n_pairs == 0 and row0 + bsz * t_len <= m_rows
    width = n_pairs * LANES
    nt = t_len // t_block
    blk0 = row0 // t_block
    seq = pl.BlockSpec((t_block, width), lambda b, p, t: (blk0 + b * nt + t, p))
    st = pl.BlockSpec((1, n_pairs, LANES, LANES), lambda b, p, t: (b, p, 0, 0))
    in_specs = [seq] * 6 + [pl.BlockSpec((5, width), lambda b, p, t: (0, p)), st]
    args = list(seqs) + [prm, s0]
    aliases = {}
    if out is not None:
        in_specs.append(pl.BlockSpec(memory_space=pl.ANY))
        args.append(out)
        aliases = {len(args) - 1: 0}
    return pl.pallas_call(
        functools.partial(_wkv_kernel, chunk=chunk, n_chunks=t_block // chunk, n_pairs=n_pairs),
        grid=(bsz, PAIRS // n_pairs, nt),
        in_specs=in_specs,
        out_specs=[seq, st],
        out_shape=[jax.ShapeDtypeStruct((m_rows, d), F32),
                   jax.ShapeDtypeStruct((bsz, PAIRS, LANES, LANES), F32)],
        scratch_shapes=[pltpu.VMEM((n_pairs, LANES, LANES), F32)],
        input_output_aliases=aliases,
        compiler_params=_cparams("parallel", "parallel", "arbitrary"),
        name="wkv_scan",
    )(*args)


def pair_states(s):
    bsz = s.shape[0]
    s = s.reshape(bsz, PAIRS, 2, HEAD, HEAD)
    z = jnp.zeros_like(s[:, :, 0])
    top = jnp.concatenate([s[:, :, 0], z], axis=-1)
    bot = jnp.concatenate([z, s[:, :, 1]], axis=-1)
    return jnp.concatenate([top, bot], axis=-2)


def unpair_states(sp):
    bsz = sp.shape[0]
    s = jnp.stack([sp[:, :, :HEAD, :HEAD], sp[:, :, HEAD:, HEAD:]], axis=2)
    return s.reshape(bsz, N_HEADS, HEAD, HEAD)


def _a_proj_kernel(*refs, has_vres):
    if has_vres:
        (x_ref, xp_ref, vf_ref, mu_ref, vec_ref, wr, wk, wv, w1, w2, a1, a2, g1, g2, v1, v2,
         r_o, k_o, v_o, a_o, ld_o, g_o) = refs
    else:
        (x_ref, xp_ref, mu_ref, vec_ref, wr, wk, wv, w1, w2, a1, a2, g1, g2,
         r_o, k_o, v_o, a_o, ld_o, g_o) = refs
    x = x_ref[...]
    xx = xp_ref[...] - x
    xr, xw, xk, xv, xa, xg = [(x + xx * mu_ref[i:i + 1, :]).astype(BF16) for i in range(6)]
    r_o[...] = _mm(xr, wr[...])
    k_o[...] = _mm(xk, wk[...])
    v = _mm(xv, wv[...])
    z = vec_ref[0:1, :] + _mm(jnp.tanh(_mm(xw, w1[...])), w2[...])
    softplus_neg = jnp.maximum(-z, 0.0) + jnp.log(1.0 + jnp.exp(-jnp.abs(z)))
    ld_o[...] = -jnp.exp(-softplus_neg - 0.5)
    if has_vres:
        mix = _sigmoid(vec_ref[2:3, :] + _mm(_mm(xv, v1[...]), v2[...]))
        v = v + (vf_ref[...] - v) * mix
    v_o[...] = v
    a_o[...] = _sigmoid(vec_ref[1:2, :] + _mm(_mm(xa, a1[...]), a2[...]))
    g_o[...] = _mm(_sigmoid(_mm(xg, g1[...])), g2[...])


def a_proj(x, x_prev, v_first, mu, vec, mats, *, tm):
    m_rows = x.shape[0]
    assert m_rows % tm == 0
    tok = pl.BlockSpec((tm, D_MODEL), lambda i: (i, 0))
    full = lambda a: pl.BlockSpec(a.shape, lambda i: (0, 0))
    has_vres = v_first is not None
    acts = [x, x_prev] + ([v_first] if has_vres else [])
    consts = [mu, vec] + list(mats)
    return pl.pallas_call(
        functools.partial(_a_proj_kernel, has_vres=has_vres),
        grid=(m_rows // tm,),
        in_specs=[tok] * len(acts) + [full(c) for c in consts],
        out_specs=[tok] * 6,
        out_shape=[jax.ShapeDtypeStruct((m_rows, D_MODEL), F32)] * 6,
        compiler_params=_cparams("parallel"),
        name="a_proj",
    )(*acts, *consts)


def _post_kernel(y_ref, x_ref, wo_ref, ln_ref, wrh_ref, wrl_ref, br_ref, x1_ref, gate_ref, idx_ref):
    y = _mm(y_ref[...], wo_ref[...])
    x1 = _layer_norm(DN_ALPHA * x_ref[...] + y, ln_ref[0:1, :], ln_ref[1:2, :])
    x1_ref[...] = x1
    parts = _split(x1, 3)
    acc = br_ref[...]
    for h in parts:
        acc = acc + jnp.dot(h, wrh_ref[...], preferred_element_type=F32)
    for h in parts[:2]:
        acc = acc + jnp.dot(h, wrl_ref[...], preferred_element_type=F32)
    lane = lax.broadcasted_iota(jnp.int32, acc.shape, 1)
    lane_f = lane.astype(F32)
    lg = jnp.where(lane < N_EXPERTS, acc, -jnp.inf)
    vals = []
    idx_out = jnp.zeros(acc.shape, F32)
    for k in range(TOP_K):
        v = jnp.max(lg, axis=-1, keepdims=True)
        idx = jnp.min(jnp.where(lg == v, lane_f, float(LANES)), axis=-1, keepdims=True)
        vals.append(v)
        idx_out = jnp.where(lane == k, idx, idx_out)
        lg = jnp.where(lane_f == idx, -jnp.inf, lg)
    es = [jnp.exp(v - vals[0]) for v in vals]
    den = es[0]
    for e in es[1:]:
        den = den + e
    gate = jnp.zeros(acc.shape, F32)
    for k in range(TOP_K):
        gate = jnp.where(lane == k, es[k] / den, gate)
    gate_ref[...] = gate
    idx_ref[...] = idx_out.astype(jnp.int32)


def post_mix(y, x, wo, ln, wr_hi, wr_lo, br, *, tm):
    m_rows = x.shape[0]
    tok = pl.BlockSpec((tm, D_MODEL), lambda i: (i, 0))
    full = lambda a: pl.BlockSpec(a.shape, lambda i: (0, 0))
    return pl.pallas_call(
        _post_kernel,
        grid=(m_rows // tm,),
        in_specs=[tok, tok] + [full(c) for c in (wo, ln, wr_hi, wr_lo, br)],
        out_specs=[tok, pl.BlockSpec((tm, LANES), lambda i: (i, 0)), pl.BlockSpec((tm, LANES), lambda i: (i, 0))],
        out_shape=[jax.ShapeDtypeStruct((m_rows, D_MODEL), F32),
                   jax.ShapeDtypeStruct((m_rows, LANES), F32),
                   jax.ShapeDtypeStruct((m_rows, LANES), jnp.int32)],
        compiler_params=_cparams("parallel"),
        name="post_mix",
    )(y, x, wo, ln, wr_hi, wr_lo, br)


def _moe_kernel(be_ref, first_ref, nact_ref, xb_ref, win_ref, bin_ref, wout_ref, bout_ref, y_ref, win_s, wout_s):
    i = pl.program_id(0)
    slab = 128

    @pl.when(first_ref[i] == 1)
    def _():
        for j in range(D_MODEL // slab):
            rows = slice(j * slab, (j + 1) * slab)
            win_s[rows, :] = win_ref[0, rows, :].astype(BF16)
            wout_s[rows, :] = wout_ref[0, rows, :].astype(BF16)

    @pl.when(i < nact_ref[0])
    def _():
        h = jnp.dot(xb_ref[...].astype(BF16), win_s[...], preferred_element_type=F32) + bin_ref[0]
        h_gate = jnp.minimum(h[:, :D_MODEL], SWIGLU_LIMIT)
        h_up = jnp.clip(h[:, D_MODEL:], -SWIGLU_LIMIT, SWIGLU_LIMIT)
        act = (h_up + 1.0) * h_gate * _sigmoid(SWIGLU_ALPHA * h_gate)
        y_ref[...] = jnp.dot(act.astype(BF16), wout_s[...], preferred_element_type=F32) + bout_ref[0]


def moe_experts(block_e, first, n_act, xb, w_in, b_in, w_out, b_out, *, layer):
    rows = xb.shape[0]
    n_blocks = rows // MOE_ROWS
    grid_spec = pltpu.PrefetchScalarGridSpec(
        num_scalar_prefetch=3,
        grid=(n_blocks,),
        in_specs=[
            pl.BlockSpec((MOE_ROWS, D_MODEL), lambda i, be, fi, na: (i, 0)),
            pl.BlockSpec((None, 1, D_MODEL, 2 * D_MODEL), lambda i, be, fi, na: (layer, be[i], 0, 0)),
            pl.BlockSpec((1, 1, 2 * D_MODEL), lambda i, be, fi, na: (be[i], 0, 0)),
            pl.BlockSpec((None, 1, D_MODEL, D_MODEL), lambda i, be, fi, na: (layer, be[i], 0, 0)),
            pl.BlockSpec((1, 1, D_MODEL), lambda i, be, fi, na: (be[i], 0, 0)),
        ],
        out_specs=pl.BlockSpec((MOE_ROWS, D_MODEL), lambda i, be, fi, na: (i, 0)),
        scratch_shapes=[pltpu.VMEM((D_MODEL, 2 * D_MODEL), BF16), pltpu.VMEM((D_MODEL, D_MODEL), BF16)],
    )
    return pl.pallas_call(
        _moe_kernel,
        grid_spec=grid_spec,
        out_shape=jax.ShapeDtypeStruct((rows, D_MODEL), F32),
        compiler_params=_cparams("arbitrary"),
        name="moe_experts",
    )(block_e, first, n_act, xb, w_in, b_in.reshape(N_EXPERTS, 1, -1), w_out, b_out.reshape(N_EXPERTS, 1, -1))


def _combine_kernel(yg_ref, gate_ref, x_ref, ln_ref, o_ref):
    gate = gate_ref[...]
    ffn = gate[:, 0:1] * yg_ref[0]
    for k in range(1, TOP_K):
        ffn = ffn + gate[:, k:k + 1] * yg_ref[k]
    o_ref[...] = _layer_norm(DN_ALPHA * x_ref[...] + ffn, ln_ref[0:1, :], ln_ref[1:2, :])


def moe_combine(yg, gate, x, ln, *, tm):
    m_rows = x.shape[0]
    tok = pl.BlockSpec((tm, D_MODEL), lambda i: (i, 0))
    return pl.pallas_call(
        _combine_kernel,
        grid=(m_rows // tm,),
        in_specs=[pl.BlockSpec((TOP_K, tm, D_MODEL), lambda i: (0, i, 0)),
                  pl.BlockSpec((tm, TOP_K), lambda i: (i, 0)), tok,
                  pl.BlockSpec(ln.shape, lambda i: (0, 0))],
        out_specs=tok,
        out_shape=jax.ShapeDtypeStruct((m_rows, D_MODEL), F32),
        compiler_params=_cparams("parallel"),
        name="moe_combine",
    )(yg, gate, x, ln)


def _route(top_e, m_rows):
    mk = m_rows * TOP_K
    flat_e = top_e.reshape(-1).astype(jnp.int32)
    onehot = (flat_e[:, None] == jnp.arange(N_EXPERTS, dtype=jnp.int32)[None, :]).astype(jnp.int32)
    csum = jnp.cumsum(onehot, axis=0)
    counts = csum[-1]
    rank = jnp.sum((csum - onehot) * onehot, axis=1)
    padded = (counts + MOE_ROWS - 1) // MOE_ROWS * MOE_ROWS
    pad_end = jnp.cumsum(padded)
    pad_start = pad_end - padded
    start = jnp.cumsum(counts) - counts
    pos = jnp.sum(onehot * pad_start[None, :], axis=1) + rank
    order = jnp.argsort(flat_e).astype(jnp.int32)
    n_blocks = -(-mk // MOE_ROWS) + N_EXPERTS
    blk_start = jnp.arange(n_blocks, dtype=jnp.int32) * MOE_ROWS
    block_e = jnp.sum((pad_end[None, :] <= blk_start[:, None]).astype(jnp.int32), axis=1)
    block_e = jnp.minimum(block_e, N_EXPERTS - 1)
    first = ((blk_start == pad_start[block_e]) & (blk_start < pad_end[-1])).astype(jnp.int32)
    n_act = (pad_end[-1:] // MOE_ROWS).astype(jnp.int32)
    e_row = jnp.repeat(block_e, MOE_ROWS)
    j_row = jnp.arange(n_blocks * MOE_ROWS, dtype=jnp.int32) - pad_start[e_row]
    compact = jnp.clip(start[e_row] + j_row, 0, mk - 1)
    row_tok = jnp.where(j_row < counts[e_row], order[compact] // TOP_K, 0)
    return row_tok, pos, block_e, first, n_act


def moe_layer(x1, gate_l, idx_l, w_in, b_in, w_out, b_out, ln, *, tm, layer):
    m_rows = x1.shape[0]
    gate = gate_l[:, :TOP_K]
    row_tok, pos, block_e, first, n_act = _route(idx_l[:, :TOP_K], m_rows)
    xb = x1[row_tok]
    yb = moe_experts(block_e, first, n_act, xb, w_in, b_in, w_out, b_out, layer=layer)
    yg = yb[pos.reshape(m_rows, TOP_K).T]
    return moe_combine(yg, gate, x1, ln, tm=tm)


def _dense_kernel(x_ref, w_ref, o_ref):
    o_ref[...] = _mm(x_ref[...], w_ref[...]).astype(o_ref.dtype)


def dense(x, w, *, tm, tn, out_dtype=F32):
    m_rows, k_dim = x.shape
    n_dim = w.shape[1]
    assert m_rows % tm == 0 and n_dim % tn == 0
    return pl.pallas_call(
        _dense_kernel,
        grid=(n_dim // tn, m_rows // tm),
        in_specs=[pl.BlockSpec((tm, k_dim), lambda j, i: (i, 0)),
                  pl.BlockSpec((k_dim, tn), lambda j, i: (0, j))],
        out_specs=pl.BlockSpec((tm, tn), lambda j, i: (i, j)),
        out_shape=jax.ShapeDtypeStruct((m_rows, n_dim), out_dtype),
        compiler_params=_cparams("parallel", "parallel"),
        name="dense",
    )(x, w)


def _slope(head):
    return 2.0 ** (-8.0 * (head + 1) / N_HEADS)


def _attn_prompt_kernel(slope_ref, q_ref, kp_ref, kc_ref, vp_ref, vc_ref, o_ref, lse_ref, bias_scr, *, dil, n_pairs):
    lb = pl.program_id(1)
    n = pl.program_id(2)
    nk = ATT_STEPS
    lane = lax.broadcasted_iota(jnp.int32, (1, LANES), 1)
    m_a = lane < HEAD

    @pl.when(n == 0)
    def _():
        qi = lax.broadcasted_iota(jnp.int32, (nk, 2 * nk), 0)
        kj = lax.broadcasted_iota(jnp.int32, (nk, 2 * nk), 1)
        delta = qi + nk - kj
        valid = (delta >= 0) & (delta <= nk)
        dist = (delta * dil).astype(F32)
        for h in range(2 * n_pairs):
            bias = jnp.where(valid, -slope_ref[lb * 2 * n_pairs + h] * dist, NEG_BIG)
            bias_scr[1, h] = bias
            bias_scr[0, h] = jnp.where(kj >= nk, bias, NEG_BIG)

    table = jnp.minimum(n, 1)

    def scores(item):
        rows, p = item
        ln = slice(p * LANES, (p + 1) * LANES)
        q = q_ref[rows, ln] * (HEAD ** -0.5)
        k = jnp.concatenate([kp_ref[rows, ln], kc_ref[rows, ln]], axis=0).astype(BF16)
        return [_mm_nt(jnp.where(m_a, q, 0.0), k), _mm_nt(jnp.where(m_a, 0.0, q), k)]

    def run(items):
        s_next = scores(items[0])
        for idx, (rows, p) in enumerate(items):
            s_cur = s_next
            if idx + 1 < len(items):
                s_next = scores(items[idx + 1])
            ln = slice(p * LANES, (p + 1) * LANES)
            v = jnp.concatenate([vp_ref[rows, ln], vc_ref[rows, ln]], axis=0).astype(BF16)
            outs, lses = [], []
            for hh in range(2):
                s = s_cur[hh] + bias_scr[table, 2 * p + hh]
                m = jnp.max(s, axis=-1, keepdims=True)
                e = jnp.exp(s - m)
                l = jnp.sum(e, axis=-1, keepdims=True)
                outs.append(jnp.dot(e.astype(BF16), v, preferred_element_type=F32) / l)
                lses.append(m + jnp.log(l))
            o_ref[rows, ln] = jnp.where(m_a, outs[0], outs[1])
            lse_ref[rows, ln] = jnp.where(m_a, lses[0], lses[1])

    if dil == 1:
        run([(slice(None), p) for p in range(n_pairs)])
    else:
        group = min(dil, ATT_CLASS_UNROLL)

        def body(gi, carry):
            run([(pl.ds(gi * group + u, nk, stride=dil), 0) for u in range(group)])
            return carry

        lax.fori_loop(0, dil // group, body, 0)


ATT_CLASS_UNROLL = 4


def attn_prompt_group(q, kv, *, group, bsz, seq_len, dil):
    tile = ATT_STEPS * dil
    assert seq_len % tile == 0
    n_tiles = seq_len // tile
    n_pairs = PAIRS if dil == 1 else 1
    width = n_pairs * LANES
    n_lb = D_MODEL // width
    slopes = jnp.asarray([_slope(h) for h in range(N_HEADS)], F32)

    def spec(col0, back):
        return pl.BlockSpec((tile, width),
                            lambda b, lb, n, sl: (b * n_tiles + jnp.maximum(n - back, 0), col0 * n_lb + lb))

    out = pl.BlockSpec((tile, width), lambda b, lb, n, sl: (b * n_tiles + n, lb))
    grid_spec = pltpu.PrefetchScalarGridSpec(
        num_scalar_prefetch=1,
        grid=(bsz, n_lb, n_tiles),
        in_specs=[spec(group, 0), spec(0, 1), spec(0, 0), spec(1, 1), spec(1, 0)],
        out_specs=[out, out],
        scratch_shapes=[pltpu.VMEM((2, 2 * n_pairs, ATT_STEPS, 2 * ATT_STEPS), F32)],
    )
    return pl.pallas_call(
        functools.partial(_attn_prompt_kernel, dil=dil, n_pairs=n_pairs),
        grid_spec=grid_spec,
        out_shape=[jax.ShapeDtypeStruct((bsz * seq_len, D_MODEL), F32)] * 2,
        compiler_params=_cparams("parallel", "parallel", "arbitrary"),
        name="attn_prompt",
    )(slopes, q, kv, kv, kv, kv)


def _attn_merge_kernel(o0, l0, o1, l1, o2, l2, out_ref):
    m = jnp.maximum(jnp.maximum(l0[...], l1[...]), l2[...])
    w0 = jnp.exp(l0[...] - m)
    w1 = jnp.exp(l1[...] - m)
    w2 = jnp.exp(l2[...] - m)
    out_ref[...] = (w0 * o0[...] + w1 * o1[...] + w2 * o2[...]) / (w0 + w1 + w2)


def attn_merge(parts, m_rows, *, tm):
    mp = parts[0].shape[0]
    tok = pl.BlockSpec((tm, D_MODEL), lambda i: (i, 0))
    return pl.pallas_call(
        _attn_merge_kernel,
        grid=(mp // tm,),
        in_specs=[tok] * 6,
        out_specs=tok,
        out_shape=jax.ShapeDtypeStruct((m_rows, D_MODEL), F32),
        compiler_params=_cparams("parallel"),
        name="attn_merge",
    )(*parts)


def _attn_sample_kernel(q0_ref, q1_ref, q2_ref, kc_ref, kn_ref, vc_ref, vn_ref, alias_ref, o_ref, *, t_len, kv_buf):
    del alias_ref
    p = pl.program_id(1)
    lane = lax.broadcasted_iota(jnp.int32, (1, LANES), 1)
    m_a = lane < HEAD
    lhs = []
    for q_ref in (q0_ref, q1_ref, q2_ref):
        q = q_ref[...].astype(F32) * (HEAD ** -0.5)
        lhs += [jnp.where(m_a, q, 0.0), jnp.where(m_a, 0.0, q)]
    lhs = jnp.concatenate(lhs, axis=0)
    n_rows = 6 * t_len
    pad = jnp.zeros((LANES - t_len, LANES), F32)
    kn = jnp.concatenate([kn_ref[...], pad], axis=0)
    vn = jnp.concatenate([vn_ref[...], pad], axis=0)
    s_c = _mm_nt(lhs, kc_ref[...])
    s_n = _mm_nt(lhs, kn)

    ri = lax.broadcasted_iota(jnp.int32, (n_rows, 1), 0)
    t = ri % t_len
    grp = ri // (2 * t_len)
    hh = (ri // t_len) % 2
    dmask = jnp.where(grp == 0, GROUPS[0][1] - 1, jnp.where(grp == 1, GROUPS[1][1] - 1, GROUPS[2][1] - 1))
    win = jnp.where(grp == 0, GROUPS[0][0], jnp.where(grp == 1, GROUPS[1][0], GROUPS[2][0]))
    head = (2 * p + hh).astype(F32)
    slope = jnp.exp2(-8.0 * (head + 1.0) / N_HEADS)

    def masked(s, dist):
        ok = (dist >= 0) & (dist <= win) & ((dist & dmask) == 0)
        return jnp.where(ok, s - slope * dist.astype(F32), NEG_BIG)

    jc = lax.broadcasted_iota(jnp.int32, (n_rows, kv_buf), 1)
    jn = lax.broadcasted_iota(jnp.int32, (n_rows, LANES), 1)
    s_c = masked(s_c, kv_buf + t - jc)
    s_n = masked(s_n, t - jn)
    m = jnp.maximum(jnp.max(s_c, axis=-1, keepdims=True), jnp.max(s_n, axis=-1, keepdims=True))
    e_c = jnp.exp(s_c - m)
    e_n = jnp.exp(s_n - m)
    l = jnp.sum(e_c, axis=-1, keepdims=True) + jnp.sum(e_n, axis=-1, keepdims=True)
    acc = _mm(e_c, vc_ref[...]) + _mm(e_n, vn)

    blk = 2 * t_len
    m_g = [m[g * blk:(g + 1) * blk] for g in range(3)]
    m_all = jnp.maximum(jnp.maximum(m_g[0], m_g[1]), m_g[2])
    num = 0.0
    den = 0.0
    for g in range(3):
        w = jnp.exp(m_g[g] - m_all)
        num = num + w * acc[g * blk:(g + 1) * blk]
        den = den + w * l[g * blk:(g + 1) * blk]
    res = num / den
    o_ref[...] = jnp.where(m_a, res[:t_len], res[t_len:])


def attn_sample(q, kv, cache_k, cache_v, out, *, row0, bsz, t_len):
    kv_buf = cache_k.shape[1]
    assert kv_buf >= GROUPS[-1][0] and kv_buf % LANES == 0 and row0 % t_len == 0 and t_len % 8 == 0
    blk0 = row0 // t_len
    qs = [pl.BlockSpec((t_len, LANES), lambda b, p, g=g: (blk0 + b, g * PAIRS + p)) for g in range(3)]
    cache = pl.BlockSpec((None, kv_buf, LANES), lambda b, p: (b, 0, p))
    k_new = pl.BlockSpec((t_len, LANES), lambda b, p: (blk0 + b, p))
    v_new = pl.BlockSpec((t_len, LANES), lambda b, p: (blk0 + b, PAIRS + p))
    return pl.pallas_call(
        functools.partial(_attn_sample_kernel, t_len=t_len, kv_buf=kv_buf),
        grid=(bsz, PAIRS),
        in_specs=qs + [cache, k_new, cache, v_new, pl.BlockSpec(memory_space=pl.ANY)],
        out_specs=pl.BlockSpec((t_len, LANES), lambda b, p: (blk0 + b, p)),
        out_shape=jax.ShapeDtypeStruct(out.shape, F32),
        input_output_aliases={7: 0},
        compiler_params=_cparams("parallel", "parallel"),
        name="attn_sample",
    )(q, q, q, cache_k, kv, cache_v, kv, out)


def _pad_cols(w):
    return jnp.pad(w, ((0, 0), (0, LORA_PAD - w.shape[1]))).astype(BF16)


def _pad_rows(w):
    return jnp.pad(w, ((0, LORA_PAD - w.shape[0]), (0, 0))).astype(BF16)


def kernel(x_prompt, x_sample, state_wkv, state_shift, cache_k, cache_v, ln_g, ln_b, rw_mu, rw_wr, rw_wk, rw_wv,
           rw_wo, rw_w0, rw_w1, rw_w2, rw_a0, rw_a1, rw_a2, rw_v0, rw_v1, rw_v2, rw_g1, rw_g2, rw_kk, rw_ka,
           rw_rk, rw_gn_g, rw_gn_b, kv_w, att_wq, att_wo, moe_wr, moe_br, moe_win, moe_bin, moe_wout, moe_bout):
    bp, seq_len, d = x_prompt.shape
    bs, dec_len, _ = x_sample.shape
    kv_buf = cache_k.shape[1]
    mp = bp * seq_len
    ms = bs * dec_len
    m_rows = mp + ms
    tm = TOKEN_TILE
    assert d == D_MODEL and m_rows % tm == 0 and mp % tm == 0
    t_block = min(SCAN_T_BLOCK, seq_len)

    x = jnp.concatenate([x_prompt.reshape(mp, d), x_sample.reshape(ms, d)], axis=0)
    wkv_p, wkv_s, shift_p, shift_s = [], [], [], []
    v_first = None
    kv = None

    def moe(layer, x1, gate_l, idx_l):
        return moe_layer(x1, gate_l, idx_l, moe_win, moe_bin[layer], moe_wout, moe_bout[layer],
                         jnp.stack([ln_g[layer, 1], ln_b[layer, 1]]), tm=tm, layer=layer)

    def post(layer, y, x_in, wo):
        wr = moe_wr[layer]
        wr_hi = wr.astype(BF16)
        wr_lo = (wr - wr_hi.astype(F32)).astype(BF16)
        padc = lambda w: jnp.pad(w, ((0, 0), (0, LANES - N_EXPERTS)))
        br = jnp.pad(moe_br[layer], (0, LANES - N_EXPERTS)).reshape(1, LANES)
        return post_mix(y, x_in, wo.astype(BF16), jnp.stack([ln_g[layer, 0], ln_b[layer, 0]]),
                        padc(wr_hi), padc(wr_lo), br, tm=tm)

    for layer in range(DEPTH):
        if layer < N_A_LAYERS:
            i = layer
            xp3 = x[:mp].reshape(bp, seq_len, d)
            xs3 = x[mp:].reshape(bs, dec_len, d)
            shift_p.append(xp3[:, -1])
            shift_s.append(xs3[:, -1])
            prev_p = jnp.concatenate([jnp.zeros((bp, 1, d), F32), xp3[:, :-1]], axis=1)
            prev_s = jnp.concatenate([state_shift[i][:, None, :], xs3[:, :-1]], axis=1)
            x_prev = jnp.concatenate([prev_p.reshape(mp, d), prev_s.reshape(ms, d)], axis=0)
            vec = jnp.stack([rw_w0[i], rw_a0[i], rw_v0[i - 1] if i > 0 else jnp.zeros((d,), F32)])
            mats = [rw_wr[i].astype(BF16), rw_wk[i].astype(BF16), rw_wv[i].astype(BF16),
                    _pad_cols(rw_w1[i]), _pad_rows(rw_w2[i]), _pad_cols(rw_a1[i]), _pad_rows(rw_a2[i]),
                    _pad_cols(rw_g1[i]), _pad_rows(rw_g2[i])]
            if i > 0:
                mats += [_pad_cols(rw_v1[i - 1]), _pad_rows(rw_v2[i - 1])]
            r, k, v, a, ld, g = a_proj(x, x_prev, v_first, rw_mu[i], vec, mats, tm=tm)
            if i == 0:
                v_first = v
            prm = jnp.stack([rw_kk[i], rw_ka[i], rw_rk[i].reshape(d), rw_gn_g[i], rw_gn_b[i]])
            seqs = (r, k, v, a, ld, g)
            y, sp = wkv_scan(seqs, prm, jnp.zeros((bp, PAIRS, LANES, LANES), F32), row0=0, t_len=seq_len,
                             chunk=SCAN_CHUNK, t_block=t_block)
            y, ss = wkv_scan(seqs, prm, pair_states(state_wkv[i]), row0=mp, t_len=dec_len,
                             chunk=dec_len, t_block=dec_len, out=y)
            wkv_p.append(unpair_states(sp))
            wkv_s.append(unpair_states(ss))
            x1, gate_l, idx_l = post(layer, y, x, rw_wo[i])
        else:
            j = layer - N_A_LAYERS
            q = dense(x, att_wq[j].astype(BF16), tm=tm, tn=D_MODEL)
            parts = []
            for gi, (window, dil) in enumerate(GROUPS):
                assert window // dil == ATT_STEPS
                parts += list(attn_prompt_group(q, kv, group=gi, bsz=bp, seq_len=seq_len, dil=dil))
            y = attn_merge(parts, m_rows, tm=tm)
            y = attn_sample(q, kv, cache_k.reshape(bs, kv_buf, d), cache_v.reshape(bs, kv_buf, d), y,
                            row0=mp, bsz=bs, t_len=dec_len)
            x1, gate_l, idx_l = post(layer, y, x, att_wo[j])
        x = moe(layer, x1, gate_l, idx_l)
        if layer == N_A_LAYERS - 1:
            kv = dense(x, kv_w.astype(BF16), tm=tm, tn=D_MODEL)

    buf_p = min(GROUPS[-1][0], seq_len)
    heads = lambda t, n: t.reshape(t.shape[0], n, N_HEADS, HEAD)
    kv_p = kv[:mp].reshape(bp, seq_len, 2 * d)[:, -buf_p:]
    kv_s = kv[mp:].reshape(bs, dec_len, 2 * d)
    k_p_out = heads(kv_p[:, :, :d], buf_p)
    v_p_out = heads(kv_p[:, :, d:], buf_p)
    k_s_out = jnp.concatenate([cache_k, heads(kv_s[:, :, :d], dec_len)], axis=1)[:, -kv_buf:]
    v_s_out = jnp.concatenate([cache_v, heads(kv_s[:, :, d:], dec_len)], axis=1)[:, -kv_buf:]
    return (x[:mp].reshape(bp, seq_len, d), x[mp:].reshape(bs, dec_len, d),
            jnp.stack(wkv_p), jnp.stack(shift_p), k_p_out, v_p_out,
            jnp.stack(wkv_s), jnp.stack(shift_s), k_s_out, v_s_out)
```

```python
import functools

import jax
import jax.numpy as jnp
from jax import lax
from jax.experimental import pallas as pl
from jax.experimental.pallas import tpu as pltpu

F32 = jnp.float32
BF16 = jnp.bfloat16

D_MODEL = 1024
HEAD = 64
N_HEADS = D_MODEL // HEAD
LANES = 128
PAIRS = D_MODEL // LANES
DEPTH = 4
N_A_LAYERS = DEPTH // 2
LORA_PAD = 128
GN_EPS = 64e-5
LN_EPS = 1e-5
DN_ALPHA = (2 * DEPTH) ** 0.25
GROUPS = ((128, 1), (512, 4), (2048, 16))
ATT_STEPS = 128
N_EXPERTS = 32
TOP_K = 4
SWIGLU_LIMIT = 7.0
SWIGLU_ALPHA = 1.702
MOE_ROWS = 256
TOKEN_TILE = 256
DENSE_TILE = 1280
SCAN_CHUNK = 64
SCAN_T_BLOCK = 256
NEG_BIG = -1e30
VMEM_LIMIT = 56 * 1024 * 1024


def _cparams(*sem):
    return pltpu.CompilerParams(dimension_semantics=sem, vmem_limit_bytes=VMEM_LIMIT)


def _mm(a, b):
    return jnp.dot(a.astype(BF16), b.astype(BF16), preferred_element_type=F32)


def _mm_nt(a, b):
    return lax.dot_general(a.astype(BF16), b.astype(BF16), (((1,), (1,)), ((), ())),
                           preferred_element_type=F32)


def _mm_tn(a, b):
    return lax.dot_general(a.astype(BF16), b.astype(BF16), (((0,), (0,)), ((), ())),
                           preferred_element_type=F32)


def _split(x, parts):
    out = []
    for _ in range(parts):
        h = x.astype(BF16)
        out.append(h)
        x = x - h.astype(F32)
    return out


def _mm_sel_r(x, sel, parts=2):
    acc = None
    for h in _split(x, parts):
        t = jnp.dot(h, sel, preferred_element_type=F32)
        acc = t if acc is None else acc + t
    return acc


def _mm_sel_l(sel, x, parts=3):
    acc = None
    for h in _split(x, parts):
        t = jnp.dot(sel, h, preferred_element_type=F32)
        acc = t if acc is None else acc + t
    return acc


def _sigmoid(x):
    return 1.0 / (1.0 + jnp.exp(-x))


def _layer_norm(x, g, b):
    mu = jnp.mean(x, axis=-1, keepdims=True)
    xc = x - mu
    var = jnp.mean(xc * xc, axis=-1, keepdims=True)
    return xc * lax.rsqrt(var + LN_EPS) * g + b


def _each(fn, *lists):
    return [fn(*xs) for xs in zip(*lists)]


def _unit_lower_inverse(a_side, cst, chunk):
    eye, row, scol, bd = cst
    blk = (row // 8) == (scol // 8)

    def mul(xs, ys):
        return _each(_mm, xs, [bd(y) for y in ys])

    x = [jnp.where(blk, -a, 0.0) for a in a_side]
    x2 = mul(x, x)
    x4 = mul(x2, x2)
    xx2 = mul(x, x2)
    y = _each(lambda x_, x2_, xx2_: eye + x_ + x2_ + xx2_, x, x2, xx2)
    t = _each(jnp.add, y, mul(y, x4))
    s = 8
    while s < chunk:
        rb = row // s
        off = (rb == (scol // s) + 1) & ((rb % 2) == 1)
        a_off = [jnp.where(off, a, 0.0) for a in a_side]
        t = _each(jnp.subtract, t, mul(t, mul(a_off, t)))
        s *= 2
    return t


def _wkv_chunk(s_mat, r, kr, v, a, ld, g, prm, cst, chunk):
    kk_p, ka_p, rk_p, gng, gnb = prm
    m_a, e_seg, e_seg2, tri, eye, row, scol, strict, incl, bd = cst
    c = chunk
    kkr = _each(jnp.multiply, kr, kk_p)
    ss = [_mm_sel_r(x * x, e_seg) for x in kkr]
    kk = _each(lambda x, s_: x / jnp.maximum(jnp.sqrt(s_), 1e-12), kkr, ss)
    k = _each(lambda kr_, a_, ka_: kr_ * (1.0 + (a_ - 1.0) * ka_), kr, a, ka_p)
    b = _each(jnp.multiply, kk, a)
    cl = [_mm_sel_l(tri, x) for x in ld]
    cl_end = [x[c - 1:c, :] for x in cl]

    def stack(x):
        return jnp.concatenate([jnp.where(m_a, x, 0.0), jnp.where(m_a, 0.0, x)], axis=0)

    kkg = _each(lambda kk_, cl_, ld_: kk_ * jnp.exp(cl_ - ld_), kk, cl, ld)
    rg = _each(lambda r_, cl_: r_ * jnp.exp(cl_), r, cl)
    g_inv = [jnp.exp(-x) for x in cl]
    bd_s = _each(lambda b_, gi: stack(b_ * gi), b, g_inv)
    kd_s = _each(lambda k_, gi: stack(k_ * gi), k, g_inv)
    g_end = _each(lambda ce, cl_: jnp.exp(ce - cl_), cl_end, cl)
    be_s = _each(lambda b_, ge: stack(b_ * ge), b, g_end)
    ke_s = _each(lambda k_, ge: stack(k_ * ge), k, g_end)
    v_s = [stack(x) for x in v]

    if (2 * c) % LANES == 0:
        bk_s = _each(lambda x_, y_: jnp.concatenate([x_, y_], axis=0), bd_s, kd_s)
        ab = _each(_mm_nt, kkg, bk_s)
        rbk = _each(_mm_nt, rg, bk_s)
        a_side, b_side = [x[:, :2 * c] for x in ab], [x[:, 2 * c:] for x in ab]
        rb_side, rk_side = [x[:, :2 * c] for x in rbk], [x[:, 2 * c:] for x in rbk]
    else:
        a_side, b_side = _each(_mm_nt, kkg, bd_s), _each(_mm_nt, kkg, kd_s)
        rb_side, rk_side = _each(_mm_nt, rg, bd_s), _each(_mm_nt, rg, kd_s)
    a_side = [jnp.where(strict, x, 0.0) for x in a_side]
    b_side = [jnp.where(strict, x, 0.0) for x in b_side]
    rb_side = [jnp.where(incl, x, 0.0) for x in rb_side]
    rk_side = [jnp.where(incl, x, 0.0) for x in rk_side]
    t_side = _unit_lower_inverse(a_side, (eye, row, scol, bd), c)

    bv = _each(_mm, b_side, v_s)
    gu = _each(lambda t_, kkg_, bv_: _mm(t_, jnp.concatenate([stack(kkg_), stack(bv_)], axis=1)), t_side, kkg, bv)
    g_s = [stack(x[:, :LANES]) for x in gu]
    u1_s = [stack(x[:, LANES:]) for x in gu]
    pq = _each(lambda rb_, gs_, u1_: _mm(rb_, jnp.concatenate([gs_, u1_], axis=1)), rb_side, g_s, u1_s)
    rkv = _each(_mm, rk_side, v_s)
    p = _each(lambda rg_, pq_: rg_ - pq_[:, :LANES], rg, pq)
    q = _each(lambda rkv_, pq_: rkv_ - pq_[:, LANES:], rkv, pq)
    o = _each(lambda p_, s_, q_: _mm_nt(p_, s_) + q_, p, s_mat, q)
    gb = _each(_mm_tn, g_s, be_s)
    nt = _each(lambda vs_, u1_, ke_, be_: _mm_tn(jnp.concatenate([vs_, -u1_], axis=0),
                                                   jnp.concatenate([ke_, be_], axis=0)), v_s, u1_s, ke_s, be_s)
    s_new = _each(lambda s_, ce, gb_, nt_: s_ * jnp.exp(ce) - _mm(s_, gb_) + nt_, s_mat, cl_end, gb, nt)

    mb = _each(lambda o_, r_, k_, rk_: _mm_sel_r(jnp.concatenate([o_, r_ * k_ * rk_], axis=1), e_seg2),
               o, r, k, rk_p)
    d = _each(lambda o_, mb_: o_ - mb_[:, :LANES] * (1.0 / HEAD), o, mb)
    var = [_mm_sel_r(x * x, e_seg) * (1.0 / HEAD) for x in d]
    out = _each(lambda d_, var_, gg, gb_, mb_, v_, g_:
                (d_ * lax.rsqrt(var_ + GN_EPS) * gg + gb_ + mb_[:, LANES:] * v_) * g_,
                d, var, gng, gnb, mb, v, g)
    return s_new, out


def _wkv_kernel(r_ref, k_ref, v_ref, a_ref, ld_ref, g_ref, prm_ref, s0_ref, *rest, chunk, n_chunks, n_pairs):
    o_ref, s_out_ref, s_scr = rest[-3:]
    tb = pl.program_id(2)

    @pl.when(tb == 0)
    def _():
        s_scr[...] = s0_ref[0]

    c2 = 2 * chunk
    lane = lax.broadcasted_iota(jnp.int32, (1, LANES), 1)
    m_a = lane < HEAD
    er = lax.broadcasted_iota(jnp.int32, (2 * LANES, 2 * LANES), 0)
    ec = lax.broadcasted_iota(jnp.int32, (2 * LANES, 2 * LANES), 1)
    e_seg2 = ((er // HEAD) == (ec // HEAD)).astype(BF16)
    e_seg = e_seg2[:LANES, :LANES]
    tr = lax.broadcasted_iota(jnp.int32, (chunk, chunk), 0)
    tc = lax.broadcasted_iota(jnp.int32, (chunk, chunk), 1)
    tri = (tr >= tc).astype(BF16)
    row = lax.broadcasted_iota(jnp.int32, (chunk, c2), 0)
    col = lax.broadcasted_iota(jnp.int32, (chunk, c2), 1)
    scol = col % chunk
    strict = scol < row
    incl = scol <= row
    eye = (row == scol).astype(F32)
    left = col < chunk

    def bd(x):
        return jnp.concatenate([jnp.where(left, x, 0.0), jnp.where(left, 0.0, x)], axis=0)

    cst = (m_a, e_seg, e_seg2, tri, eye, row, scol, strict, incl, bd)

    def body(ci, carry):
        sl = pl.ds(pl.multiple_of(ci * chunk, chunk), chunk)
        lanes = [slice(p * LANES, (p + 1) * LANES) for p in range(n_pairs)]
        prm = tuple([prm_ref[i:i + 1, ln] for ln in lanes] for i in range(5))
        seqs = [[ref[sl, ln] for ln in lanes] for ref in (r_ref, k_ref, v_ref, a_ref, ld_ref, g_ref)]
        s_new, out = _wkv_chunk([s_scr[p] for p in range(n_pairs)], *seqs, prm, cst, chunk)
        for p in range(n_pairs):
            s_scr[p] = s_new[p]
            o_ref[sl, lanes[p]] = out[p]
        return carry

    lax.fori_loop(0, n_chunks, body, 0)

    @pl.when(tb == pl.num_programs(2) - 1)
    def _():
        s_out_ref[0] = s_scr[...]


def wkv_scan(seqs, prm, s0, *, row0, t_len, chunk, t_block, n_pairs=PAIRS, out=None):
    m_rows, d = seqs[0].shape
    bsz = s0.shape[0]
    assert d == D_MODEL and t_len % t_block == 0 and t_block % chunk == 0 and row0 % t_block == 0
    assert PAIRS % n_pairs == 0 and row0 + bsz * t_len <= m_rows
    width = n_pairs * LANES
    nt = t_len // t_block
    blk0 = row0 // t_block
    seq = pl.BlockSpec((t_block, width), lambda b, p, t: (blk0 + b * nt + t, p))
    st = pl.BlockSpec((1, n_pairs, LANES, LANES), lambda b, p, t: (b, p, 0, 0))
    in_specs = [seq] * 6 + [pl.BlockSpec((5, width), lambda b, p, t: (0, p)), st]
    args = list(seqs) + [prm, s0]
    aliases = {}
    if out is not None:
        in_specs.append(pl.BlockSpec(memory_space=pl.ANY))
        args.append(out)
        aliases = {len(args) - 1: 0}
    return pl.pallas_call(
        functools.partial(_wkv_kernel, chunk=chunk, n_chunks=t_block // chunk, n_pairs=n_pairs),
        grid=(bsz, PAIRS // n_pairs, nt),
        in_specs=in_specs,
        out_specs=[seq, st],
        out_shape=[jax.ShapeDtypeStruct((m_rows, d), F32),
                   jax.ShapeDtypeStruct((bsz, PAIRS, LANES, LANES), F32)],
        scratch_shapes=[pltpu.VMEM((n_pairs, LANES, LANES), F32)],
        input_output_aliases=aliases,
        compiler_params=_cparams("parallel", "parallel", "arbitrary"),
        name="wkv_scan",
    )(*args)


def pair_states(s):
    bsz = s.shape[0]
    s = s.reshape(bsz, PAIRS, 2, HEAD, HEAD)
    z = jnp.zeros_like(s[:, :, 0])
    top = jnp.concatenate([s[:, :, 0], z], axis=-1)
    bot = jnp.concatenate([z, s[:, :, 1]], axis=-1)
    return jnp.concatenate([top, bot], axis=-2)


def unpair_states(sp):
    bsz = sp.shape[0]
    s = jnp.stack([sp[:, :, :HEAD, :HEAD], sp[:, :, HEAD:, HEAD:]], axis=2)
    return s.reshape(bsz, N_HEADS, HEAD, HEAD)


def _a_proj_kernel(*refs, has_vres):
    if has_vres:
        (x_ref, xp_ref, vf_ref, mu_ref, vec_ref, wr, wk, wv, w1, w2, a1, a2, g1, g2, v1, v2,
         r_o, k_o, v_o, a_o, ld_o, g_o) = refs
    else:
        (x_ref, xp_ref, mu_ref, vec_ref, wr, wk, wv, w1, w2, a1, a2, g1, g2,
         r_o, k_o, v_o, a_o, ld_o, g_o) = refs
    x = x_ref[...]
    xx = xp_ref[...] - x
    xr, xw, xk, xv, xa, xg = [(x + xx * mu_ref[i:i + 1, :]).astype(BF16) for i in range(6)]
    r_o[...] = _mm(xr, wr[...])
    k_o[...] = _mm(xk, wk[...])
    v = _mm(xv, wv[...])
    z = vec_ref[0:1, :] + _mm(jnp.tanh(_mm(xw, w1[...])), w2[...])
    softplus_neg = jnp.maximum(-z, 0.0) + jnp.log(1.0 + jnp.exp(-jnp.abs(z)))
    ld_o[...] = -jnp.exp(-softplus_neg - 0.5)
    if has_vres:
        mix = _sigmoid(vec_ref[2:3, :] + _mm(_mm(xv, v1[...]), v2[...]))
        v = v + (vf_ref[...] - v) * mix
    v_o[...] = v
    a_o[...] = _sigmoid(vec_ref[1:2, :] + _mm(_mm(xa, a1[...]), a2[...]))
    g_o[...] = _mm(_sigmoid(_mm(xg, g1[...])), g2[...])


def a_proj(x, x_prev, v_first, mu, vec, mats, *, tm):
    m_rows = x.shape[0]
    assert m_rows % tm == 0
    tok = pl.BlockSpec((tm, D_MODEL), lambda i: (i, 0))
    full = lambda a: pl.BlockSpec(a.shape, lambda i: (0, 0))
    has_vres = v_first is not None
    acts = [x, x_prev] + ([v_first] if has_vres else [])
    consts = [mu, vec] + list(mats)
    return pl.pallas_call(
        functools.partial(_a_proj_kernel, has_vres=has_vres),
        grid=(m_rows // tm,),
        in_specs=[tok] * len(acts) + [full(c) for c in consts],
        out_specs=[tok] * 6,
        out_shape=[jax.ShapeDtypeStruct((m_rows, D_MODEL), F32)] * 6,
        compiler_params=_cparams("parallel"),
        name="a_proj",
    )(*acts, *consts)


def _post_kernel(y_ref, x_ref, wo_ref, ln_ref, wrh_ref, wrl_ref, br_ref, x1_ref, gate_ref, idx_ref):
    y = _mm(y_ref[...], wo_ref[...])
    x1 = _layer_norm(DN_ALPHA * x_ref[...] + y, ln_ref[0:1, :], ln_ref[1:2, :])
    x1_ref[...] = x1
    parts = _split(x1, 3)
    acc = br_ref[...]
    for h in parts:
        acc = acc + jnp.dot(h, wrh_ref[...], preferred_element_type=F32)
    for h in parts[:2]:
        acc = acc + jnp.dot(h, wrl_ref[...], preferred_element_type=F32)
    lane = lax.broadcasted_iota(jnp.int32, acc.shape, 1)
    lane_f = lane.astype(F32)
    lg = jnp.where(lane < N_EXPERTS, acc, -jnp.inf)
    vals = []
    idx_out = jnp.zeros(acc.shape, F32)
    for k in range(TOP_K):
        v = jnp.max(lg, axis=-1, keepdims=True)
        idx = jnp.min(jnp.where(lg == v, lane_f, float(LANES)), axis=-1, keepdims=True)
        vals.append(v)
        idx_out = jnp.where(lane == k, idx, idx_out)
        lg = jnp.where(lane_f == idx, -jnp.inf, lg)
    es = [jnp.exp(v - vals[0]) for v in vals]
    den = es[0]
    for e in es[1:]:
        den = den + e
    gate = jnp.zeros(acc.shape, F32)
    for k in range(TOP_K):
        gate = jnp.where(lane == k, es[k] / den, gate)
    gate_ref[...] = gate
    idx_ref[...] = idx_out.astype(jnp.int32)


def post_mix(y, x, wo, ln, wr_hi, wr_lo, br, *, tm):
    m_rows = x.shape[0]
    tok = pl.BlockSpec((tm, D_MODEL), lambda i: (i, 0))
    full = lambda a: pl.BlockSpec(a.shape, lambda i: (0, 0))
    return pl.pallas_call(
        _post_kernel,
        grid=(m_rows // tm,),
        in_specs=[tok, tok] + [full(c) for c in (wo, ln, wr_hi, wr_lo, br)],
        out_specs=[tok, pl.BlockSpec((tm, LANES), lambda i: (i, 0)), pl.BlockSpec((tm, LANES), lambda i: (i, 0))],
        out_shape=[jax.ShapeDtypeStruct((m_rows, D_MODEL), F32),
                   jax.ShapeDtypeStruct((m_rows, LANES), F32),
                   jax.ShapeDtypeStruct((m_rows, LANES), jnp.int32)],
        compiler_params=_cparams("parallel"),
        name="post_mix",
    )(y, x, wo, ln, wr_hi, wr_lo, br)


def _moe_kernel(be_ref, first_ref, nact_ref, xb_ref, win_ref, bin_ref, wout_ref, bout_ref, y_ref, win_s, wout_s):
    i = pl.program_id(0)
    slab = 128

    @pl.when(first_ref[i] == 1)
    def _():
        for j in range(D_MODEL // slab):
            rows = slice(j * slab, (j + 1) * slab)
            win_s[rows, :] = win_ref[0, rows, :].astype(BF16)
            wout_s[rows, :] = wout_ref[0, rows, :].astype(BF16)

    @pl.when(i < nact_ref[0])
    def _():
        h = jnp.dot(xb_ref[...].astype(BF16), win_s[...], preferred_element_type=F32) + bin_ref[0]
        h_gate = jnp.minimum(h[:, :D_MODEL], SWIGLU_LIMIT)
        h_up = jnp.clip(h[:, D_MODEL:], -SWIGLU_LIMIT, SWIGLU_LIMIT)
        act = (h_up + 1.0) * h_gate * _sigmoid(SWIGLU_ALPHA * h_gate)
        y_ref[...] = jnp.dot(act.astype(BF16), wout_s[...], preferred_element_type=F32) + bout_ref[0]


def moe_experts(block_e, first, n_act, xb, w_in, b_in, w_out, b_out, *, layer):
    rows = xb.shape[0]
    n_blocks = rows // MOE_ROWS
    grid_spec = pltpu.PrefetchScalarGridSpec(
        num_scalar_prefetch=3,
        grid=(n_blocks,),
        in_specs=[
            pl.BlockSpec((MOE_ROWS, D_MODEL), lambda i, be, fi, na: (i, 0)),
            pl.BlockSpec((None, 1, D_MODEL, 2 * D_MODEL), lambda i, be, fi, na: (layer, be[i], 0, 0)),
            pl.BlockSpec((1, 1, 2 * D_MODEL), lambda i, be, fi, na: (be[i], 0, 0)),
            pl.BlockSpec((None, 1, D_MODEL, D_MODEL), lambda i, be, fi, na: (layer, be[i], 0, 0)),
            pl.BlockSpec((1, 1, D_MODEL), lambda i, be, fi, na: (be[i], 0, 0)),
        ],
        out_specs=pl.BlockSpec((MOE_ROWS, D_MODEL), lambda i, be, fi, na: (i, 0)),
        scratch_shapes=[pltpu.VMEM((D_MODEL, 2 * D_MODEL), BF16), pltpu.VMEM((D_MODEL, D_MODEL), BF16)],
    )
    return pl.pallas_call(
        _moe_kernel,
        grid_spec=grid_spec,
        out_shape=jax.ShapeDtypeStruct((rows, D_MODEL), F32),
        compiler_params=_cparams("arbitrary"),
        name="moe_experts",
    )(block_e, first, n_act, xb, w_in, b_in.reshape(N_EXPERTS, 1, -1), w_out, b_out.reshape(N_EXPERTS, 1, -1))


def _combine_kernel(yg_ref, gate_ref, x_ref, ln_ref, o_ref):
    gate = gate_ref[...]
    ffn = gate[:, 0:1] * yg_ref[0]
    for k in range(1, TOP_K):
        ffn = ffn + gate[:, k:k + 1] * yg_ref[k]
    o_ref[...] = _layer_norm(DN_ALPHA * x_ref[...] + ffn, ln_ref[0:1, :], ln_ref[1:2, :])


def moe_combine(yg, gate, x, ln, *, tm):
    m_rows = x.shape[0]
    tok = pl.BlockSpec((tm, D_MODEL), lambda i: (i, 0))
    return pl.pallas_call(
        _combine_kernel,
        grid=(m_rows // tm,),
        in_specs=[pl.BlockSpec((TOP_K, tm, D_MODEL), lambda i: (0, i, 0)),
                  pl.BlockSpec((tm, TOP_K), lambda i: (i, 0)), tok,
                  pl.BlockSpec(ln.shape, lambda i: (0, 0))],
        out_specs=tok,
        out_shape=jax.ShapeDtypeStruct((m_rows, D_MODEL), F32),
        compiler_params=_cparams("parallel"),
        name="moe_combine",
    )(yg, gate, x, ln)


def _route(top_e, m_rows):
    mk = m_rows * TOP_K
    flat_e = top_e.reshape(-1).astype(jnp.int32)
    onehot = (flat_e[:, None] == jnp.arange(N_EXPERTS, dtype=jnp.int32)[None, :]).astype(jnp.int32)
    csum = jnp.cumsum(onehot, axis=0)
    counts = csum[-1]
    rank = jnp.sum((csum - onehot) * onehot, axis=1)
    padded = (counts + MOE_ROWS - 1) // MOE_ROWS * MOE_ROWS
    pad_end = jnp.cumsum(padded)
    pad_start = pad_end - padded
    start = jnp.cumsum(counts) - counts
    pos = jnp.sum(onehot * pad_start[None, :], axis=1) + rank
    order = jnp.argsort(flat_e).astype(jnp.int32)
    n_blocks = -(-mk // MOE_ROWS) + N_EXPERTS
    blk_start = jnp.arange(n_blocks, dtype=jnp.int32) * MOE_ROWS
    block_e = jnp.sum((pad_end[None, :] <= blk_start[:, None]).astype(jnp.int32), axis=1)
    block_e = jnp.minimum(block_e, N_EXPERTS - 1)
    first = ((blk_start == pad_start[block_e]) & (blk_start < pad_end[-1])).astype(jnp.int32)
    n_act = (pad_end[-1:] // MOE_ROWS).astype(jnp.int32)
    e_row = jnp.repeat(block_e, MOE_ROWS)
    j_row = jnp.arange(n_blocks * MOE_ROWS, dtype=jnp.int32) - pad_start[e_row]
    compact = jnp.clip(start[e_row] + j_row, 0, mk - 1)
    row_tok = jnp.where(j_row < counts[e_row], order[compact] // TOP_K, 0)
    return row_tok, pos, block_e, first, n_act


def moe_layer(x1, gate_l, idx_l, w_in, b_in, w_out, b_out, ln, *, tm, layer):
    m_rows = x1.shape[0]
    gate = gate_l[:, :TOP_K]
    row_tok, pos, block_e, first, n_act = _route(idx_l[:, :TOP_K], m_rows)
    xb = x1[row_tok]
    yb = moe_experts(block_e, first, n_act, xb, w_in, b_in, w_out, b_out, layer=layer)
    yg = yb[pos.reshape(m_rows, TOP_K).T]
    return moe_combine(yg, gate, x1, ln, tm=tm)


def _dense_kernel(x_ref, w_ref, o_ref):
    o_ref[...] = _mm(x_ref[...], w_ref[...]).astype(o_ref.dtype)


def dense(x, w, *, tm, tn, out_dtype=F32):
    m_rows, k_dim = x.shape
    n_dim = w.shape[1]
    assert m_rows % tm == 0 and n_dim % tn == 0
    return pl.pallas_call(
        _dense_kernel,
        grid=(n_dim // tn, m_rows // tm),
        in_specs=[pl.BlockSpec((tm, k_dim), lambda j, i: (i, 0)),
                  pl.BlockSpec((k_dim, tn), lambda j, i: (0, j))],
        out_specs=pl.BlockSpec((tm, tn), lambda j, i: (i, j)),
        out_shape=jax.ShapeDtypeStruct((m_rows, n_dim), out_dtype),
        compiler_params=_cparams("parallel", "parallel"),
        name="dense",
    )(x, w)


def _slope(head):
    return 2.0 ** (-8.0 * (head + 1) / N_HEADS)


def _attn_prompt_kernel(slope_ref, q_ref, kp_ref, kc_ref, vp_ref, vc_ref, o_ref, lse_ref, bias_scr, *, dil, n_pairs):
    lb = pl.program_id(1)
    n = pl.program_id(2)
    nk = ATT_STEPS
    lane = lax.broadcasted_iota(jnp.int32, (1, LANES), 1)
    m_a = lane < HEAD

    @pl.when(n == 0)
    def _():
        qi = lax.broadcasted_iota(jnp.int32, (nk, 2 * nk), 0)
        kj = lax.broadcasted_iota(jnp.int32, (nk, 2 * nk), 1)
        delta = qi + nk - kj
        valid = (delta >= 0) & (delta <= nk)
        dist = (delta * dil).astype(F32)
        for h in range(2 * n_pairs):
            bias = jnp.where(valid, -slope_ref[lb * 2 * n_pairs + h] * dist, NEG_BIG)
            bias_scr[1, h] = bias
            bias_scr[0, h] = jnp.where(kj >= nk, bias, NEG_BIG)

    table = jnp.minimum(n, 1)

    def scores(item):
        rows, p = item
        ln = slice(p * LANES, (p + 1) * LANES)
        q = q_ref[rows, ln] * (HEAD ** -0.5)
        k = jnp.concatenate([kp_ref[rows, ln], kc_ref[rows, ln]], axis=0).astype(BF16)
        return [_mm_nt(jnp.where(m_a, q, 0.0), k), _mm_nt(jnp.where(m_a, 0.0, q), k)]

    def run(items):
        groups = [items[i:i + 2] for i in range(0, len(items), 2)]
        s_next = [scores(it) for it in groups[0]]
        for gi, grp in enumerate(groups):
            s_cur = s_next
            if gi + 1 < len(groups):
                s_next = [scores(it) for it in groups[gi + 1]]
            lns = [slice(p * LANES, (p + 1) * LANES) for _, p in grp]
            vs = [jnp.concatenate([vp_ref[rows, ln], vc_ref[rows, ln]], axis=0).astype(BF16)
                  for (rows, _), ln in zip(grp, lns)]
            s = [s_cur[i][hh] + bias_scr[table, 2 * p + hh] for i, (_, p) in enumerate(grp) for hh in range(2)]
            m = [jnp.max(x, axis=-1, keepdims=True) for x in s]
            e = _each(lambda x, m_: jnp.exp(x - m_), s, m)
            l = [jnp.sum(x, axis=-1, keepdims=True) for x in e]
            pv = [jnp.dot(x.astype(BF16), vs[j // 2], preferred_element_type=F32) for j, x in enumerate(e)]
            outs = _each(jnp.divide, pv, l)
            lses = _each(lambda m_, l_: m_ + jnp.log(l_), m, l)
            for i, ((rows, _), ln) in enumerate(zip(grp, lns)):
                o_ref[rows, ln] = jnp.where(m_a, outs[2 * i], outs[2 * i + 1])
                lse_ref[rows, ln] = jnp.where(m_a, lses[2 * i], lses[2 * i + 1])

    if dil == 1:
        run([(slice(None), p) for p in range(n_pairs)])
    else:
        group = min(dil, ATT_CLASS_UNROLL)

        def body(gi, carry):
            run([(pl.ds(gi * group + u, nk, stride=dil), 0) for u in range(group)])
            return carry

        lax.fori_loop(0, dil // group, body, 0)


ATT_CLASS_UNROLL = 4


def attn_prompt_group(q, kv, *, group, bsz, seq_len, dil):
    tile = ATT_STEPS * dil
    assert seq_len % tile == 0
    n_tiles = seq_len // tile
    n_pairs = PAIRS if dil == 1 else 1
    width = n_pairs * LANES
    n_lb = D_MODEL // width
    slopes = jnp.asarray([_slope(h) for h in range(N_HEADS)], F32)

    def spec(col0, back):
        return pl.BlockSpec((tile, width),
                            lambda b, lb, n, sl: (b * n_tiles + jnp.maximum(n - back, 0), col0 * n_lb + lb))

    out = pl.BlockSpec((tile, width), lambda b, lb, n, sl: (b * n_tiles + n, lb))
    grid_spec = pltpu.PrefetchScalarGridSpec(
        num_scalar_prefetch=1,
        grid=(bsz, n_lb, n_tiles),
        in_specs=[spec(group, 0), spec(0, 1), spec(0, 0), spec(1, 1), spec(1, 0)],
        out_specs=[out, out],
        scratch_shapes=[pltpu.VMEM((2, 2 * n_pairs, ATT_STEPS, 2 * ATT_STEPS), F32)],
    )
    return pl.pallas_call(
        functools.partial(_attn_prompt_kernel, dil=dil, n_pairs=n_pairs),
        grid_spec=grid_spec,
        out_shape=[jax.ShapeDtypeStruct((bsz * seq_len, D_MODEL), F32)] * 2,
        compiler_params=_cparams("parallel", "parallel", "arbitrary"),
        name="attn_prompt",
    )(slopes, q, kv, kv, kv, kv)


def _attn_merge_kernel(o0, l0, o1, l1, o2, l2, out_ref):
    m = jnp.maximum(jnp.maximum(l0[...], l1[...]), l2[...])
    w0 = jnp.exp(l0[...] - m)
    w1 = jnp.exp(l1[...] - m)
    w2 = jnp.exp(l2[...] - m)
    out_ref[...] = (w0 * o0[...] + w1 * o1[...] + w2 * o2[...]) / (w0 + w1 + w2)


def attn_merge(parts, m_rows, *, tm):
    mp = parts[0].shape[0]
    tok = pl.BlockSpec((tm, D_MODEL), lambda i: (i, 0))
    return pl.pallas_call(
        _attn_merge_kernel,
        grid=(mp // tm,),
        in_specs=[tok] * 6,
        out_specs=tok,
        out_shape=jax.ShapeDtypeStruct((m_rows, D_MODEL), F32),
        compiler_params=_cparams("parallel"),
        name="attn_merge",
    )(*parts)


SAMPLE_PAIRS = 4


def _attn_sample_kernel(q0_ref, q1_ref, q2_ref, kc_ref, kn_ref, vc_ref, vn_ref, alias_ref, o_ref, *,
                        t_len, kv_buf, n_pairs):
    del alias_ref
    pb = pl.program_id(1)
    lane = lax.broadcasted_iota(jnp.int32, (1, LANES), 1)
    m_a = lane < HEAD
    n_rows = 6 * t_len
    ri = lax.broadcasted_iota(jnp.int32, (n_rows, 1), 0)
    t = ri % t_len
    grp = ri // (2 * t_len)
    hh = (ri // t_len) % 2
    dmask = jnp.where(grp == 0, GROUPS[0][1] - 1, jnp.where(grp == 1, GROUPS[1][1] - 1, GROUPS[2][1] - 1))
    win = jnp.where(grp == 0, GROUPS[0][0], jnp.where(grp == 1, GROUPS[1][0], GROUPS[2][0]))
    jc = lax.broadcasted_iota(jnp.int32, (n_rows, kv_buf), 1)
    jn = lax.broadcasted_iota(jnp.int32, (n_rows, LANES), 1)

    def band(dist):
        return (dist >= 0) & (dist <= win) & ((dist & dmask) == 0), dist.astype(F32)

    ok_c, dist_c = band(kv_buf + t - jc)
    ok_n, dist_n = band(t - jn)
    pad = jnp.zeros((LANES - t_len, LANES), F32)
    blk = 2 * t_len

    for p in range(n_pairs):
        ln = slice(p * LANES, (p + 1) * LANES)
        lhs = []
        for q_ref in (q0_ref, q1_ref, q2_ref):
            q = q_ref[:, ln] * (HEAD ** -0.5)
            lhs += [jnp.where(m_a, q, 0.0), jnp.where(m_a, 0.0, q)]
        lhs = jnp.concatenate(lhs, axis=0)
        kn = jnp.concatenate([kn_ref[:, ln], pad], axis=0)
        vn = jnp.concatenate([vn_ref[:, ln], pad], axis=0)
        head = (2 * (pb * n_pairs + p) + hh).astype(F32)
        slope = jnp.exp2(-8.0 * (head + 1.0) / N_HEADS)
        s_c = jnp.where(ok_c, _mm_nt(lhs, kc_ref[:, ln]) - slope * dist_c, NEG_BIG)
        s_n = jnp.where(ok_n, _mm_nt(lhs, kn) - slope * dist_n, NEG_BIG)
        m = jnp.maximum(jnp.max(s_c, axis=-1, keepdims=True), jnp.max(s_n, axis=-1, keepdims=True))
        e_c = jnp.exp(s_c - m)
        e_n = jnp.exp(s_n - m)
        l = jnp.sum(e_c, axis=-1, keepdims=True) + jnp.sum(e_n, axis=-1, keepdims=True)
        acc = _mm(e_c, vc_ref[:, ln]) + _mm(e_n, vn)
        m_g = [m[g * blk:(g + 1) * blk] for g in range(3)]
        m_all = jnp.maximum(jnp.maximum(m_g[0], m_g[1]), m_g[2])
        num = 0.0
        den = 0.0
        for g in range(3):
            w = jnp.exp(m_g[g] - m_all)
            num = num + w * acc[g * blk:(g + 1) * blk]
            den = den + w * l[g * blk:(g + 1) * blk]
        res = num / den
        o_ref[:, ln] = jnp.where(m_a, res[:t_len], res[t_len:])


def attn_sample(q, kv, cache_k, cache_v, out, *, row0, bsz, t_len):
    kv_buf = cache_k.shape[1]
    assert kv_buf >= GROUPS[-1][0] and kv_buf % LANES == 0 and row0 % t_len == 0 and t_len % 8 == 0
    blk0 = row0 // t_len
    width = SAMPLE_PAIRS * LANES
    n_lb = D_MODEL // width
    qs = [pl.BlockSpec((t_len, width), lambda b, p, g=g: (blk0 + b, g * n_lb + p)) for g in range(3)]
    cache = pl.BlockSpec((None, kv_buf, width), lambda b, p: (b, 0, p))
    k_new = pl.BlockSpec((t_len, width), lambda b, p: (blk0 + b, p))
    v_new = pl.BlockSpec((t_len, width), lambda b, p: (blk0 + b, n_lb + p))
    return pl.pallas_call(
        functools.partial(_attn_sample_kernel, t_len=t_len, kv_buf=kv_buf, n_pairs=SAMPLE_PAIRS),
        grid=(bsz, n_lb),
        in_specs=qs + [cache, k_new, cache, v_new, pl.BlockSpec(memory_space=pl.ANY)],
        out_specs=pl.BlockSpec((t_len, width), lambda b, p: (blk0 + b, p)),
        out_shape=jax.ShapeDtypeStruct(out.shape, F32),
        input_output_aliases={7: 0},
        compiler_params=_cparams("parallel", "parallel"),
        name="attn_sample",
    )(q, q, q, cache_k, kv, cache_v, kv, out)


def _pad_cols(w):
    return jnp.pad(w, ((0, 0), (0, LORA_PAD - w.shape[1]))).astype(BF16)


def _pad_rows(w):
    return jnp.pad(w, ((0, LORA_PAD - w.shape[0]), (0, 0))).astype(BF16)


def kernel(x_prompt, x_sample, state_wkv, state_shift, cache_k, cache_v, ln_g, ln_b, rw_mu, rw_wr, rw_wk, rw_wv,
           rw_wo, rw_w0, rw_w1, rw_w2, rw_a0, rw_a1, rw_a2, rw_v0, rw_v1, rw_v2, rw_g1, rw_g2, rw_kk, rw_ka,
           rw_rk, rw_gn_g, rw_gn_b, kv_w, att_wq, att_wo, moe_wr, moe_br, moe_win, moe_bin, moe_wout, moe_bout):
    bp, seq_len, d = x_prompt.shape
    bs, dec_len, _ = x_sample.shape
    kv_buf = cache_k.shape[1]
    mp = bp * seq_len
    ms = bs * dec_len
    m_rows = mp + ms
    tm = TOKEN_TILE
    assert d == D_MODEL and m_rows % tm == 0 and mp % tm == 0
    dense_tile = DENSE_TILE if m_rows % DENSE_TILE == 0 else tm
    t_block = min(SCAN_T_BLOCK, seq_len)

    x = jnp.concatenate([x_prompt.reshape(mp, d), x_sample.reshape(ms, d)], axis=0)
    wkv_p, wkv_s, shift_p, shift_s = [], [], [], []
    v_first = None
    kv = None

    def moe(layer, x1, gate_l, idx_l):
        return moe_layer(x1, gate_l, idx_l, moe_win, moe_bin[layer], moe_wout, moe_bout[layer],
                         jnp.stack([ln_g[layer, 1], ln_b[layer, 1]]), tm=tm, layer=layer)

    def post(layer, y, x_in, wo):
        wr = moe_wr[layer]
        wr_hi = wr.astype(BF16)
        wr_lo = (wr - wr_hi.astype(F32)).astype(BF16)
        padc = lambda w: jnp.pad(w, ((0, 0), (0, LANES - N_EXPERTS)))
        br = jnp.pad(moe_br[layer], (0, LANES - N_EXPERTS)).reshape(1, LANES)
        return post_mix(y, x_in, wo.astype(BF16), jnp.stack([ln_g[layer, 0], ln_b[layer, 0]]),
                        padc(wr_hi), padc(wr_lo), br, tm=tm)

    for layer in range(DEPTH):
        if layer < N_A_LAYERS:
            i = layer
            xp3 = x[:mp].reshape(bp, seq_len, d)
            xs3 = x[mp:].reshape(bs, dec_len, d)
            shift_p.append(xp3[:, -1])
            shift_s.append(xs3[:, -1])
            prev_p = jnp.concatenate([jnp.zeros((bp, 1, d), F32), xp3[:, :-1]], axis=1)
            prev_s = jnp.concatenate([state_shift[i][:, None, :], xs3[:, :-1]], axis=1)
            x_prev = jnp.concatenate([prev_p.reshape(mp, d), prev_s.reshape(ms, d)], axis=0)
            vec = jnp.stack([rw_w0[i], rw_a0[i], rw_v0[i - 1] if i > 0 else jnp.zeros((d,), F32)])
            mats = [rw_wr[i].astype(BF16), rw_wk[i].astype(BF16), rw_wv[i].astype(BF16),
                    _pad_cols(rw_w1[i]), _pad_rows(rw_w2[i]), _pad_cols(rw_a1[i]), _pad_rows(rw_a2[i]),
                    _pad_cols(rw_g1[i]), _pad_rows(rw_g2[i])]
            if i > 0:
                mats += [_pad_cols(rw_v1[i - 1]), _pad_rows(rw_v2[i - 1])]
            r, k, v, a, ld, g = a_proj(x, x_prev, v_first, rw_mu[i], vec, mats, tm=tm)
            if i == 0:
                v_first = v
            prm = jnp.stack([rw_kk[i], rw_ka[i], rw_rk[i].reshape(d), rw_gn_g[i], rw_gn_b[i]])
            seqs = (r, k, v, a, ld, g)
            y, sp = wkv_scan(seqs, prm, jnp.zeros((bp, PAIRS, LANES, LANES), F32), row0=0, t_len=seq_len,
                             chunk=SCAN_CHUNK, t_block=t_block)
            y, ss = wkv_scan(seqs, prm, pair_states(state_wkv[i]), row0=mp, t_len=dec_len,
                             chunk=dec_len, t_block=dec_len, out=y)
            wkv_p.append(unpair_states(sp))
            wkv_s.append(unpair_states(ss))
            x1, gate_l, idx_l = post(layer, y, x, rw_wo[i])
        else:
            j = layer - N_A_LAYERS
            q = dense(x, att_wq[j].astype(BF16), tm=dense_tile, tn=D_MODEL)
            parts = []
            for gi, (window, dil) in enumerate(GROUPS):
                assert window // dil == ATT_STEPS
                parts += list(attn_prompt_group(q, kv, group=gi, bsz=bp, seq_len=seq_len, dil=dil))
            y = attn_merge(parts, m_rows, tm=tm)
            y = attn_sample(q, kv, cache_k.reshape(bs, kv_buf, d), cache_v.reshape(bs, kv_buf, d), y,
                            row0=mp, bsz=bs, t_len=dec_len)
            x1, gate_l, idx_l = post(layer, y, x, att_wo[j])
        x = moe(layer, x1, gate_l, idx_l)
        if layer == N_A_LAYERS - 1:
            kv = dense(x, kv_w.astype(BF16), tm=dense_tile, tn=D_MODEL)

    buf_p = min(GROUPS[-1][0], seq_len)
    heads = lambda t, n: t.reshape(t.shape[0], n, N_HEADS, HEAD)
    kv_p = kv[:mp].reshape(bp, seq_len, 2 * d)[:, -buf_p:]
    kv_s = kv[mp:].reshape(bs, dec_len, 2 * d)
    k_p_out = heads(kv_p[:, :, :d], buf_p)
    v_p_out = heads(kv_p[:, :, d:], buf_p)
    k_s_out = jnp.concatenate([cache_k, heads(kv_s[:, :, :d], dec_len)], axis=1)[:, -kv_buf:]
    v_s_out = jnp.concatenate([cache_v, heads(kv_s[:, :, d:], dec_len)], axis=1)[:, -kv_buf:]
    return (x[:mp].reshape(bp, seq_len, d), x[mp:].reshape(bs, dec_len, d),
            jnp.stack(wkv_p), jnp.stack(shift_p), k_p_out, v_p_out,
            jnp.stack(wkv_s), jnp.stack(shift_s), k_s_out, v_s_out)
```

```python
import functools

import jax
import jax.numpy as jnp
from jax import lax
from jax.experimental import pallas as pl
from jax.experimental.pallas import tpu as pltpu

F32 = jnp.float32
BF16 = jnp.bfloat16

D_MODEL = 1024
HEAD = 64
N_HEADS = D_MODEL // HEAD
LANES = 128
PAIRS = D_MODEL // LANES
DEPTH = 4
N_A_LAYERS = DEPTH // 2
LORA_PAD = 128
GN_EPS = 64e-5
LN_EPS = 1e-5
DN_ALPHA = (2 * DEPTH) ** 0.25
GROUPS = ((128, 1), (512, 4), (2048, 16))
ATT_STEPS = 128
N_EXPERTS = 32
TOP_K = 4
SWIGLU_LIMIT = 7.0
SWIGLU_ALPHA = 1.702
MOE_ROWS = 256
TOKEN_TILE = 256
DENSE_TILE = 1280
SCAN_CHUNK = 64
SCAN_T_BLOCK = 256
NEG_BIG = -1e30
VMEM_LIMIT = 56 * 1024 * 1024


def _cparams(*sem):
    return pltpu.CompilerParams(dimension_semantics=sem, vmem_limit_bytes=VMEM_LIMIT)


def _mm(a, b):
    return jnp.dot(a.astype(BF16), b.astype(BF16), preferred_element_type=F32)


def _mm_nt(a, b):
    return lax.dot_general(a.astype(BF16), b.astype(BF16), (((1,), (1,)), ((), ())),
                           preferred_element_type=F32)


def _mm_tn(a, b):
    return lax.dot_general(a.astype(BF16), b.astype(BF16), (((0,), (0,)), ((), ())),
                           preferred_element_type=F32)


def _split(x, parts):
    out = []
    for _ in range(parts):
        h = x.astype(BF16)
        out.append(h)
        x = x - h.astype(F32)
    return out


def _mm_sel_r(x, sel, parts=2):
    acc = None
    for h in _split(x, parts):
        t = jnp.dot(h, sel, preferred_element_type=F32)
        acc = t if acc is None else acc + t
    return acc


def _mm_sel_l(sel, x, parts=3):
    acc = None
    for h in _split(x, parts):
        t = jnp.dot(sel, h, preferred_element_type=F32)
        acc = t if acc is None else acc + t
    return acc


def _sigmoid(x):
    return 1.0 / (1.0 + jnp.exp(-x))


def _layer_norm(x, g, b):
    mu = jnp.mean(x, axis=-1, keepdims=True)
    xc = x - mu
    var = jnp.mean(xc * xc, axis=-1, keepdims=True)
    return xc * lax.rsqrt(var + LN_EPS) * g + b


def _each(fn, *lists):
    return [fn(*xs) for xs in zip(*lists)]


def _unit_lower_inverse(a_side, cst, chunk):
    eye, row, scol, bd = cst
    blk = (row // 8) == (scol // 8)

    def mul(xs, ys):
        return _each(_mm, xs, [bd(y) for y in ys])

    x = [jnp.where(blk, -a, 0.0) for a in a_side]
    x2 = mul(x, x)
    x4 = mul(x2, x2)
    xx2 = mul(x, x2)
    y = _each(lambda x_, x2_, xx2_: eye + x_ + x2_ + xx2_, x, x2, xx2)
    t = _each(jnp.add, y, mul(y, x4))
    s = 8
    while s < chunk:
        rb = row // s
        off = (rb == (scol // s) + 1) & ((rb % 2) == 1)
        a_off = [jnp.where(off, a, 0.0) for a in a_side]
        t = _each(jnp.subtract, t, mul(t, mul(a_off, t)))
        s *= 2
    return t


def _wkv_chunk(s_mat, r, kr, v, a, ld, g, prm, cst, chunk):
    kk_p, ka_p, rk_p, gng, gnb = prm
    m_a, e_seg, e_seg2, tri, eye, row, scol, strict, incl, bd = cst
    c = chunk
    kkr = _each(jnp.multiply, kr, kk_p)
    ss = [_mm_sel_r(x * x, e_seg) for x in kkr]
    kk = _each(lambda x, s_: x / jnp.maximum(jnp.sqrt(s_), 1e-12), kkr, ss)
    k = _each(lambda kr_, a_, ka_: kr_ * (1.0 + (a_ - 1.0) * ka_), kr, a, ka_p)
    b = _each(jnp.multiply, kk, a)
    cl = [_mm_sel_l(tri, x) for x in ld]
    cl_end = [x[c - 1:c, :] for x in cl]

    def stack(x):
        return jnp.concatenate([jnp.where(m_a, x, 0.0), jnp.where(m_a, 0.0, x)], axis=0)

    kkg = _each(lambda kk_, cl_, ld_: kk_ * jnp.exp(cl_ - ld_), kk, cl, ld)
    rg = _each(lambda r_, cl_: r_ * jnp.exp(cl_), r, cl)
    g_inv = [jnp.exp(-x) for x in cl]
    bd_s = _each(lambda b_, gi: stack(b_ * gi), b, g_inv)
    kd_s = _each(lambda k_, gi: stack(k_ * gi), k, g_inv)
    g_end = _each(lambda ce, cl_: jnp.exp(ce - cl_), cl_end, cl)
    be_s = _each(lambda b_, ge: stack(b_ * ge), b, g_end)
    ke_s = _each(lambda k_, ge: stack(k_ * ge), k, g_end)
    v_s = [stack(x) for x in v]

    if (2 * c) % LANES == 0:
        bk_s = _each(lambda x_, y_: jnp.concatenate([x_, y_], axis=0), bd_s, kd_s)
        ab = _each(_mm_nt, kkg, bk_s)
        rbk = _each(_mm_nt, rg, bk_s)
        a_side, b_side = [x[:, :2 * c] for x in ab], [x[:, 2 * c:] for x in ab]
        rb_side, rk_side = [x[:, :2 * c] for x in rbk], [x[:, 2 * c:] for x in rbk]
    else:
        a_side, b_side = _each(_mm_nt, kkg, bd_s), _each(_mm_nt, kkg, kd_s)
        rb_side, rk_side = _each(_mm_nt, rg, bd_s), _each(_mm_nt, rg, kd_s)
    a_side = [jnp.where(strict, x, 0.0) for x in a_side]
    b_side = [jnp.where(strict, x, 0.0) for x in b_side]
    rb_side = [jnp.where(incl, x, 0.0) for x in rb_side]
    rk_side = [jnp.where(incl, x, 0.0) for x in rk_side]
    t_side = _unit_lower_inverse(a_side, (eye, row, scol, bd), c)

    bv = _each(_mm, b_side, v_s)
    gu = _each(lambda t_, kkg_, bv_: _mm(t_, jnp.concatenate([stack(kkg_), stack(bv_)], axis=1)), t_side, kkg, bv)
    g_s = [stack(x[:, :LANES]) for x in gu]
    u1_s = [stack(x[:, LANES:]) for x in gu]
    pq = _each(lambda rb_, gs_, u1_: _mm(rb_, jnp.concatenate([gs_, u1_], axis=1)), rb_side, g_s, u1_s)
    rkv = _each(_mm, rk_side, v_s)
    p = _each(lambda rg_, pq_: rg_ - pq_[:, :LANES], rg, pq)
    q = _each(lambda rkv_, pq_: rkv_ - pq_[:, LANES:], rkv, pq)
    o = _each(lambda p_, s_, q_: _mm_nt(p_, s_) + q_, p, s_mat, q)
    gb = _each(_mm_tn, g_s, be_s)
    nt = _each(lambda vs_, u1_, ke_, be_: _mm_tn(jnp.concatenate([vs_, -u1_], axis=0),
                                                   jnp.concatenate([ke_, be_], axis=0)), v_s, u1_s, ke_s, be_s)
    s_new = _each(lambda s_, ce, gb_, nt_: s_ * jnp.exp(ce) - _mm(s_, gb_) + nt_, s_mat, cl_end, gb, nt)

    mb = _each(lambda o_, r_, k_, rk_: _mm_sel_r(jnp.concatenate([o_, r_ * k_ * rk_], axis=1), e_seg2),
               o, r, k, rk_p)
    d = _each(lambda o_, mb_: o_ - mb_[:, :LANES] * (1.0 / HEAD), o, mb)
    var = [_mm_sel_r(x * x, e_seg) * (1.0 / HEAD) for x in d]
    out = _each(lambda d_, var_, gg, gb_, mb_, v_, g_:
                (d_ * lax.rsqrt(var_ + GN_EPS) * gg + gb_ + mb_[:, LANES:] * v_) * g_,
                d, var, gng, gnb, mb, v, g)
    return s_new, out


def _wkv_kernel(r_ref, k_ref, v_ref, a_ref, ld_ref, g_ref, prm_ref, s0_ref, *rest, chunk, n_chunks, n_pairs):
    o_ref, s_out_ref, s_scr = rest[-3:]
    tb = pl.program_id(2)

    @pl.when(tb == 0)
    def _():
        s_scr[...] = s0_ref[0]

    c2 = 2 * chunk
    lane = lax.broadcasted_iota(jnp.int32, (1, LANES), 1)
    m_a = lane < HEAD
    er = lax.broadcasted_iota(jnp.int32, (2 * LANES, 2 * LANES), 0)
    ec = lax.broadcasted_iota(jnp.int32, (2 * LANES, 2 * LANES), 1)
    e_seg2 = ((er // HEAD) == (ec // HEAD)).astype(BF16)
    e_seg = e_seg2[:LANES, :LANES]
    tr = lax.broadcasted_iota(jnp.int32, (chunk, chunk), 0)
    tc = lax.broadcasted_iota(jnp.int32, (chunk, chunk), 1)
    tri = (tr >= tc).astype(BF16)
    row = lax.broadcasted_iota(jnp.int32, (chunk, c2), 0)
    col = lax.broadcasted_iota(jnp.int32, (chunk, c2), 1)
    scol = col % chunk
    strict = scol < row
    incl = scol <= row
    eye = (row == scol).astype(F32)
    left = col < chunk

    def bd(x):
        return jnp.concatenate([jnp.where(left, x, 0.0), jnp.where(left, 0.0, x)], axis=0)

    cst = (m_a, e_seg, e_seg2, tri, eye, row, scol, strict, incl, bd)

    def body(ci, carry):
        sl = pl.ds(pl.multiple_of(ci * chunk, chunk), chunk)
        lanes = [slice(p * LANES, (p + 1) * LANES) for p in range(n_pairs)]
        prm = tuple([prm_ref[i:i + 1, ln] for ln in lanes] for i in range(5))
        seqs = [[ref[sl, ln] for ln in lanes] for ref in (r_ref, k_ref, v_ref, a_ref, ld_ref, g_ref)]
        s_new, out = _wkv_chunk([s_scr[p] for p in range(n_pairs)], *seqs, prm, cst, chunk)
        for p in range(n_pairs):
            s_scr[p] = s_new[p]
            o_ref[sl, lanes[p]] = out[p]
        return carry

    lax.fori_loop(0, n_chunks, body, 0)

    @pl.when(tb == pl.num_programs(2) - 1)
    def _():
        s_out_ref[0] = s_scr[...]


def wkv_scan(seqs, prm, s0, *, row0, t_len, chunk, t_block, n_pairs=PAIRS, out=None):
    m_rows, d = seqs[0].shape
    bsz = s0.shape[0]
    assert d == D_MODEL and t_len % t_block == 0 and t_block % chunk == 0 and row0 % t_block == 0
    assert PAIRS % n_pairs == 0 and row0 + bsz * t_len <= m_rows
    width = n_pairs * LANES
    nt = t_len // t_block
    blk0 = row0 // t_block
    seq = pl.BlockSpec((t_block, width), lambda b, p, t: (blk0 + b * nt + t, p))
    st = pl.BlockSpec((1, n_pairs, LANES, LANES), lambda b, p, t: (b, p, 0, 0))
    in_specs = [seq] * 6 + [pl.BlockSpec((5, width), lambda b, p, t: (0, p)), st]
    args = list(seqs) + [prm, s0]
    aliases = {}
    if out is not None:
        in_specs.append(pl.BlockSpec(memory_space=pl.ANY))
        args.append(out)
        aliases = {len(args) - 1: 0}
    return pl.pallas_call(
        functools.partial(_wkv_kernel, chunk=chunk, n_chunks=t_block // chunk, n_pairs=n_pairs),
        grid=(bsz, PAIRS // n_pairs, nt),
        in_specs=in_specs,
        out_specs=[seq, st],
        out_shape=[jax.ShapeDtypeStruct((m_rows, d), F32),
                   jax.ShapeDtypeStruct((bsz, PAIRS, LANES, LANES), F32)],
        scratch_shapes=[pltpu.VMEM((n_pairs, LANES, LANES), F32)],
        input_output_aliases=aliases,
        compiler_params=_cparams("parallel", "parallel", "arbitrary"),
        name="wkv_scan",
    )(*args)


def pair_states(s):
    bsz = s.shape[0]
    s = s.reshape(bsz, PAIRS, 2, HEAD, HEAD)
    z = jnp.zeros_like(s[:, :, 0])
    top = jnp.concatenate([s[:, :, 0], z], axis=-1)
    bot = jnp.concatenate([z, s[:, :, 1]], axis=-1)
    return jnp.concatenate([top, bot], axis=-2)


def unpair_states(sp):
    bsz = sp.shape[0]
    s = jnp.stack([sp[:, :, :HEAD, :HEAD], sp[:, :, HEAD:, HEAD:]], axis=2)
    return s.reshape(bsz, N_HEADS, HEAD, HEAD)


def _a_proj_kernel(*refs, has_vres):
    if has_vres:
        (x_ref, xp_ref, vf_ref, mu_ref, vec_ref, wr, wk, wv, w1, w2, a1, a2, g1, g2, v1, v2,
         r_o, k_o, v_o, a_o, ld_o, g_o) = refs
    else:
        (x_ref, xp_ref, mu_ref, vec_ref, wr, wk, wv, w1, w2, a1, a2, g1, g2,
         r_o, k_o, v_o, a_o, ld_o, g_o) = refs
    x = x_ref[...]
    xx = xp_ref[...] - x
    xr, xw, xk, xv, xa, xg = [(x + xx * mu_ref[i:i + 1, :]).astype(BF16) for i in range(6)]
    r_o[...] = _mm(xr, wr[...])
    k_o[...] = _mm(xk, wk[...])
    v = _mm(xv, wv[...])
    z = vec_ref[0:1, :] + _mm(jnp.tanh(_mm(xw, w1[...])), w2[...])
    softplus_neg = jnp.maximum(-z, 0.0) + jnp.log(1.0 + jnp.exp(-jnp.abs(z)))
    ld_o[...] = -jnp.exp(-softplus_neg - 0.5)
    if has_vres:
        mix = _sigmoid(vec_ref[2:3, :] + _mm(_mm(xv, v1[...]), v2[...]))
        v = v + (vf_ref[...] - v) * mix
    v_o[...] = v
    a_o[...] = _sigmoid(vec_ref[1:2, :] + _mm(_mm(xa, a1[...]), a2[...]))
    g_o[...] = _mm(_sigmoid(_mm(xg, g1[...])), g2[...])


def a_proj(x, x_prev, v_first, mu, vec, mats, *, tm):
    m_rows = x.shape[0]
    assert m_rows % tm == 0
    tok = pl.BlockSpec((tm, D_MODEL), lambda i: (i, 0))
    full = lambda a: pl.BlockSpec(a.shape, lambda i: (0, 0))
    has_vres = v_first is not None
    acts = [x, x_prev] + ([v_first] if has_vres else [])
    consts = [mu, vec] + list(mats)
    return pl.pallas_call(
        functools.partial(_a_proj_kernel, has_vres=has_vres),
        grid=(m_rows // tm,),
        in_specs=[tok] * len(acts) + [full(c) for c in consts],
        out_specs=[tok] * 6,
        out_shape=[jax.ShapeDtypeStruct((m_rows, D_MODEL), F32)] * 6,
        compiler_params=_cparams("parallel"),
        name="a_proj",
    )(*acts, *consts)


def _post_kernel(*refs, n_prompt_tiles):
    if n_prompt_tiles is None:
        y_ref = refs[0]
        y_in = y_ref[...]
        rest = refs[1:]
    else:
        o0, l0, o1, l1, o2, l2, ys_ref = refs[:7]
        rest = refs[7:]
        m = jnp.maximum(jnp.maximum(l0[...], l1[...]), l2[...])
        w0 = jnp.exp(l0[...] - m)
        w1 = jnp.exp(l1[...] - m)
        w2 = jnp.exp(l2[...] - m)
        merged = (w0 * o0[...] + w1 * o1[...] + w2 * o2[...]) / (w0 + w1 + w2)
        y_in = jnp.where(pl.program_id(0) < n_prompt_tiles, merged, ys_ref[...])
    x_ref, wo_ref, ln_ref, wrh_ref, wrl_ref, br_ref, x1_ref, gate_ref, idx_ref = rest
    y = _mm(y_in, wo_ref[...])
    x1 = _layer_norm(DN_ALPHA * x_ref[...] + y, ln_ref[0:1, :], ln_ref[1:2, :])
    x1_ref[...] = x1
    parts = _split(x1, 3)
    acc = br_ref[...]
    for h in parts:
        acc = acc + jnp.dot(h, wrh_ref[...], preferred_element_type=F32)
    for h in parts[:2]:
        acc = acc + jnp.dot(h, wrl_ref[...], preferred_element_type=F32)
    lane = lax.broadcasted_iota(jnp.int32, acc.shape, 1)
    lane_f = lane.astype(F32)
    lg = jnp.where(lane < N_EXPERTS, acc, -jnp.inf)
    vals = []
    idx_out = jnp.zeros(acc.shape, F32)
    for k in range(TOP_K):
        v = jnp.max(lg, axis=-1, keepdims=True)
        idx = jnp.min(jnp.where(lg == v, lane_f, float(LANES)), axis=-1, keepdims=True)
        vals.append(v)
        idx_out = jnp.where(lane == k, idx, idx_out)
        lg = jnp.where(lane_f == idx, -jnp.inf, lg)
    es = [jnp.exp(v - vals[0]) for v in vals]
    den = es[0]
    for e in es[1:]:
        den = den + e
    gate = jnp.zeros(acc.shape, F32)
    for k in range(TOP_K):
        gate = jnp.where(lane == k, es[k] / den, gate)
    gate_ref[...] = gate
    idx_ref[...] = idx_out.astype(jnp.int32)


def post_mix(y, x, wo, ln, wr_hi, wr_lo, br, *, tm):
    m_rows = x.shape[0]
    tok = pl.BlockSpec((tm, D_MODEL), lambda i: (i, 0))
    full = lambda a: pl.BlockSpec(a.shape, lambda i: (0, 0))
    if isinstance(y, tuple):
        parts, y_sample = y
        n_p = parts[0].shape[0] // tm
        assert parts[0].shape[0] % tm == 0 and y_sample.shape[0] % tm == 0
        prompt = pl.BlockSpec((tm, D_MODEL), lambda i: (jnp.minimum(i, n_p - 1), 0))
        sample = pl.BlockSpec((tm, D_MODEL), lambda i: (jnp.maximum(i - n_p, 0), 0))
        y_args, y_specs = list(parts) + [y_sample], [prompt] * 6 + [sample]
    else:
        n_p = None
        y_args, y_specs = [y], [tok]
    return pl.pallas_call(
        functools.partial(_post_kernel, n_prompt_tiles=n_p),
        grid=(m_rows // tm,),
        in_specs=y_specs + [tok] + [full(c) for c in (wo, ln, wr_hi, wr_lo, br)],
        out_specs=[tok, pl.BlockSpec((tm, LANES), lambda i: (i, 0)), pl.BlockSpec((tm, LANES), lambda i: (i, 0))],
        out_shape=[jax.ShapeDtypeStruct((m_rows, D_MODEL), F32),
                   jax.ShapeDtypeStruct((m_rows, LANES), F32),
                   jax.ShapeDtypeStruct((m_rows, LANES), jnp.int32)],
        compiler_params=_cparams("parallel"),
        name="post_mix",
    )(*y_args, x, wo, ln, wr_hi, wr_lo, br)


def _moe_kernel(be_ref, first_ref, nact_ref, xb_ref, win_ref, bin_ref, wout_ref, bout_ref, y_ref, win_s, wout_s):
    i = pl.program_id(0)
    slab = 128

    @pl.when(first_ref[i] == 1)
    def _():
        for j in range(D_MODEL // slab):
            rows = slice(j * slab, (j + 1) * slab)
            win_s[rows, :] = win_ref[0, rows, :].astype(BF16)
            wout_s[rows, :] = wout_ref[0, rows, :].astype(BF16)

    @pl.when(i < nact_ref[0])
    def _():
        h = jnp.dot(xb_ref[...].astype(BF16), win_s[...], preferred_element_type=F32) + bin_ref[0]
        h_gate = jnp.minimum(h[:, :D_MODEL], SWIGLU_LIMIT)
        h_up = jnp.clip(h[:, D_MODEL:], -SWIGLU_LIMIT, SWIGLU_LIMIT)
        act = (h_up + 1.0) * h_gate * _sigmoid(SWIGLU_ALPHA * h_gate)
        y_ref[...] = jnp.dot(act.astype(BF16), wout_s[...], preferred_element_type=F32) + bout_ref[0]


def moe_experts(block_e, first, n_act, xb, w_in, b_in, w_out, b_out, *, layer):
    rows = xb.shape[0]
    n_blocks = rows // MOE_ROWS
    grid_spec = pltpu.PrefetchScalarGridSpec(
        num_scalar_prefetch=3,
        grid=(n_blocks,),
        in_specs=[
            pl.BlockSpec((MOE_ROWS, D_MODEL), lambda i, be, fi, na: (i, 0)),
            pl.BlockSpec((None, 1, D_MODEL, 2 * D_MODEL), lambda i, be, fi, na: (layer, be[i], 0, 0)),
            pl.BlockSpec((1, 1, 2 * D_MODEL), lambda i, be, fi, na: (be[i], 0, 0)),
            pl.BlockSpec((None, 1, D_MODEL, D_MODEL), lambda i, be, fi, na: (layer, be[i], 0, 0)),
            pl.BlockSpec((1, 1, D_MODEL), lambda i, be, fi, na: (be[i], 0, 0)),
        ],
        out_specs=pl.BlockSpec((MOE_ROWS, D_MODEL), lambda i, be, fi, na: (i, 0)),
        scratch_shapes=[pltpu.VMEM((D_MODEL, 2 * D_MODEL), BF16), pltpu.VMEM((D_MODEL, D_MODEL), BF16)],
    )
    return pl.pallas_call(
        _moe_kernel,
        grid_spec=grid_spec,
        out_shape=jax.ShapeDtypeStruct((rows, D_MODEL), F32),
        compiler_params=_cparams("arbitrary"),
        name="moe_experts",
    )(block_e, first, n_act, xb, w_in, b_in.reshape(N_EXPERTS, 1, -1), w_out, b_out.reshape(N_EXPERTS, 1, -1))


def _combine_kernel(yg_ref, gate_ref, x_ref, ln_ref, o_ref):
    gate = gate_ref[...]
    ffn = gate[:, 0:1] * yg_ref[0]
    for k in range(1, TOP_K):
        ffn = ffn + gate[:, k:k + 1] * yg_ref[k]
    o_ref[...] = _layer_norm(DN_ALPHA * x_ref[...] + ffn, ln_ref[0:1, :], ln_ref[1:2, :])


def moe_combine(yg, gate, x, ln, *, tm):
    m_rows = x.shape[0]
    tok = pl.BlockSpec((tm, D_MODEL), lambda i: (i, 0))
    return pl.pallas_call(
        _combine_kernel,
        grid=(m_rows // tm,),
        in_specs=[pl.BlockSpec((TOP_K, tm, D_MODEL), lambda i: (0, i, 0)),
                  pl.BlockSpec((tm, TOP_K), lambda i: (i, 0)), tok,
                  pl.BlockSpec(ln.shape, lambda i: (0, 0))],
        out_specs=tok,
        out_shape=jax.ShapeDtypeStruct((m_rows, D_MODEL), F32),
        compiler_params=_cparams("parallel"),
        name="moe_combine",
    )(yg, gate, x, ln)


def _route(top_e, m_rows):
    mk = m_rows * TOP_K
    flat_e = top_e.reshape(-1).astype(jnp.int32)
    onehot = (flat_e[:, None] == jnp.arange(N_EXPERTS, dtype=jnp.int32)[None, :]).astype(jnp.int32)
    csum = jnp.cumsum(onehot, axis=0)
    counts = csum[-1]
    rank = jnp.sum((csum - onehot) * onehot, axis=1)
    padded = (counts + MOE_ROWS - 1) // MOE_ROWS * MOE_ROWS
    pad_end = jnp.cumsum(padded)
    pad_start = pad_end - padded
    start = jnp.cumsum(counts) - counts
    pos = jnp.sum(onehot * pad_start[None, :], axis=1) + rank
    order = jnp.argsort(flat_e).astype(jnp.int32)
    n_blocks = -(-mk // MOE_ROWS) + N_EXPERTS
    blk_start = jnp.arange(n_blocks, dtype=jnp.int32) * MOE_ROWS
    block_e = jnp.sum((pad_end[None, :] <= blk_start[:, None]).astype(jnp.int32), axis=1)
    block_e = jnp.minimum(block_e, N_EXPERTS - 1)
    first = ((blk_start == pad_start[block_e]) & (blk_start < pad_end[-1])).astype(jnp.int32)
    n_act = (pad_end[-1:] // MOE_ROWS).astype(jnp.int32)
    e_row = jnp.repeat(block_e, MOE_ROWS)
    j_row = jnp.arange(n_blocks * MOE_ROWS, dtype=jnp.int32) - pad_start[e_row]
    compact = jnp.clip(start[e_row] + j_row, 0, mk - 1)
    row_tok = jnp.where(j_row < counts[e_row], order[compact] // TOP_K, 0)
    return row_tok, pos, block_e, first, n_act


def moe_layer(x1, gate_l, idx_l, w_in, b_in, w_out, b_out, ln, *, tm, layer):
    m_rows = x1.shape[0]
    gate = gate_l[:, :TOP_K]
    row_tok, pos, block_e, first, n_act = _route(idx_l[:, :TOP_K], m_rows)
    xb = x1[row_tok]
    yb = moe_experts(block_e, first, n_act, xb, w_in, b_in, w_out, b_out, layer=layer)
    yg = yb[pos.reshape(m_rows, TOP_K).T]
    return moe_combine(yg, gate, x1, ln, tm=tm)


def _dense_kernel(x_ref, w_ref, o_ref):
    o_ref[...] = _mm(x_ref[...], w_ref[...]).astype(o_ref.dtype)


def dense(x, w, *, tm, tn, out_dtype=F32):
    m_rows, k_dim = x.shape
    n_dim = w.shape[1]
    assert m_rows % tm == 0 and n_dim % tn == 0
    return pl.pallas_call(
        _dense_kernel,
        grid=(n_dim // tn, m_rows // tm),
        in_specs=[pl.BlockSpec((tm, k_dim), lambda j, i: (i, 0)),
                  pl.BlockSpec((k_dim, tn), lambda j, i: (0, j))],
        out_specs=pl.BlockSpec((tm, tn), lambda j, i: (i, j)),
        out_shape=jax.ShapeDtypeStruct((m_rows, n_dim), out_dtype),
        compiler_params=_cparams("parallel", "parallel"),
        name="dense",
    )(x, w)


def _slope(head):
    return 2.0 ** (-8.0 * (head + 1) / N_HEADS)


def _attn_prompt_kernel(slope_ref, q_ref, kp_ref, kc_ref, vp_ref, vc_ref, o_ref, lse_ref, bias_scr, *, dil, n_pairs):
    lb = pl.program_id(1)
    n = pl.program_id(2)
    nk = ATT_STEPS
    lane = lax.broadcasted_iota(jnp.int32, (1, LANES), 1)
    m_a = lane < HEAD

    @pl.when(n == 0)
    def _():
        qi = lax.broadcasted_iota(jnp.int32, (nk, 2 * nk), 0)
        kj = lax.broadcasted_iota(jnp.int32, (nk, 2 * nk), 1)
        delta = qi + nk - kj
        valid = (delta >= 0) & (delta <= nk)
        dist = (delta * dil).astype(F32)
        for h in range(2 * n_pairs):
            bias = jnp.where(valid, -slope_ref[lb * 2 * n_pairs + h] * dist, NEG_BIG)
            bias_scr[1, h] = bias
            bias_scr[0, h] = jnp.where(kj >= nk, bias, NEG_BIG)

    table = jnp.minimum(n, 1)

    def scores(item):
        rows, p = item
        ln = slice(p * LANES, (p + 1) * LANES)
        q = q_ref[rows, ln] * (HEAD ** -0.5)
        k = jnp.concatenate([kp_ref[rows, ln], kc_ref[rows, ln]], axis=0).astype(BF16)
        return [_mm_nt(jnp.where(m_a, q, 0.0), k), _mm_nt(jnp.where(m_a, 0.0, q), k)]

    def run(items):
        groups = [items[i:i + 2] for i in range(0, len(items), 2)]
        s_next = [scores(it) for it in groups[0]]
        for gi, grp in enumerate(groups):
            s_cur = s_next
            if gi + 1 < len(groups):
                s_next = [scores(it) for it in groups[gi + 1]]
            lns = [slice(p * LANES, (p + 1) * LANES) for _, p in grp]
            vs = [jnp.concatenate([vp_ref[rows, ln], vc_ref[rows, ln]], axis=0).astype(BF16)
                  for (rows, _), ln in zip(grp, lns)]
            s = [s_cur[i][hh] + bias_scr[table, 2 * p + hh] for i, (_, p) in enumerate(grp) for hh in range(2)]
            m = [jnp.max(x, axis=-1, keepdims=True) for x in s]
            e = _each(lambda x, m_: jnp.exp(x - m_), s, m)
            l = [jnp.sum(x, axis=-1, keepdims=True) for x in e]
            pv = [jnp.dot(x.astype(BF16), vs[j // 2], preferred_element_type=F32) for j, x in enumerate(e)]
            outs = _each(jnp.divide, pv, l)
            lses = _each(lambda m_, l_: m_ + jnp.log(l_), m, l)
            for i, ((rows, _), ln) in enumerate(zip(grp, lns)):
                o_ref[rows, ln] = jnp.where(m_a, outs[2 * i], outs[2 * i + 1])
                lse_ref[rows, ln] = jnp.where(m_a, lses[2 * i], lses[2 * i + 1])

    if dil == 1:
        run([(slice(None), p) for p in range(n_pairs)])
    else:
        group = min(dil, ATT_CLASS_UNROLL)

        def body(gi, carry):
            run([(pl.ds(gi * group + u, nk, stride=dil), 0) for u in range(group)])
            return carry

        lax.fori_loop(0, dil // group, body, 0)


ATT_CLASS_UNROLL = 4


def attn_prompt_group(q, kv, *, group, bsz, seq_len, dil):
    tile = ATT_STEPS * dil
    assert seq_len % tile == 0
    n_tiles = seq_len // tile
    n_pairs = PAIRS if dil == 1 else 1
    width = n_pairs * LANES
    n_lb = D_MODEL // width
    slopes = jnp.asarray([_slope(h) for h in range(N_HEADS)], F32)

    def spec(col0, back):
        return pl.BlockSpec((tile, width),
                            lambda b, lb, n, sl: (b * n_tiles + jnp.maximum(n - back, 0), col0 * n_lb + lb))

    out = pl.BlockSpec((tile, width), lambda b, lb, n, sl: (b * n_tiles + n, lb))
    grid_spec = pltpu.PrefetchScalarGridSpec(
        num_scalar_prefetch=1,
        grid=(bsz, n_lb, n_tiles),
        in_specs=[spec(group, 0), spec(0, 1), spec(0, 0), spec(1, 1), spec(1, 0)],
        out_specs=[out, out],
        scratch_shapes=[pltpu.VMEM((2, 2 * n_pairs, ATT_STEPS, 2 * ATT_STEPS), F32)],
    )
    return pl.pallas_call(
        functools.partial(_attn_prompt_kernel, dil=dil, n_pairs=n_pairs),
        grid_spec=grid_spec,
        out_shape=[jax.ShapeDtypeStruct((bsz * seq_len, D_MODEL), F32)] * 2,
        compiler_params=_cparams("parallel", "parallel", "arbitrary"),
        name="attn_prompt",
    )(slopes, q, kv, kv, kv, kv)


SAMPLE_PAIRS = 4


def _attn_sample_kernel(q0_ref, q1_ref, q2_ref, kc_ref, kn_ref, vc_ref, vn_ref, o_ref, *, t_len, kv_buf, n_pairs):
    pb = pl.program_id(1)
    lane = lax.broadcasted_iota(jnp.int32, (1, LANES), 1)
    m_a = lane < HEAD
    n_rows = 6 * t_len
    ri = lax.broadcasted_iota(jnp.int32, (n_rows, 1), 0)
    t = ri % t_len
    grp = ri // (2 * t_len)
    hh = (ri // t_len) % 2
    dmask = jnp.where(grp == 0, GROUPS[0][1] - 1, jnp.where(grp == 1, GROUPS[1][1] - 1, GROUPS[2][1] - 1))
    win = jnp.where(grp == 0, GROUPS[0][0], jnp.where(grp == 1, GROUPS[1][0], GROUPS[2][0]))
    jc = lax.broadcasted_iota(jnp.int32, (n_rows, kv_buf), 1)
    jn = lax.broadcasted_iota(jnp.int32, (n_rows, LANES), 1)

    def band(dist):
        return (dist >= 0) & (dist <= win) & ((dist & dmask) == 0), dist.astype(F32)

    ok_c, dist_c = band(kv_buf + t - jc)
    ok_n, dist_n = band(t - jn)
    pad = jnp.zeros((LANES - t_len, LANES), F32)
    blk = 2 * t_len

    for p in range(n_pairs):
        ln = slice(p * LANES, (p + 1) * LANES)
        lhs = []
        for q_ref in (q0_ref, q1_ref, q2_ref):
            q = q_ref[:, ln] * (HEAD ** -0.5)
            lhs += [jnp.where(m_a, q, 0.0), jnp.where(m_a, 0.0, q)]
        lhs = jnp.concatenate(lhs, axis=0)
        kn = jnp.concatenate([kn_ref[:, ln], pad], axis=0)
        vn = jnp.concatenate([vn_ref[:, ln], pad], axis=0)
        head = (2 * (pb * n_pairs + p) + hh).astype(F32)
        slope = jnp.exp2(-8.0 * (head + 1.0) / N_HEADS)
        s_c = jnp.where(ok_c, _mm_nt(lhs, kc_ref[:, ln]) - slope * dist_c, NEG_BIG)
        s_n = jnp.where(ok_n, _mm_nt(lhs, kn) - slope * dist_n, NEG_BIG)
        m = jnp.maximum(jnp.max(s_c, axis=-1, keepdims=True), jnp.max(s_n, axis=-1, keepdims=True))
        e_c = jnp.exp(s_c - m)
        e_n = jnp.exp(s_n - m)
        l = jnp.sum(e_c, axis=-1, keepdims=True) + jnp.sum(e_n, axis=-1, keepdims=True)
        acc = _mm(e_c, vc_ref[:, ln]) + _mm(e_n, vn)
        m_g = [m[g * blk:(g + 1) * blk] for g in range(3)]
        m_all = jnp.maximum(jnp.maximum(m_g[0], m_g[1]), m_g[2])
        num = 0.0
        den = 0.0
        for g in range(3):
            w = jnp.exp(m_g[g] - m_all)
            num = num + w * acc[g * blk:(g + 1) * blk]
            den = den + w * l[g * blk:(g + 1) * blk]
        res = num / den
        o_ref[:, ln] = jnp.where(m_a, res[:t_len], res[t_len:])


def attn_sample(q, kv, cache_k, cache_v, *, row0, bsz, t_len):
    kv_buf = cache_k.shape[1]
    assert kv_buf >= GROUPS[-1][0] and kv_buf % LANES == 0 and row0 % t_len == 0 and t_len % 8 == 0
    blk0 = row0 // t_len
    width = SAMPLE_PAIRS * LANES
    n_lb = D_MODEL // width
    qs = [pl.BlockSpec((t_len, width), lambda b, p, g=g: (blk0 + b, g * n_lb + p)) for g in range(3)]
    cache = pl.BlockSpec((None, kv_buf, width), lambda b, p: (b, 0, p))
    k_new = pl.BlockSpec((t_len, width), lambda b, p: (blk0 + b, p))
    v_new = pl.BlockSpec((t_len, width), lambda b, p: (blk0 + b, n_lb + p))
    return pl.pallas_call(
        functools.partial(_attn_sample_kernel, t_len=t_len, kv_buf=kv_buf, n_pairs=SAMPLE_PAIRS),
        grid=(bsz, n_lb),
        in_specs=qs + [cache, k_new, cache, v_new],
        out_specs=pl.BlockSpec((t_len, width), lambda b, p: (b, p)),
        out_shape=jax.ShapeDtypeStruct((bsz * t_len, D_MODEL), F32),
        compiler_params=_cparams("parallel", "parallel"),
        name="attn_sample",
    )(q, q, q, cache_k, kv, cache_v, kv)


def _pad_cols(w):
    return jnp.pad(w, ((0, 0), (0, LORA_PAD - w.shape[1]))).astype(BF16)


def _pad_rows(w):
    return jnp.pad(w, ((0, LORA_PAD - w.shape[0]), (0, 0))).astype(BF16)


def kernel(x_prompt, x_sample, state_wkv, state_shift, cache_k, cache_v, ln_g, ln_b, rw_mu, rw_wr, rw_wk, rw_wv,
           rw_wo, rw_w0, rw_w1, rw_w2, rw_a0, rw_a1, rw_a2, rw_v0, rw_v1, rw_v2, rw_g1, rw_g2, rw_kk, rw_ka,
           rw_rk, rw_gn_g, rw_gn_b, kv_w, att_wq, att_wo, moe_wr, moe_br, moe_win, moe_bin, moe_wout, moe_bout):
    bp, seq_len, d = x_prompt.shape
    bs, dec_len, _ = x_sample.shape
    kv_buf = cache_k.shape[1]
    mp = bp * seq_len
    ms = bs * dec_len
    m_rows = mp + ms
    tm = TOKEN_TILE
    assert d == D_MODEL and mp % tm == 0 and ms % tm == 0
    dense_tile = DENSE_TILE if m_rows % DENSE_TILE == 0 else tm
    t_block = min(SCAN_T_BLOCK, seq_len)

    x = jnp.concatenate([x_prompt.reshape(mp, d), x_sample.reshape(ms, d)], axis=0)
    wkv_p, wkv_s, shift_p, shift_s = [], [], [], []
    v_first = None
    kv = None

    def moe(layer, x1, gate_l, idx_l):
        return moe_layer(x1, gate_l, idx_l, moe_win, moe_bin[layer], moe_wout, moe_bout[layer],
                         jnp.stack([ln_g[layer, 1], ln_b[layer, 1]]), tm=tm, layer=layer)

    def post(layer, y, x_in, wo):
        wr = moe_wr[layer]
        wr_hi = wr.astype(BF16)
        wr_lo = (wr - wr_hi.astype(F32)).astype(BF16)
        padc = lambda w: jnp.pad(w, ((0, 0), (0, LANES - N_EXPERTS)))
        br = jnp.pad(moe_br[layer], (0, LANES - N_EXPERTS)).reshape(1, LANES)
        return post_mix(y, x_in, wo.astype(BF16), jnp.stack([ln_g[layer, 0], ln_b[layer, 0]]),
                        padc(wr_hi), padc(wr_lo), br, tm=tm)

    for layer in range(DEPTH):
        if layer < N_A_LAYERS:
            i = layer
            xp3 = x[:mp].reshape(bp, seq_len, d)
            xs3 = x[mp:].reshape(bs, dec_len, d)
            shift_p.append(xp3[:, -1])
            shift_s.append(xs3[:, -1])
            prev_p = jnp.concatenate([jnp.zeros((bp, 1, d), F32), xp3[:, :-1]], axis=1)
            prev_s = jnp.concatenate([state_shift[i][:, None, :], xs3[:, :-1]], axis=1)
            x_prev = jnp.concatenate([prev_p.reshape(mp, d), prev_s.reshape(ms, d)], axis=0)
            vec = jnp.stack([rw_w0[i], rw_a0[i], rw_v0[i - 1] if i > 0 else jnp.zeros((d,), F32)])
            mats = [rw_wr[i].astype(BF16), rw_wk[i].astype(BF16), rw_wv[i].astype(BF16),
                    _pad_cols(rw_w1[i]), _pad_rows(rw_w2[i]), _pad_cols(rw_a1[i]), _pad_rows(rw_a2[i]),
                    _pad_cols(rw_g1[i]), _pad_rows(rw_g2[i])]
            if i > 0:
                mats += [_pad_cols(rw_v1[i - 1]), _pad_rows(rw_v2[i - 1])]
            r, k, v, a, ld, g = a_proj(x, x_prev, v_first, rw_mu[i], vec, mats, tm=tm)
            if i == 0:
                v_first = v
            prm = jnp.stack([rw_kk[i], rw_ka[i], rw_rk[i].reshape(d), rw_gn_g[i], rw_gn_b[i]])
            seqs = (r, k, v, a, ld, g)
            y, sp = wkv_scan(seqs, prm, jnp.zeros((bp, PAIRS, LANES, LANES), F32), row0=0, t_len=seq_len,
                             chunk=SCAN_CHUNK, t_block=t_block)
            y, ss = wkv_scan(seqs, prm, pair_states(state_wkv[i]), row0=mp, t_len=dec_len,
                             chunk=dec_len, t_block=dec_len, out=y)
            wkv_p.append(unpair_states(sp))
            wkv_s.append(unpair_states(ss))
            x1, gate_l, idx_l = post(layer, y, x, rw_wo[i])
        else:
            j = layer - N_A_LAYERS
            q = dense(x, att_wq[j].astype(BF16), tm=dense_tile, tn=D_MODEL)
            parts = []
            for gi, (window, dil) in enumerate(GROUPS):
                assert window // dil == ATT_STEPS
                parts += list(attn_prompt_group(q, kv, group=gi, bsz=bp, seq_len=seq_len, dil=dil))
            y_s = attn_sample(q, kv, cache_k.reshape(bs, kv_buf, d), cache_v.reshape(bs, kv_buf, d),
                              row0=mp, bsz=bs, t_len=dec_len)
            x1, gate_l, idx_l = post(layer, (tuple(parts), y_s), x, att_wo[j])
        x = moe(layer, x1, gate_l, idx_l)
        if layer == N_A_LAYERS - 1:
            kv = dense(x, kv_w.astype(BF16), tm=dense_tile, tn=D_MODEL)

    buf_p = min(GROUPS[-1][0], seq_len)
    heads = lambda t, n: t.reshape(t.shape[0], n, N_HEADS, HEAD)
    tails = [kv[(b + 1) * seq_len - buf_p:(b + 1) * seq_len] for b in range(bp)]
    kv_s = kv[mp:].reshape(bs, dec_len, 2 * d)
    k_p_out = heads(jnp.stack([t[:, :d] for t in tails]), buf_p)
    v_p_out = heads(jnp.stack([t[:, d:] for t in tails]), buf_p)
    k_s_out = jnp.concatenate([cache_k, heads(kv_s[:, :, :d], dec_len)], axis=1)[:, -kv_buf:]
    v_s_out = jnp.concatenate([cache_v, heads(kv_s[:, :, d:], dec_len)], axis=1)[:, -kv_buf:]
    return (x[:mp].reshape(bp, seq_len, d), x[mp:].reshape(bs, dec_len, d),
            jnp.stack(wkv_p), jnp.stack(shift_p), k_p_out, v_p_out,
            jnp.stack(wkv_s), jnp.stack(shift_s), k_s_out, v_s_out)
```

```python
import functools

import jax
import jax.numpy as jnp
from jax import lax
from jax.experimental import pallas as pl
from jax.experimental.pallas import tpu as pltpu

F32 = jnp.float32
BF16 = jnp.bfloat16

D_MODEL = 1024
HEAD = 64
N_HEADS = D_MODEL // HEAD
LANES = 128
PAIRS = D_MODEL // LANES
DEPTH = 4
N_A_LAYERS = DEPTH // 2
LORA_PAD = 128
GN_EPS = 64e-5
LN_EPS = 1e-5
DN_ALPHA = (2 * DEPTH) ** 0.25
GROUPS = ((128, 1), (512, 4), (2048, 16))
ATT_STEPS = 128
N_EXPERTS = 32
TOP_K = 4
SWIGLU_LIMIT = 7.0
SWIGLU_ALPHA = 1.702
MOE_ROWS = 256
TOKEN_TILE = 256
DENSE_TILE = 1280
SCAN_CHUNK = 64
INV_BASE = 8
CAST_SLAB = 128
SCAN_T_BLOCK = 256
NEG_BIG = -1e30
VMEM_LIMIT = 56 * 1024 * 1024


def _cparams(*sem):
    return pltpu.CompilerParams(dimension_semantics=sem, vmem_limit_bytes=VMEM_LIMIT)


def _mm(a, b):
    return jnp.dot(a.astype(BF16), b.astype(BF16), preferred_element_type=F32)


def _mm_nt(a, b):
    return lax.dot_general(a.astype(BF16), b.astype(BF16), (((1,), (1,)), ((), ())),
                           preferred_element_type=F32)


def _mm_tn(a, b):
    return lax.dot_general(a.astype(BF16), b.astype(BF16), (((0,), (0,)), ((), ())),
                           preferred_element_type=F32)


def _split(x, parts):
    out = []
    for _ in range(parts):
        h = x.astype(BF16)
        out.append(h)
        x = x - h.astype(F32)
    return out


def _mm_sel_r(x, sel, parts=2):
    acc = None
    for h in _split(x, parts):
        t = jnp.dot(h, sel, preferred_element_type=F32)
        acc = t if acc is None else acc + t
    return acc


def _mm_sel_l(sel, x, parts=3):
    acc = None
    for h in _split(x, parts):
        t = jnp.dot(sel, h, preferred_element_type=F32)
        acc = t if acc is None else acc + t
    return acc


def _sigmoid(x):
    return 1.0 / (1.0 + jnp.exp(-x))


def _layer_norm(x, g, b):
    mu = jnp.mean(x, axis=-1, keepdims=True)
    xc = x - mu
    var = jnp.mean(xc * xc, axis=-1, keepdims=True)
    return xc * lax.rsqrt(var + LN_EPS) * g + b


def _each(fn, *lists):
    return [fn(*xs) for xs in zip(*lists)]


def _unit_lower_inverse(a_side, cst, chunk):
    eye, row, scol, bd = cst
    base = INV_BASE
    blk = (row // base) == (scol // base)

    def mul(xs, ys):
        return _each(_mm, xs, [bd(y) for y in ys])

    x = [jnp.where(blk, -a, 0.0) for a in a_side]
    x2 = mul(x, x)
    x4 = mul(x2, x2)
    xx2 = mul(x, x2)
    y = _each(lambda x_, x2_, xx2_: eye + x_ + x2_ + xx2_, x, x2, xx2)
    t = _each(jnp.add, y, mul(y, x4))
    s = base
    while s < chunk:
        rb = row // s
        off = (rb == (scol // s) + 1) & ((rb % 2) == 1)
        a_off = [jnp.where(off, a, 0.0) for a in a_side]
        t = _each(jnp.subtract, t, mul(t, mul(a_off, t)))
        s *= 2
    return t


def _wkv_chunk(s_mat, r, kr, v, a, ld, g, prm, cst, chunk):
    kk_p, ka_p, rk_p, gng, gnb = prm
    m_a, e_seg, e_seg2, tri, eye, row, scol, strict, incl, bd = cst
    c = chunk
    kkr = _each(jnp.multiply, kr, kk_p)
    ss = [_mm_sel_r(x * x, e_seg) for x in kkr]
    kk = _each(lambda x, s_: x / jnp.maximum(jnp.sqrt(s_), 1e-12), kkr, ss)
    k = _each(lambda kr_, a_, ka_: kr_ * (1.0 + (a_ - 1.0) * ka_), kr, a, ka_p)
    b = _each(jnp.multiply, kk, a)
    cl = [_mm_sel_l(tri, x) for x in ld]
    cl_end = [x[c - 1:c, :] for x in cl]

    def stack(x):
        return jnp.concatenate([jnp.where(m_a, x, 0.0), jnp.where(m_a, 0.0, x)], axis=0)

    kkg = _each(lambda kk_, cl_, ld_: kk_ * jnp.exp(cl_ - ld_), kk, cl, ld)
    rg = _each(lambda r_, cl_: r_ * jnp.exp(cl_), r, cl)
    g_inv = [jnp.exp(-x) for x in cl]
    bd_s = _each(lambda b_, gi: stack(b_ * gi), b, g_inv)
    kd_s = _each(lambda k_, gi: stack(k_ * gi), k, g_inv)
    g_end = _each(lambda ce, cl_: jnp.exp(ce - cl_), cl_end, cl)
    be_s = _each(lambda b_, ge: stack(b_ * ge), b, g_end)
    ke_s = _each(lambda k_, ge: stack(k_ * ge), k, g_end)
    v_s = [stack(x) for x in v]

    if (2 * c) % LANES == 0:
        bk_s = _each(lambda x_, y_: jnp.concatenate([x_, y_], axis=0), bd_s, kd_s)
        ab = _each(_mm_nt, kkg, bk_s)
        rbk = _each(_mm_nt, rg, bk_s)
        a_side, b_side = [x[:, :2 * c] for x in ab], [x[:, 2 * c:] for x in ab]
        rb_side, rk_side = [x[:, :2 * c] for x in rbk], [x[:, 2 * c:] for x in rbk]
    else:
        a_side, b_side = _each(_mm_nt, kkg, bd_s), _each(_mm_nt, kkg, kd_s)
        rb_side, rk_side = _each(_mm_nt, rg, bd_s), _each(_mm_nt, rg, kd_s)
    a_side = [jnp.where(strict, x, 0.0) for x in a_side]
    b_side = [jnp.where(strict, x, 0.0) for x in b_side]
    rb_side = [jnp.where(incl, x, 0.0) for x in rb_side]
    rk_side = [jnp.where(incl, x, 0.0) for x in rk_side]
    t_side = _unit_lower_inverse(a_side, (eye, row, scol, bd), c)

    bv = _each(_mm, b_side, v_s)
    gu = _each(lambda t_, kkg_, bv_: _mm(t_, jnp.concatenate([stack(kkg_), stack(bv_)], axis=1)), t_side, kkg, bv)
    g_s = [stack(x[:, :LANES]) for x in gu]
    u1_s = [stack(x[:, LANES:]) for x in gu]
    pq = _each(lambda rb_, gs_, u1_: _mm(rb_, jnp.concatenate([gs_, u1_], axis=1)), rb_side, g_s, u1_s)
    rkv = _each(_mm, rk_side, v_s)
    p = _each(lambda rg_, pq_: rg_ - pq_[:, :LANES], rg, pq)
    q = _each(lambda rkv_, pq_: rkv_ - pq_[:, LANES:], rkv, pq)
    o = _each(lambda p_, s_, q_: _mm_nt(p_, s_) + q_, p, s_mat, q)
    gb = _each(_mm_tn, g_s, be_s)
    nt = _each(lambda vs_, u1_, ke_, be_: _mm_tn(jnp.concatenate([vs_, -u1_], axis=0),
                                                   jnp.concatenate([ke_, be_], axis=0)), v_s, u1_s, ke_s, be_s)
    s_new = _each(lambda s_, ce, gb_, nt_: s_ * jnp.exp(ce) - _mm(s_, gb_) + nt_, s_mat, cl_end, gb, nt)

    mb = _each(lambda o_, r_, k_, rk_: _mm_sel_r(jnp.concatenate([o_, r_ * k_ * rk_], axis=1), e_seg2),
               o, r, k, rk_p)
    d = _each(lambda o_, mb_: o_ - mb_[:, :LANES] * (1.0 / HEAD), o, mb)
    var = [_mm_sel_r(x * x, e_seg) * (1.0 / HEAD) for x in d]
    out = _each(lambda d_, var_, gg, gb_, mb_, v_, g_:
                (d_ * lax.rsqrt(var_ + GN_EPS) * gg + gb_ + mb_[:, LANES:] * v_) * g_,
                d, var, gng, gnb, mb, v, g)
    return s_new, out


def _wkv_kernel(r_ref, k_ref, v_ref, a_ref, ld_ref, g_ref, prm_ref, s0_ref, *rest, chunk, n_chunks, n_pairs):
    o_ref, s_out_ref, s_scr = rest[-3:]
    tb = pl.program_id(2)

    @pl.when(tb == 0)
    def _():
        s_scr[...] = s0_ref[0]

    c2 = 2 * chunk
    lane = lax.broadcasted_iota(jnp.int32, (1, LANES), 1)
    m_a = lane < HEAD
    er = lax.broadcasted_iota(jnp.int32, (2 * LANES, 2 * LANES), 0)
    ec = lax.broadcasted_iota(jnp.int32, (2 * LANES, 2 * LANES), 1)
    e_seg2 = ((er // HEAD) == (ec // HEAD)).astype(BF16)
    e_seg = e_seg2[:LANES, :LANES]
    tr = lax.broadcasted_iota(jnp.int32, (chunk, chunk), 0)
    tc = lax.broadcasted_iota(jnp.int32, (chunk, chunk), 1)
    tri = (tr >= tc).astype(BF16)
    row = lax.broadcasted_iota(jnp.int32, (chunk, c2), 0)
    col = lax.broadcasted_iota(jnp.int32, (chunk, c2), 1)
    scol = col % chunk
    strict = scol < row
    incl = scol <= row
    eye = (row == scol).astype(F32)
    left = col < chunk

    def bd(x):
        return jnp.concatenate([jnp.where(left, x, 0.0), jnp.where(left, 0.0, x)], axis=0)

    cst = (m_a, e_seg, e_seg2, tri, eye, row, scol, strict, incl, bd)

    def body(ci, carry):
        sl = pl.ds(pl.multiple_of(ci * chunk, chunk), chunk)
        lanes = [slice(p * LANES, (p + 1) * LANES) for p in range(n_pairs)]
        prm = tuple([prm_ref[i:i + 1, ln] for ln in lanes] for i in range(5))
        seqs = [[ref[sl, ln] for ln in lanes] for ref in (r_ref, k_ref, v_ref, a_ref, ld_ref, g_ref)]
        s_new, out = _wkv_chunk([s_scr[p] for p in range(n_pairs)], *seqs, prm, cst, chunk)
        for p in range(n_pairs):
            s_scr[p] = s_new[p]
            o_ref[sl, lanes[p]] = out[p]
        return carry

    lax.fori_loop(0, n_chunks, body, 0, unroll=min(2, n_chunks))

    @pl.when(tb == pl.num_programs(2) - 1)
    def _():
        s_out_ref[0] = s_scr[...]


def wkv_scan(seqs, prm, s0, *, row0, t_len, chunk, t_block, n_pairs=PAIRS, out=None):
    m_rows, d = seqs[0].shape
    bsz = s0.shape[0]
    assert d == D_MODEL and t_len % t_block == 0 and t_block % chunk == 0 and row0 % t_block == 0
    assert PAIRS % n_pairs == 0 and row0 + bsz * t_len <= m_rows
    width = n_pairs * LANES
    nt = t_len // t_block
    blk0 = row0 // t_block
    seq = pl.BlockSpec((t_block, width), lambda b, p, t: (blk0 + b * nt + t, p))
    st = pl.BlockSpec((1, n_pairs, LANES, LANES), lambda b, p, t: (b, p, 0, 0))
    in_specs = [seq] * 6 + [pl.BlockSpec((5, width), lambda b, p, t: (0, p)), st]
    args = list(seqs) + [prm, s0]
    aliases = {}
    if out is not None:
        in_specs.append(pl.BlockSpec(memory_space=pl.ANY))
        args.append(out)
        aliases = {len(args) - 1: 0}
    return pl.pallas_call(
        functools.partial(_wkv_kernel, chunk=chunk, n_chunks=t_block // chunk, n_pairs=n_pairs),
        grid=(bsz, PAIRS // n_pairs, nt),
        in_specs=in_specs,
        out_specs=[seq, st],
        out_shape=[jax.ShapeDtypeStruct((m_rows, d), F32),
                   jax.ShapeDtypeStruct((bsz, PAIRS, LANES, LANES), F32)],
        scratch_shapes=[pltpu.VMEM((n_pairs, LANES, LANES), F32)],
        input_output_aliases=aliases,
        compiler_params=_cparams("parallel", "parallel", "arbitrary"),
        name="wkv_scan",
    )(*args)


def pair_states(s):
    bsz = s.shape[0]
    s = s.reshape(bsz, PAIRS, 2, HEAD, HEAD)
    z = jnp.zeros_like(s[:, :, 0])
    top = jnp.concatenate([s[:, :, 0], z], axis=-1)
    bot = jnp.concatenate([z, s[:, :, 1]], axis=-1)
    return jnp.concatenate([top, bot], axis=-2)


def unpair_states(sp):
    bsz = sp.shape[0]
    s = jnp.stack([sp[:, :, :HEAD, :HEAD], sp[:, :, HEAD:, HEAD:]], axis=2)
    return s.reshape(bsz, N_HEADS, HEAD, HEAD)


def _a_proj_kernel(*refs, has_vres):
    if has_vres:
        (x_ref, xp_ref, vf_ref, mu_ref, vec_ref, wr, wk, wv, w1, w2, a1, a2, g1, g2, v1, v2,
         r_o, k_o, v_o, a_o, ld_o, g_o) = refs
    else:
        (x_ref, xp_ref, mu_ref, vec_ref, wr, wk, wv, w1, w2, a1, a2, g1, g2,
         r_o, k_o, v_o, a_o, ld_o, g_o) = refs
    x = x_ref[...]
    xx = xp_ref[...] - x
    xr, xw, xk, xv, xa, xg = [(x + xx * mu_ref[i:i + 1, :]).astype(BF16) for i in range(6)]
    r_o[...] = _mm(xr, wr[...])
    k_o[...] = _mm(xk, wk[...])
    v = _mm(xv, wv[...])
    z = vec_ref[0:1, :] + _mm(jnp.tanh(_mm(xw, w1[...])), w2[...])
    softplus_neg = jnp.maximum(-z, 0.0) + jnp.log(1.0 + jnp.exp(-jnp.abs(z)))
    ld_o[...] = -jnp.exp(-softplus_neg - 0.5)
    if has_vres:
        mix = _sigmoid(vec_ref[2:3, :] + _mm(_mm(xv, v1[...]), v2[...]))
        v = v + (vf_ref[...] - v) * mix
    v_o[...] = v
    a_o[...] = _sigmoid(vec_ref[1:2, :] + _mm(_mm(xa, a1[...]), a2[...]))
    g_o[...] = _mm(_sigmoid(_mm(xg, g1[...])), g2[...])


def a_proj(x, x_prev, v_first, mu, vec, mats, *, tm):
    m_rows = x.shape[0]
    assert m_rows % tm == 0
    tok = pl.BlockSpec((tm, D_MODEL), lambda i: (i, 0))
    full = lambda a: pl.BlockSpec(a.shape, lambda i: (0, 0))
    has_vres = v_first is not None
    acts = [x, x_prev] + ([v_first] if has_vres else [])
    consts = [mu, vec] + list(mats)
    return pl.pallas_call(
        functools.partial(_a_proj_kernel, has_vres=has_vres),
        grid=(m_rows // tm,),
        in_specs=[tok] * len(acts) + [full(c) for c in consts],
        out_specs=[tok] * 6,
        out_shape=[jax.ShapeDtypeStruct((m_rows, D_MODEL), F32)] * 6,
        compiler_params=_cparams("parallel"),
        name="a_proj",
    )(*acts, *consts)


def _post_kernel(*refs, n_prompt_tiles):
    if n_prompt_tiles is None:
        y_ref = refs[0]
        y_in = y_ref[...]
        rest = refs[1:]
    else:
        o0, l0, o1, l1, o2, l2, ys_ref = refs[:7]
        rest = refs[7:]
        m = jnp.maximum(jnp.maximum(l0[...], l1[...]), l2[...])
        w0 = jnp.exp(l0[...] - m)
        w1 = jnp.exp(l1[...] - m)
        w2 = jnp.exp(l2[...] - m)
        merged = (w0 * o0[...] + w1 * o1[...] + w2 * o2[...]) / (w0 + w1 + w2)
        y_in = jnp.where(pl.program_id(0) < n_prompt_tiles, merged, ys_ref[...])
    x_ref, wo_ref, ln_ref, wrh_ref, wrl_ref, br_ref, x1_ref, gate_ref, idx_ref = rest
    y = _mm(y_in, wo_ref[...])
    x1 = _layer_norm(DN_ALPHA * x_ref[...] + y, ln_ref[0:1, :], ln_ref[1:2, :])
    x1_ref[...] = x1
    parts = _split(x1, 3)
    acc = br_ref[...]
    for h in parts:
        acc = acc + jnp.dot(h, wrh_ref[...], preferred_element_type=F32)
    for h in parts[:2]:
        acc = acc + jnp.dot(h, wrl_ref[...], preferred_element_type=F32)
    lane = lax.broadcasted_iota(jnp.int32, acc.shape, 1)
    lane_f = lane.astype(F32)
    lg = jnp.where(lane < N_EXPERTS, acc, -jnp.inf)
    vals = []
    idx_out = jnp.zeros(acc.shape, F32)
    for k in range(TOP_K):
        v = jnp.max(lg, axis=-1, keepdims=True)
        idx = jnp.min(jnp.where(lg == v, lane_f, float(LANES)), axis=-1, keepdims=True)
        vals.append(v)
        idx_out = jnp.where(lane == k, idx, idx_out)
        lg = jnp.where(lane_f == idx, -jnp.inf, lg)
    es = [jnp.exp(v - vals[0]) for v in vals]
    den = es[0]
    for e in es[1:]:
        den = den + e
    gate = jnp.zeros(acc.shape, F32)
    for k in range(TOP_K):
        gate = jnp.where(lane == k, es[k] / den, gate)
    gate_ref[...] = gate
    idx_ref[...] = idx_out.astype(jnp.int32)


def post_mix(y, x, wo, ln, wr_hi, wr_lo, br, *, tm):
    m_rows = x.shape[0]
    tok = pl.BlockSpec((tm, D_MODEL), lambda i: (i, 0))
    full = lambda a: pl.BlockSpec(a.shape, lambda i: (0, 0))
    if isinstance(y, tuple):
        parts, y_sample = y
        n_p = parts[0].shape[0] // tm
        assert parts[0].shape[0] % tm == 0 and y_sample.shape[0] % tm == 0
        prompt = pl.BlockSpec((tm, D_MODEL), lambda i: (jnp.minimum(i, n_p - 1), 0))
        sample = pl.BlockSpec((tm, D_MODEL), lambda i: (jnp.maximum(i - n_p, 0), 0))
        y_args, y_specs = list(parts) + [y_sample], [prompt] * 6 + [sample]
    else:
        n_p = None
        y_args, y_specs = [y], [tok]
    return pl.pallas_call(
        functools.partial(_post_kernel, n_prompt_tiles=n_p),
        grid=(m_rows // tm,),
        in_specs=y_specs + [tok] + [full(c) for c in (wo, ln, wr_hi, wr_lo, br)],
        out_specs=[tok, pl.BlockSpec((tm, LANES), lambda i: (i, 0)), pl.BlockSpec((tm, LANES), lambda i: (i, 0))],
        out_shape=[jax.ShapeDtypeStruct((m_rows, D_MODEL), F32),
                   jax.ShapeDtypeStruct((m_rows, LANES), F32),
                   jax.ShapeDtypeStruct((m_rows, LANES), jnp.int32)],
        compiler_params=_cparams("parallel"),
        name="post_mix",
    )(*y_args, x, wo, ln, wr_hi, wr_lo, br)


def _moe_kernel(be_ref, first_ref, nact_ref, xb_ref, win_ref, bin_ref, wout_ref, bout_ref, y_ref, win_s, wout_s):
    i = pl.program_id(0)

    @pl.when(first_ref[i] == 1)
    def _():
        for j in range(D_MODEL // CAST_SLAB):
            rows = slice(j * CAST_SLAB, (j + 1) * CAST_SLAB)
            win_s[rows, :] = win_ref[0, rows, :].astype(BF16)
            wout_s[rows, :] = wout_ref[0, rows, :].astype(BF16)

    @pl.when(i < nact_ref[0])
    def _():
        h = jnp.dot(xb_ref[...].astype(BF16), win_s[...], preferred_element_type=F32) + bin_ref[0]
        h_gate = jnp.minimum(h[:, :D_MODEL], SWIGLU_LIMIT)
        h_up = jnp.clip(h[:, D_MODEL:], -SWIGLU_LIMIT, SWIGLU_LIMIT)
        act = (h_up + 1.0) * h_gate * _sigmoid(SWIGLU_ALPHA * h_gate)
        y_ref[...] = jnp.dot(act.astype(BF16), wout_s[...], preferred_element_type=F32) + bout_ref[0]


def moe_experts(block_e, first, n_act, xb, w_in, b_in, w_out, b_out, *, layer):
    rows = xb.shape[0]
    n_blocks = rows // MOE_ROWS
    grid_spec = pltpu.PrefetchScalarGridSpec(
        num_scalar_prefetch=3,
        grid=(n_blocks,),
        in_specs=[
            pl.BlockSpec((MOE_ROWS, D_MODEL), lambda i, be, fi, na: (i, 0)),
            pl.BlockSpec((None, 1, D_MODEL, 2 * D_MODEL), lambda i, be, fi, na: (layer, be[i], 0, 0)),
            pl.BlockSpec((1, 1, 2 * D_MODEL), lambda i, be, fi, na: (be[i], 0, 0)),
            pl.BlockSpec((None, 1, D_MODEL, D_MODEL), lambda i, be, fi, na: (layer, be[i], 0, 0)),
            pl.BlockSpec((1, 1, D_MODEL), lambda i, be, fi, na: (be[i], 0, 0)),
        ],
        out_specs=pl.BlockSpec((MOE_ROWS, D_MODEL), lambda i, be, fi, na: (i, 0)),
        scratch_shapes=[pltpu.VMEM((D_MODEL, 2 * D_MODEL), BF16), pltpu.VMEM((D_MODEL, D_MODEL), BF16)],
    )
    return pl.pallas_call(
        _moe_kernel,
        grid_spec=grid_spec,
        out_shape=jax.ShapeDtypeStruct((rows, D_MODEL), F32),
        compiler_params=_cparams("arbitrary"),
        name="moe_experts",
    )(block_e, first, n_act, xb, w_in, b_in.reshape(N_EXPERTS, 1, -1), w_out, b_out.reshape(N_EXPERTS, 1, -1))


def _combine_kernel(yg_ref, gate_ref, x_ref, ln_ref, o_ref):
    gate = gate_ref[...]
    ffn = gate[:, 0:1] * yg_ref[0]
    for k in range(1, TOP_K):
        ffn = ffn + gate[:, k:k + 1] * yg_ref[k]
    o_ref[...] = _layer_norm(DN_ALPHA * x_ref[...] + ffn, ln_ref[0:1, :], ln_ref[1:2, :])


def moe_combine(yg, gate, x, ln, *, tm):
    m_rows = x.shape[0]
    tok = pl.BlockSpec((tm, D_MODEL), lambda i: (i, 0))
    return pl.pallas_call(
        _combine_kernel,
        grid=(m_rows // tm,),
        in_specs=[pl.BlockSpec((TOP_K, tm, D_MODEL), lambda i: (0, i, 0)),
                  pl.BlockSpec((tm, TOP_K), lambda i: (i, 0)), tok,
                  pl.BlockSpec(ln.shape, lambda i: (0, 0))],
        out_specs=tok,
        out_shape=jax.ShapeDtypeStruct((m_rows, D_MODEL), F32),
        compiler_params=_cparams("parallel"),
        name="moe_combine",
    )(yg, gate, x, ln)


def _route(top_e, m_rows):
    mk = m_rows * TOP_K
    flat_e = top_e.reshape(-1).astype(jnp.int32)
    onehot = (flat_e[:, None] == jnp.arange(N_EXPERTS, dtype=jnp.int32)[None, :]).astype(jnp.int32)
    csum = jnp.cumsum(onehot, axis=0)
    counts = csum[-1]
    rank = jnp.sum((csum - onehot) * onehot, axis=1)
    padded = (counts + MOE_ROWS - 1) // MOE_ROWS * MOE_ROWS
    pad_end = jnp.cumsum(padded)
    pad_start = pad_end - padded
    start = jnp.cumsum(counts) - counts
    pos = jnp.sum(onehot * pad_start[None, :], axis=1) + rank
    order = jnp.argsort(flat_e).astype(jnp.int32)
    n_blocks = -(-mk // MOE_ROWS) + N_EXPERTS
    blk_start = jnp.arange(n_blocks, dtype=jnp.int32) * MOE_ROWS
    block_e = jnp.sum((pad_end[None, :] <= blk_start[:, None]).astype(jnp.int32), axis=1)
    block_e = jnp.minimum(block_e, N_EXPERTS - 1)
    first = ((blk_start == pad_start[block_e]) & (blk_start < pad_end[-1])).astype(jnp.int32)
    n_act = (pad_end[-1:] // MOE_ROWS).astype(jnp.int32)
    e_row = jnp.repeat(block_e, MOE_ROWS)
    j_row = jnp.arange(n_blocks * MOE_ROWS, dtype=jnp.int32) - pad_start[e_row]
    compact = jnp.clip(start[e_row] + j_row, 0, mk - 1)
    row_tok = jnp.where(j_row < counts[e_row], order[compact] // TOP_K, 0)
    return row_tok, pos, block_e, first, n_act


def moe_layer(x1, gate_l, idx_l, w_in, b_in, w_out, b_out, ln, *, tm, layer):
    m_rows = x1.shape[0]
    gate = gate_l[:, :TOP_K]
    row_tok, pos, block_e, first, n_act = _route(idx_l[:, :TOP_K], m_rows)
    xb = x1[row_tok]
    yb = moe_experts(block_e, first, n_act, xb, w_in, b_in, w_out, b_out, layer=layer)
    yg = yb[pos.reshape(m_rows, TOP_K).T]
    return moe_combine(yg, gate, x1, ln, tm=tm)


def _dense_kernel(x_ref, w_ref, o_ref):
    o_ref[...] = _mm(x_ref[...], w_ref[...]).astype(o_ref.dtype)


def dense(x, w, *, tm, tn, out_dtype=F32):
    m_rows, k_dim = x.shape
    n_dim = w.shape[1]
    assert m_rows % tm == 0 and n_dim % tn == 0
    return pl.pallas_call(
        _dense_kernel,
        grid=(n_dim // tn, m_rows // tm),
        in_specs=[pl.BlockSpec((tm, k_dim), lambda j, i: (i, 0)),
                  pl.BlockSpec((k_dim, tn), lambda j, i: (0, j))],
        out_specs=pl.BlockSpec((tm, tn), lambda j, i: (i, j)),
        out_shape=jax.ShapeDtypeStruct((m_rows, n_dim), out_dtype),
        compiler_params=_cparams("parallel", "parallel"),
        name="dense",
    )(x, w)


def _slope(head):
    return 2.0 ** (-8.0 * (head + 1) / N_HEADS)


def _attn_prompt_kernel(slope_ref, q_ref, kp_ref, kc_ref, vp_ref, vc_ref, o_ref, lse_ref, bias_scr, *, dil, n_pairs):
    lb = pl.program_id(1)
    n = pl.program_id(2)
    nk = ATT_STEPS
    lane = lax.broadcasted_iota(jnp.int32, (1, LANES), 1)
    m_a = lane < HEAD

    @pl.when(n == 0)
    def _():
        qi = lax.broadcasted_iota(jnp.int32, (nk, 2 * nk), 0)
        kj = lax.broadcasted_iota(jnp.int32, (nk, 2 * nk), 1)
        delta = qi + nk - kj
        valid = (delta >= 0) & (delta <= nk)
        dist = (delta * dil).astype(F32)
        for h in range(2 * n_pairs):
            bias = jnp.where(valid, -slope_ref[lb * 2 * n_pairs + h] * dist, NEG_BIG)
            bias_scr[1, h] = bias
            bias_scr[0, h] = jnp.where(kj >= nk, bias, NEG_BIG)

    table = jnp.minimum(n, 1)

    def scores(item):
        rows, p = item
        ln = slice(p * LANES, (p + 1) * LANES)
        q = q_ref[rows, ln] * (HEAD ** -0.5)
        k = jnp.concatenate([kp_ref[rows, ln], kc_ref[rows, ln]], axis=0).astype(BF16)
        return [_mm_nt(jnp.where(m_a, q, 0.0), k), _mm_nt(jnp.where(m_a, 0.0, q), k)]

    def run(items):
        groups = [items[i:i + 2] for i in range(0, len(items), 2)]
        s_next = [scores(it) for it in groups[0]]
        for gi, grp in enumerate(groups):
            s_cur = s_next
            if gi + 1 < len(groups):
                s_next = [scores(it) for it in groups[gi + 1]]
            lns = [slice(p * LANES, (p + 1) * LANES) for _, p in grp]
            vs = [jnp.concatenate([vp_ref[rows, ln], vc_ref[rows, ln]], axis=0).astype(BF16)
                  for (rows, _), ln in zip(grp, lns)]
            s = [s_cur[i][hh] + bias_scr[table, 2 * p + hh] for i, (_, p) in enumerate(grp) for hh in range(2)]
            m = [jnp.max(x, axis=-1, keepdims=True) for x in s]
            e = _each(lambda x, m_: jnp.exp(x - m_), s, m)
            l = [jnp.sum(x, axis=-1, keepdims=True) for x in e]
            pv = [jnp.dot(x.astype(BF16), vs[j // 2], preferred_element_type=F32) for j, x in enumerate(e)]
            outs = _each(jnp.divide, pv, l)
            lses = _each(lambda m_, l_: m_ + jnp.log(l_), m, l)
            for i, ((rows, _), ln) in enumerate(zip(grp, lns)):
                o_ref[rows, ln] = jnp.where(m_a, outs[2 * i], outs[2 * i + 1])
                lse_ref[rows, ln] = jnp.where(m_a, lses[2 * i], lses[2 * i + 1])

    if dil == 1:
        run([(slice(None), p) for p in range(n_pairs)])
    else:
        group = min(dil, ATT_CLASS_UNROLL)

        def body(gi, carry):
            run([(pl.ds(gi * group + u, nk, stride=dil), 0) for u in range(group)])
            return carry

        lax.fori_loop(0, dil // group, body, 0)


ATT_CLASS_UNROLL = 8


def attn_prompt_group(q, kv, *, group, bsz, seq_len, dil):
    tile = ATT_STEPS * dil
    assert seq_len % tile == 0
    n_tiles = seq_len // tile
    n_pairs = PAIRS if dil == 1 else 1
    width = n_pairs * LANES
    n_lb = D_MODEL // width
    slopes = jnp.asarray([_slope(h) for h in range(N_HEADS)], F32)

    def spec(col0, back):
        return pl.BlockSpec((tile, width),
                            lambda b, lb, n, sl: (b * n_tiles + jnp.maximum(n - back, 0), col0 * n_lb + lb))

    out = pl.BlockSpec((tile, width), lambda b, lb, n, sl: (b * n_tiles + n, lb))
    grid_spec = pltpu.PrefetchScalarGridSpec(
        num_scalar_prefetch=1,
        grid=(bsz, n_lb, n_tiles),
        in_specs=[spec(group, 0), spec(0, 1), spec(0, 0), spec(1, 1), spec(1, 0)],
        out_specs=[out, out],
        scratch_shapes=[pltpu.VMEM((2, 2 * n_pairs, ATT_STEPS, 2 * ATT_STEPS), F32)],
    )
    return pl.pallas_call(
        functools.partial(_attn_prompt_kernel, dil=dil, n_pairs=n_pairs),
        grid_spec=grid_spec,
        out_shape=[jax.ShapeDtypeStruct((bsz * seq_len, D_MODEL), F32)] * 2,
        compiler_params=_cparams("parallel", "parallel", "arbitrary"),
        name="attn_prompt",
    )(slopes, q, kv, kv, kv, kv)


SAMPLE_PAIRS = 4


def _attn_sample_kernel(q0_ref, q1_ref, q2_ref, kc_ref, kn_ref, vc_ref, vn_ref, o_ref, *, t_len, kv_buf, n_pairs):
    pb = pl.program_id(1)
    lane = lax.broadcasted_iota(jnp.int32, (1, LANES), 1)
    m_a = lane < HEAD
    n_rows = 6 * t_len
    ri = lax.broadcasted_iota(jnp.int32, (n_rows, 1), 0)
    t = ri % t_len
    grp = ri // (2 * t_len)
    hh = (ri // t_len) % 2
    dmask = jnp.where(grp == 0, GROUPS[0][1] - 1, jnp.where(grp == 1, GROUPS[1][1] - 1, GROUPS[2][1] - 1))
    win = jnp.where(grp == 0, GROUPS[0][0], jnp.where(grp == 1, GROUPS[1][0], GROUPS[2][0]))
    jc = lax.broadcasted_iota(jnp.int32, (n_rows, kv_buf), 1)
    jn = lax.broadcasted_iota(jnp.int32, (n_rows, LANES), 1)

    def band(dist):
        return (dist >= 0) & (dist <= win) & ((dist & dmask) == 0), dist.astype(F32)

    ok_c, dist_c = band(kv_buf + t - jc)
    ok_n, dist_n = band(t - jn)
    pad = jnp.zeros((LANES - t_len, LANES), F32)
    blk = 2 * t_len

    for p in range(n_pairs):
        ln = slice(p * LANES, (p + 1) * LANES)
        lhs = []
        for q_ref in (q0_ref, q1_ref, q2_ref):
            q = q_ref[:, ln] * (HEAD ** -0.5)
            lhs += [jnp.where(m_a, q, 0.0), jnp.where(m_a, 0.0, q)]
        lhs = jnp.concatenate(lhs, axis=0)
        kn = jnp.concatenate([kn_ref[:, ln], pad], axis=0)
        vn = jnp.concatenate([vn_ref[:, ln], pad], axis=0)
        head = (2 * (pb * n_pairs + p) + hh).astype(F32)
        slope = jnp.exp2(-8.0 * (head + 1.0) / N_HEADS)
        s_c = jnp.where(ok_c, _mm_nt(lhs, kc_ref[:, ln]) - slope * dist_c, NEG_BIG)
        s_n = jnp.where(ok_n, _mm_nt(lhs, kn) - slope * dist_n, NEG_BIG)
        m = jnp.maximum(jnp.max(s_c, axis=-1, keepdims=True), jnp.max(s_n, axis=-1, keepdims=True))
        e_c = jnp.exp(s_c - m)
        e_n = jnp.exp(s_n - m)
        l = jnp.sum(e_c, axis=-1, keepdims=True) + jnp.sum(e_n, axis=-1, keepdims=True)
        acc = _mm(e_c, vc_ref[:, ln]) + _mm(e_n, vn)
        m_g = [m[g * blk:(g + 1) * blk] for g in range(3)]
        m_all = jnp.maximum(jnp.maximum(m_g[0], m_g[1]), m_g[2])
        num = 0.0
        den = 0.0
        for g in range(3):
            w = jnp.exp(m_g[g] - m_all)
            num = num + w * acc[g * blk:(g + 1) * blk]
            den = den + w * l[g * blk:(g + 1) * blk]
        res = num / den
        o_ref[:, ln] = jnp.where(m_a, res[:t_len], res[t_len:])


def attn_sample(q, kv, cache_k, cache_v, *, row0, bsz, t_len):
    kv_buf = cache_k.shape[1]
    assert kv_buf >= GROUPS[-1][0] and kv_buf % LANES == 0 and row0 % t_len == 0 and t_len % 8 == 0
    blk0 = row0 // t_len
    width = SAMPLE_PAIRS * LANES
    n_lb = D_MODEL // width
    qs = [pl.BlockSpec((t_len, width), lambda b, p, g=g: (blk0 + b, g * n_lb + p)) for g in range(3)]
    cache = pl.BlockSpec((None, kv_buf, width), lambda b, p: (b, 0, p))
    k_new = pl.BlockSpec((t_len, width), lambda b, p: (blk0 + b, p))
    v_new = pl.BlockSpec((t_len, width), lambda b, p: (blk0 + b, n_lb + p))
    return pl.pallas_call(
        functools.partial(_attn_sample_kernel, t_len=t_len, kv_buf=kv_buf, n_pairs=SAMPLE_PAIRS),
        grid=(bsz, n_lb),
        in_specs=qs + [cache, k_new, cache, v_new],
        out_specs=pl.BlockSpec((t_len, width), lambda b, p: (b, p)),
        out_shape=jax.ShapeDtypeStruct((bsz * t_len, D_MODEL), F32),
        compiler_params=_cparams("parallel", "parallel"),
        name="attn_sample",
    )(q, q, q, cache_k, kv, cache_v, kv)


def _pad_cols(w):
    return jnp.pad(w, ((0, 0), (0, LORA_PAD - w.shape[1]))).astype(BF16)


def _pad_rows(w):
    return jnp.pad(w, ((0, LORA_PAD - w.shape[0]), (0, 0))).astype(BF16)


def kernel(x_prompt, x_sample, state_wkv, state_shift, cache_k, cache_v, ln_g, ln_b, rw_mu, rw_wr, rw_wk, rw_wv,
           rw_wo, rw_w0, rw_w1, rw_w2, rw_a0, rw_a1, rw_a2, rw_v0, rw_v1, rw_v2, rw_g1, rw_g2, rw_kk, rw_ka,
           rw_rk, rw_gn_g, rw_gn_b, kv_w, att_wq, att_wo, moe_wr, moe_br, moe_win, moe_bin, moe_wout, moe_bout):
    bp, seq_len, d = x_prompt.shape
    bs, dec_len, _ = x_sample.shape
    kv_buf = cache_k.shape[1]
    mp = bp * seq_len
    ms = bs * dec_len
    m_rows = mp + ms
    tm = TOKEN_TILE
    assert d == D_MODEL and mp % tm == 0 and ms % tm == 0
    dense_tile = DENSE_TILE if m_rows % DENSE_TILE == 0 else tm
    t_block = min(SCAN_T_BLOCK, seq_len)

    x = jnp.concatenate([x_prompt.reshape(mp, d), x_sample.reshape(ms, d)], axis=0)
    wkv_p, wkv_s, shift_p, shift_s = [], [], [], []
    v_first = None
    kv = None

    def moe(layer, x1, gate_l, idx_l):
        return moe_layer(x1, gate_l, idx_l, moe_win, moe_bin[layer], moe_wout, moe_bout[layer],
                         jnp.stack([ln_g[layer, 1], ln_b[layer, 1]]), tm=tm, layer=layer)

    def post(layer, y, x_in, wo):
        wr = moe_wr[layer]
        wr_hi = wr.astype(BF16)
        wr_lo = (wr - wr_hi.astype(F32)).astype(BF16)
        padc = lambda w: jnp.pad(w, ((0, 0), (0, LANES - N_EXPERTS)))
        br = jnp.pad(moe_br[layer], (0, LANES - N_EXPERTS)).reshape(1, LANES)
        return post_mix(y, x_in, wo.astype(BF16), jnp.stack([ln_g[layer, 0], ln_b[layer, 0]]),
                        padc(wr_hi), padc(wr_lo), br, tm=tm)

    for layer in range(DEPTH):
        if layer < N_A_LAYERS:
            i = layer
            xp3 = x[:mp].reshape(bp, seq_len, d)
            xs3 = x[mp:].reshape(bs, dec_len, d)
            shift_p.append(xp3[:, -1])
            shift_s.append(xs3[:, -1])
            prev_p = jnp.concatenate([jnp.zeros((bp, 1, d), F32), xp3[:, :-1]], axis=1)
            prev_s = jnp.concatenate([state_shift[i][:, None, :], xs3[:, :-1]], axis=1)
            x_prev = jnp.concatenate([prev_p.reshape(mp, d), prev_s.reshape(ms, d)], axis=0)
            vec = jnp.stack([rw_w0[i], rw_a0[i], rw_v0[i - 1] if i > 0 else jnp.zeros((d,), F32)])
            mats = [rw_wr[i].astype(BF16), rw_wk[i].astype(BF16), rw_wv[i].astype(BF16),
                    _pad_cols(rw_w1[i]), _pad_rows(rw_w2[i]), _pad_cols(rw_a1[i]), _pad_rows(rw_a2[i]),
                    _pad_cols(rw_g1[i]), _pad_rows(rw_g2[i])]
            if i > 0:
                mats += [_pad_cols(rw_v1[i - 1]), _pad_rows(rw_v2[i - 1])]
            r, k, v, a, ld, g = a_proj(x, x_prev, v_first, rw_mu[i], vec, mats, tm=tm)
            if i == 0:
                v_first = v
            prm = jnp.stack([rw_kk[i], rw_ka[i], rw_rk[i].reshape(d), rw_gn_g[i], rw_gn_b[i]])
            seqs = (r, k, v, a, ld, g)
            y, sp = wkv_scan(seqs, prm, jnp.zeros((bp, PAIRS, LANES, LANES), F32), row0=0, t_len=seq_len,
                             chunk=SCAN_CHUNK, t_block=t_block)
            y, ss = wkv_scan(seqs, prm, pair_states(state_wkv[i]), row0=mp, t_len=dec_len,
                             chunk=dec_len, t_block=dec_len, out=y)
            wkv_p.append(unpair_states(sp))
            wkv_s.append(unpair_states(ss))
            x1, gate_l, idx_l = post(layer, y, x, rw_wo[i])
        else:
            j = layer - N_A_LAYERS
            q = dense(x, att_wq[j].astype(BF16), tm=dense_tile, tn=D_MODEL)
            parts = []
            for gi, (window, dil) in enumerate(GROUPS):
                assert window // dil == ATT_STEPS
                parts += list(attn_prompt_group(q, kv, group=gi, bsz=bp, seq_len=seq_len, dil=dil))
            y_s = attn_sample(q, kv, cache_k.reshape(bs, kv_buf, d), cache_v.reshape(bs, kv_buf, d),
                              row0=mp, bsz=bs, t_len=dec_len)
            x1, gate_l, idx_l = post(layer, (tuple(parts), y_s), x, att_wo[j])
        x = moe(layer, x1, gate_l, idx_l)
        if layer == N_A_LAYERS - 1:
            kv = dense(x, kv_w.astype(BF16), tm=dense_tile, tn=D_MODEL)

    buf_p = min(GROUPS[-1][0], seq_len)
    heads = lambda t, n: t.reshape(t.shape[0], n, N_HEADS, HEAD)
    tails = [kv[(b + 1) * seq_len - buf_p:(b + 1) * seq_len] for b in range(bp)]
    kv_s = kv[mp:].reshape(bs, dec_len, 2 * d)
    k_p_out = heads(jnp.stack([t[:, :d] for t in tails]), buf_p)
    v_p_out = heads(jnp.stack([t[:, d:] for t in tails]), buf_p)
    k_s_out = jnp.concatenate([cache_k, heads(kv_s[:, :, :d], dec_len)], axis=1)[:, -kv_buf:]
    v_s_out = jnp.concatenate([cache_v, heads(kv_s[:, :, d:], dec_len)], axis=1)[:, -kv_buf:]
    return (x[:mp].reshape(bp, seq_len, d), x[mp:].reshape(bs, dec_len, d),
            jnp.stack(wkv_p), jnp.stack(shift_p), k_p_out, v_p_out,
            jnp.stack(wkv_s), jnp.stack(shift_s), k_s_out, v_s_out)
```

```python
import functools

import jax
import jax.numpy as jnp
from jax import lax
from jax.experimental import pallas as pl
from jax.experimental.pallas import tpu as pltpu

F32 = jnp.float32
BF16 = jnp.bfloat16

D_MODEL = 1024
HEAD = 64
N_HEADS = D_MODEL // HEAD
LANES = 128
PAIRS = D_MODEL // LANES
DEPTH = 4
N_A_LAYERS = DEPTH // 2
LORA_PAD = 128
GN_EPS = 64e-5
LN_EPS = 1e-5
DN_ALPHA = (2 * DEPTH) ** 0.25
GROUPS = ((128, 1), (512, 4), (2048, 16))
ATT_STEPS = 128
N_EXPERTS = 32
TOP_K = 4
SWIGLU_LIMIT = 7.0
SWIGLU_ALPHA = 1.702
MOE_ROWS = 256
TOKEN_TILE = 256
DENSE_TILE = 1280
SCAN_CHUNK = 64
INV_BASE = 8
CAST_SLAB = 128
SCAN_T_BLOCK = 256
NEG_BIG = -1e30
VMEM_LIMIT = 56 * 1024 * 1024


def _cparams(*sem):
    return pltpu.CompilerParams(dimension_semantics=sem, vmem_limit_bytes=VMEM_LIMIT)


def _mm(a, b):
    return jnp.dot(a.astype(BF16), b.astype(BF16), preferred_element_type=F32)


def _mm_nt(a, b):
    return lax.dot_general(a.astype(BF16), b.astype(BF16), (((1,), (1,)), ((), ())),
                           preferred_element_type=F32)


def _mm_tn(a, b):
    return lax.dot_general(a.astype(BF16), b.astype(BF16), (((0,), (0,)), ((), ())),
                           preferred_element_type=F32)


def _split(x, parts):
    out = []
    for _ in range(parts):
        h = x.astype(BF16)
        out.append(h)
        x = x - h.astype(F32)
    return out


def _mm_sel_r(x, sel, parts=2):
    acc = None
    for h in _split(x, parts):
        t = jnp.dot(h, sel, preferred_element_type=F32)
        acc = t if acc is None else acc + t
    return acc


def _mm_sel_l(sel, x, parts=3):
    acc = None
    for h in _split(x, parts):
        t = jnp.dot(sel, h, preferred_element_type=F32)
        acc = t if acc is None else acc + t
    return acc


def _sigmoid(x):
    return 1.0 / (1.0 + jnp.exp(-x))


def _layer_norm(x, g, b):
    mu = jnp.mean(x, axis=-1, keepdims=True)
    xc = x - mu
    var = jnp.mean(xc * xc, axis=-1, keepdims=True)
    return xc * lax.rsqrt(var + LN_EPS) * g + b


def _each(fn, *lists):
    return [fn(*xs) for xs in zip(*lists)]


def _unit_lower_inverse(a_side, cst, chunk):
    eye, row, scol, bd = cst
    base = INV_BASE
    blk = (row // base) == (scol // base)

    def mul(xs, ys):
        return _each(_mm, xs, [bd(y) for y in ys])

    x = [jnp.where(blk, -a, 0.0) for a in a_side]
    x2 = mul(x, x)
    x4 = mul(x2, x2)
    xx2 = mul(x, x2)
    y = _each(lambda x_, x2_, xx2_: eye + x_ + x2_ + xx2_, x, x2, xx2)
    t = _each(jnp.add, y, mul(y, x4))
    s = base
    while s < chunk:
        rb = row // s
        off = (rb == (scol // s) + 1) & ((rb % 2) == 1)
        a_off = [jnp.where(off, a, 0.0) for a in a_side]
        t = _each(jnp.subtract, t, mul(t, mul(a_off, t)))
        s *= 2
    return t


def _wkv_chunk(s_mat, r, kr, v, a, ld, g, prm, cst, chunk):
    kk_p, ka_p, rk_p, gng, gnb = prm
    m_a, e_seg, e_seg2, tri, eye, row, scol, strict, incl, bd = cst
    c = chunk
    kkr = _each(jnp.multiply, kr, kk_p)
    ss = [_mm_sel_r(x * x, e_seg) for x in kkr]
    kk = _each(lambda x, s_: x / jnp.maximum(jnp.sqrt(s_), 1e-12), kkr, ss)
    k = _each(lambda kr_, a_, ka_: kr_ * (1.0 + (a_ - 1.0) * ka_), kr, a, ka_p)
    b = _each(jnp.multiply, kk, a)
    cl = [_mm_sel_l(tri, x) for x in ld]
    cl_end = [x[c - 1:c, :] for x in cl]

    def stack(x):
        return jnp.concatenate([jnp.where(m_a, x, 0.0), jnp.where(m_a, 0.0, x)], axis=0)

    kkg = _each(lambda kk_, cl_, ld_: kk_ * jnp.exp(cl_ - ld_), kk, cl, ld)
    rg = _each(lambda r_, cl_: r_ * jnp.exp(cl_), r, cl)
    g_inv = [jnp.exp(-x) for x in cl]
    bd_s = _each(lambda b_, gi: stack(b_ * gi), b, g_inv)
    kd_s = _each(lambda k_, gi: stack(k_ * gi), k, g_inv)
    g_end = _each(lambda ce, cl_: jnp.exp(ce - cl_), cl_end, cl)
    be_s = _each(lambda b_, ge: stack(b_ * ge), b, g_end)
    ke_s = _each(lambda k_, ge: stack(k_ * ge), k, g_end)
    v_s = [stack(x) for x in v]

    if (2 * c) % LANES == 0:
        bk_s = _each(lambda x_, y_: jnp.concatenate([x_, y_], axis=0), bd_s, kd_s)
        ab = _each(_mm_nt, kkg, bk_s)
        rbk = _each(_mm_nt, rg, bk_s)
        a_side, b_side = [x[:, :2 * c] for x in ab], [x[:, 2 * c:] for x in ab]
        rb_side, rk_side = [x[:, :2 * c] for x in rbk], [x[:, 2 * c:] for x in rbk]
    else:
        a_side, b_side = _each(_mm_nt, kkg, bd_s), _each(_mm_nt, kkg, kd_s)
        rb_side, rk_side = _each(_mm_nt, rg, bd_s), _each(_mm_nt, rg, kd_s)
    a_side = [jnp.where(strict, x, 0.0) for x in a_side]
    b_side = [jnp.where(strict, x, 0.0) for x in b_side]
    rb_side = [jnp.where(incl, x, 0.0) for x in rb_side]
    rk_side = [jnp.where(incl, x, 0.0) for x in rk_side]
    t_side = _unit_lower_inverse(a_side, (eye, row, scol, bd), c)

    bv = _each(_mm, b_side, v_s)
    gu = _each(lambda t_, kkg_, bv_: _mm(t_, jnp.concatenate([stack(kkg_), stack(bv_)], axis=1)), t_side, kkg, bv)
    g_s = [stack(x[:, :LANES]) for x in gu]
    u1_s = [stack(x[:, LANES:]) for x in gu]
    pq = _each(lambda rb_, gs_, u1_: _mm(rb_, jnp.concatenate([gs_, u1_], axis=1)), rb_side, g_s, u1_s)
    rkv = _each(_mm, rk_side, v_s)
    p = _each(lambda rg_, pq_: rg_ - pq_[:, :LANES], rg, pq)
    q = _each(lambda rkv_, pq_: rkv_ - pq_[:, LANES:], rkv, pq)
    o = _each(lambda p_, s_, q_: _mm_nt(p_, s_) + q_, p, s_mat, q)
    gb = _each(_mm_tn, g_s, be_s)
    nt = _each(lambda vs_, u1_, ke_, be_: _mm_tn(jnp.concatenate([vs_, -u1_], axis=0),
                                                   jnp.concatenate([ke_, be_], axis=0)), v_s, u1_s, ke_s, be_s)
    s_new = _each(lambda s_, ce, gb_, nt_: s_ * jnp.exp(ce) - _mm(s_, gb_) + nt_, s_mat, cl_end, gb, nt)

    mb = _each(lambda o_, r_, k_, rk_: _mm_sel_r(jnp.concatenate([o_, r_ * k_ * rk_], axis=1), e_seg2),
               o, r, k, rk_p)
    d = _each(lambda o_, mb_: o_ - mb_[:, :LANES] * (1.0 / HEAD), o, mb)
    var = [_mm_sel_r(x * x, e_seg) * (1.0 / HEAD) for x in d]
    out = _each(lambda d_, var_, gg, gb_, mb_, v_, g_:
                (d_ * lax.rsqrt(var_ + GN_EPS) * gg + gb_ + mb_[:, LANES:] * v_) * g_,
                d, var, gng, gnb, mb, v, g)
    return s_new, out


def _wkv_kernel(r_ref, k_ref, v_ref, a_ref, ld_ref, g_ref, prm_ref, s0_ref, *rest, chunk, n_chunks, n_pairs):
    o_ref, s_out_ref, s_scr = rest[-3:]
    tb = pl.program_id(2)

    @pl.when(tb == 0)
    def _():
        s_scr[...] = s0_ref[0]

    c2 = 2 * chunk
    lane = lax.broadcasted_iota(jnp.int32, (1, LANES), 1)
    m_a = lane < HEAD
    er = lax.broadcasted_iota(jnp.int32, (2 * LANES, 2 * LANES), 0)
    ec = lax.broadcasted_iota(jnp.int32, (2 * LANES, 2 * LANES), 1)
    e_seg2 = ((er // HEAD) == (ec // HEAD)).astype(BF16)
    e_seg = e_seg2[:LANES, :LANES]
    tr = lax.broadcasted_iota(jnp.int32, (chunk, chunk), 0)
    tc = lax.broadcasted_iota(jnp.int32, (chunk, chunk), 1)
    tri = (tr >= tc).astype(BF16)
    row = lax.broadcasted_iota(jnp.int32, (chunk, c2), 0)
    col = lax.broadcasted_iota(jnp.int32, (chunk, c2), 1)
    scol = col % chunk
    strict = scol < row
    incl = scol <= row
    eye = (row == scol).astype(F32)
    left = col < chunk

    def bd(x):
        return jnp.concatenate([jnp.where(left, x, 0.0), jnp.where(left, 0.0, x)], axis=0)

    cst = (m_a, e_seg, e_seg2, tri, eye, row, scol, strict, incl, bd)

    def body(ci, carry):
        sl = pl.ds(pl.multiple_of(ci * chunk, chunk), chunk)
        lanes = [slice(p * LANES, (p + 1) * LANES) for p in range(n_pairs)]
        prm = tuple([prm_ref[i:i + 1, ln] for ln in lanes] for i in range(5))
        seqs = [[ref[sl, ln] for ln in lanes] for ref in (r_ref, k_ref, v_ref, a_ref, ld_ref, g_ref)]
        s_new, out = _wkv_chunk([s_scr[p] for p in range(n_pairs)], *seqs, prm, cst, chunk)
        for p in range(n_pairs):
            s_scr[p] = s_new[p]
            o_ref[sl, lanes[p]] = out[p]
        return carry

    lax.fori_loop(0, n_chunks, body, 0, unroll=min(2, n_chunks))

    @pl.when(tb == pl.num_programs(2) - 1)
    def _():
        s_out_ref[0] = s_scr[...]


def wkv_scan(seqs, prm, s0, *, row0, t_len, chunk, t_block, n_pairs=PAIRS, out=None):
    m_rows, d = seqs[0].shape
    bsz = s0.shape[0]
    assert d == D_MODEL and t_len % t_block == 0 and t_block % chunk == 0 and row0 % t_block == 0
    assert PAIRS % n_pairs == 0 and row0 + bsz * t_len <= m_rows
    width = n_pairs * LANES
    nt = t_len // t_block
    blk0 = row0 // t_block
    seq = pl.BlockSpec((t_block, width), lambda b, p, t: (blk0 + b * nt + t, p))
    st = pl.BlockSpec((1, n_pairs, LANES, LANES), lambda b, p, t: (b, p, 0, 0))
    in_specs = [seq] * 6 + [pl.BlockSpec((5, width), lambda b, p, t: (0, p)), st]
    args = list(seqs) + [prm, s0]
    aliases = {}
    if out is not None:
        in_specs.append(pl.BlockSpec(memory_space=pl.ANY))
        args.append(out)
        aliases = {len(args) - 1: 0}
    return pl.pallas_call(
        functools.partial(_wkv_kernel, chunk=chunk, n_chunks=t_block // chunk, n_pairs=n_pairs),
        grid=(bsz, PAIRS // n_pairs, nt),
        in_specs=in_specs,
        out_specs=[seq, st],
        out_shape=[jax.ShapeDtypeStruct((m_rows, d), F32),
                   jax.ShapeDtypeStruct((bsz, PAIRS, LANES, LANES), F32)],
        scratch_shapes=[pltpu.VMEM((n_pairs, LANES, LANES), F32)],
        input_output_aliases=aliases,
        compiler_params=_cparams("parallel", "parallel", "arbitrary"),
        name="wkv_scan",
    )(*args)


def pair_states(s):
    bsz = s.shape[0]
    s = s.reshape(bsz, PAIRS, 2, HEAD, HEAD)
    z = jnp.zeros_like(s[:, :, 0])
    top = jnp.concatenate([s[:, :, 0], z], axis=-1)
    bot = jnp.concatenate([z, s[:, :, 1]], axis=-1)
    return jnp.concatenate([top, bot], axis=-2)


def unpair_states(sp):
    bsz = sp.shape[0]
    s = jnp.stack([sp[:, :, :HEAD, :HEAD], sp[:, :, HEAD:, HEAD:]], axis=2)
    return s.reshape(bsz, N_HEADS, HEAD, HEAD)


def _a_proj_kernel(*refs, has_vres):
    if has_vres:
        (x_ref, xp_ref, vf_ref, mu_ref, vec_ref, wr, wk, wv, w1, w2, a1, a2, g1, g2, v1, v2,
         r_o, k_o, v_o, a_o, ld_o, g_o) = refs
    else:
        (x_ref, xp_ref, mu_ref, vec_ref, wr, wk, wv, w1, w2, a1, a2, g1, g2,
         r_o, k_o, v_o, a_o, ld_o, g_o) = refs
    x = x_ref[...]
    xx = xp_ref[...] - x
    xr, xw, xk, xv, xa, xg = [(x + xx * mu_ref[i:i + 1, :]).astype(BF16) for i in range(6)]
    r_o[...] = _mm(xr, wr[...])
    k_o[...] = _mm(xk, wk[...])
    v = _mm(xv, wv[...])
    z = vec_ref[0:1, :] + _mm(jnp.tanh(_mm(xw, w1[...])), w2[...])
    softplus_neg = jnp.maximum(-z, 0.0) + jnp.log(1.0 + jnp.exp(-jnp.abs(z)))
    ld_o[...] = -jnp.exp(-softplus_neg - 0.5)
    if has_vres:
        mix = _sigmoid(vec_ref[2:3, :] + _mm(_mm(xv, v1[...]), v2[...]))
        v = v + (vf_ref[...] - v) * mix
    v_o[...] = v
    a_o[...] = _sigmoid(vec_ref[1:2, :] + _mm(_mm(xa, a1[...]), a2[...]))
    g_o[...] = _mm(_sigmoid(_mm(xg, g1[...])), g2[...])


def a_proj(x, x_prev, v_first, mu, vec, mats, *, tm):
    m_rows = x.shape[0]
    assert m_rows % tm == 0
    tok = pl.BlockSpec((tm, D_MODEL), lambda i: (i, 0))
    full = lambda a: pl.BlockSpec(a.shape, lambda i: (0, 0))
    has_vres = v_first is not None
    acts = [x, x_prev] + ([v_first] if has_vres else [])
    consts = [mu, vec] + list(mats)
    return pl.pallas_call(
        functools.partial(_a_proj_kernel, has_vres=has_vres),
        grid=(m_rows // tm,),
        in_specs=[tok] * len(acts) + [full(c) for c in consts],
        out_specs=[tok] * 6,
        out_shape=[jax.ShapeDtypeStruct((m_rows, D_MODEL), F32)] * 6,
        compiler_params=_cparams("parallel"),
        name="a_proj",
    )(*acts, *consts)


def _post_kernel(*refs, n_prompt_tiles):
    if n_prompt_tiles is None:
        y_ref = refs[0]
        y_in = y_ref[...]
        rest = refs[1:]
    else:
        o0, l0, o1, l1, o2, l2, ys_ref = refs[:7]
        rest = refs[7:]
        m = jnp.maximum(jnp.maximum(l0[...], l1[...]), l2[...])
        w0 = jnp.exp(l0[...] - m)
        w1 = jnp.exp(l1[...] - m)
        w2 = jnp.exp(l2[...] - m)
        merged = (w0 * o0[...] + w1 * o1[...] + w2 * o2[...]) / (w0 + w1 + w2)
        y_in = jnp.where(pl.program_id(0) < n_prompt_tiles, merged, ys_ref[...])
    x_ref, wo_ref, ln_ref, wrh_ref, wrl_ref, br_ref, x1_ref, gate_ref, idx_ref = rest
    y = _mm(y_in, wo_ref[...])
    x1 = _layer_norm(DN_ALPHA * x_ref[...] + y, ln_ref[0:1, :], ln_ref[1:2, :])
    x1_ref[...] = x1
    parts = _split(x1, 3)
    acc = br_ref[...]
    for h in parts:
        acc = acc + jnp.dot(h, wrh_ref[...], preferred_element_type=F32)
    for h in parts[:2]:
        acc = acc + jnp.dot(h, wrl_ref[...], preferred_element_type=F32)
    lane = lax.broadcasted_iota(jnp.int32, acc.shape, 1)
    lane_f = lane.astype(F32)
    lg = jnp.where(lane < N_EXPERTS, acc, -jnp.inf)
    vals = []
    idx_out = jnp.zeros(acc.shape, F32)
    for k in range(TOP_K):
        v = jnp.max(lg, axis=-1, keepdims=True)
        idx = jnp.min(jnp.where(lg == v, lane_f, float(LANES)), axis=-1, keepdims=True)
        vals.append(v)
        idx_out = jnp.where(lane == k, idx, idx_out)
        lg = jnp.where(lane_f == idx, -jnp.inf, lg)
    es = [jnp.exp(v - vals[0]) for v in vals]
    den = es[0]
    for e in es[1:]:
        den = den + e
    gate = jnp.zeros(acc.shape, F32)
    for k in range(TOP_K):
        gate = jnp.where(lane == k, es[k] / den, gate)
    gate_ref[...] = gate
    idx_ref[...] = idx_out.astype(jnp.int32)


def post_mix(y, x, wo, ln, wr_hi, wr_lo, br, *, tm):
    m_rows = x.shape[0]
    tok = pl.BlockSpec((tm, D_MODEL), lambda i: (i, 0))
    full = lambda a: pl.BlockSpec(a.shape, lambda i: (0, 0))
    if isinstance(y, tuple):
        parts, y_sample = y
        n_p = parts[0].shape[0] // tm
        assert parts[0].shape[0] % tm == 0 and y_sample.shape[0] % tm == 0
        prompt = pl.BlockSpec((tm, D_MODEL), lambda i: (jnp.minimum(i, n_p - 1), 0))
        sample = pl.BlockSpec((tm, D_MODEL), lambda i: (jnp.maximum(i - n_p, 0), 0))
        y_args, y_specs = list(parts) + [y_sample], [prompt] * 6 + [sample]
    else:
        n_p = None
        y_args, y_specs = [y], [tok]
    return pl.pallas_call(
        functools.partial(_post_kernel, n_prompt_tiles=n_p),
        grid=(m_rows // tm,),
        in_specs=y_specs + [tok] + [full(c) for c in (wo, ln, wr_hi, wr_lo, br)],
        out_specs=[tok, pl.BlockSpec((tm, LANES), lambda i: (i, 0)), pl.BlockSpec((tm, LANES), lambda i: (i, 0))],
        out_shape=[jax.ShapeDtypeStruct((m_rows, D_MODEL), F32),
                   jax.ShapeDtypeStruct((m_rows, LANES), F32),
                   jax.ShapeDtypeStruct((m_rows, LANES), jnp.int32)],
        compiler_params=_cparams("parallel"),
        name="post_mix",
    )(*y_args, x, wo, ln, wr_hi, wr_lo, br)


def _moe_kernel(be_ref, first_ref, nact_ref, xb_ref, win_ref, bin_ref, wout_ref, bout_ref, *rest):
    y_ref, win_s, wout_s = rest[-3:]
    i = pl.program_id(0)

    @pl.when(first_ref[i] == 1)
    def _():
        for j in range(D_MODEL // CAST_SLAB):
            rows = slice(j * CAST_SLAB, (j + 1) * CAST_SLAB)
            win_s[rows, :] = win_ref[0, rows, :].astype(BF16)
            wout_s[rows, :] = wout_ref[0, rows, :].astype(BF16)

    @pl.when(i < nact_ref[0])
    def _():
        h = jnp.dot(xb_ref[...].astype(BF16), win_s[...], preferred_element_type=F32) + bin_ref[0]
        h_gate = jnp.minimum(h[:, :D_MODEL], SWIGLU_LIMIT)
        h_up = jnp.clip(h[:, D_MODEL:], -SWIGLU_LIMIT, SWIGLU_LIMIT)
        act = (h_up + 1.0) * h_gate * _sigmoid(SWIGLU_ALPHA * h_gate)
        y_ref[...] = jnp.dot(act.astype(BF16), wout_s[...], preferred_element_type=F32) + bout_ref[0]


def moe_experts(block_e, first, n_act, xb, w_in, b_in, w_out, b_out, *, layer, block0, total_blocks, out=None):
    rows = xb.shape[0]
    n_blocks = rows // MOE_ROWS
    args = [block_e, first, n_act, xb, w_in, b_in.reshape(N_EXPERTS, 1, -1), w_out, b_out.reshape(N_EXPERTS, 1, -1)]
    extra_specs, aliases = [], {}
    if out is not None:
        extra_specs = [pl.BlockSpec(memory_space=pl.ANY)]
        args.append(out)
        aliases = {len(args) - 1: 0}
    grid_spec = pltpu.PrefetchScalarGridSpec(
        num_scalar_prefetch=3,
        grid=(n_blocks,),
        in_specs=[
            pl.BlockSpec((MOE_ROWS, D_MODEL), lambda i, be, fi, na: (i, 0)),
            pl.BlockSpec((None, 1, D_MODEL, 2 * D_MODEL), lambda i, be, fi, na: (layer, be[i], 0, 0)),
            pl.BlockSpec((1, 1, 2 * D_MODEL), lambda i, be, fi, na: (be[i], 0, 0)),
            pl.BlockSpec((None, 1, D_MODEL, D_MODEL), lambda i, be, fi, na: (layer, be[i], 0, 0)),
            pl.BlockSpec((1, 1, D_MODEL), lambda i, be, fi, na: (be[i], 0, 0)),
        ] + extra_specs,
        out_specs=pl.BlockSpec((MOE_ROWS, D_MODEL), lambda i, be, fi, na: (block0 + i, 0)),
        scratch_shapes=[pltpu.VMEM((D_MODEL, 2 * D_MODEL), BF16), pltpu.VMEM((D_MODEL, D_MODEL), BF16)],
    )
    return pl.pallas_call(
        _moe_kernel,
        grid_spec=grid_spec,
        out_shape=jax.ShapeDtypeStruct((total_blocks * MOE_ROWS, D_MODEL), F32),
        input_output_aliases=aliases,
        compiler_params=_cparams("arbitrary"),
        name="moe_experts",
    )(*args)


def _combine_kernel(yg_ref, gate_ref, x_ref, ln_ref, o_ref):
    gate = gate_ref[...]
    ffn = gate[:, 0:1] * yg_ref[0]
    for k in range(1, TOP_K):
        ffn = ffn + gate[:, k:k + 1] * yg_ref[k]
    o_ref[...] = _layer_norm(DN_ALPHA * x_ref[...] + ffn, ln_ref[0:1, :], ln_ref[1:2, :])


def moe_combine(yg, gate, x, ln, *, tm):
    m_rows = x.shape[0]
    tok = pl.BlockSpec((tm, D_MODEL), lambda i: (i, 0))
    return pl.pallas_call(
        _combine_kernel,
        grid=(m_rows // tm,),
        in_specs=[pl.BlockSpec((TOP_K, tm, D_MODEL), lambda i: (0, i, 0)),
                  pl.BlockSpec((tm, TOP_K), lambda i: (i, 0)), tok,
                  pl.BlockSpec(ln.shape, lambda i: (0, 0))],
        out_specs=tok,
        out_shape=jax.ShapeDtypeStruct((m_rows, D_MODEL), F32),
        compiler_params=_cparams("parallel"),
        name="moe_combine",
    )(yg, gate, x, ln)


def _route(top_e, m_rows):
    mk = m_rows * TOP_K
    flat_e = top_e.reshape(-1).astype(jnp.int32)
    onehot = (flat_e[:, None] == jnp.arange(N_EXPERTS, dtype=jnp.int32)[None, :]).astype(jnp.int32)
    csum = jnp.cumsum(onehot, axis=0)
    counts = csum[-1]
    rank = jnp.sum((csum - onehot) * onehot, axis=1)
    padded = (counts + MOE_ROWS - 1) // MOE_ROWS * MOE_ROWS
    pad_end = jnp.cumsum(padded)
    pad_start = pad_end - padded
    start = jnp.cumsum(counts) - counts
    pos = jnp.sum(onehot * pad_start[None, :], axis=1) + rank
    order = jnp.argsort(flat_e).astype(jnp.int32)
    n_blocks = -(-mk // MOE_ROWS) + N_EXPERTS
    blk_start = jnp.arange(n_blocks, dtype=jnp.int32) * MOE_ROWS
    block_e = jnp.sum((pad_end[None, :] <= blk_start[:, None]).astype(jnp.int32), axis=1)
    block_e = jnp.minimum(block_e, N_EXPERTS - 1)
    first = ((blk_start == pad_start[block_e]) & (blk_start < pad_end[-1])).astype(jnp.int32)
    n_act = (pad_end[-1:] // MOE_ROWS).astype(jnp.int32)
    e_row = jnp.repeat(block_e, MOE_ROWS)
    j_row = jnp.arange(n_blocks * MOE_ROWS, dtype=jnp.int32) - pad_start[e_row]
    compact = jnp.clip(start[e_row] + j_row, 0, mk - 1)
    row_tok = jnp.where(j_row < counts[e_row], order[compact] // TOP_K, 0)
    return row_tok, pos, block_e, first, n_act


def moe_layer(x1, gate_l, idx_l, w_in, b_in, w_out, b_out, ln, *, tm, layer):
    m_rows = x1.shape[0]
    gate = gate_l[:, :TOP_K]
    row_tok, pos, block_e, first, n_act = _route(idx_l[:, :TOP_K], m_rows)
    n_blocks = block_e.shape[0]
    half = n_blocks // 2
    yb = None
    for lo, hi in ((0, half), (half, n_blocks)):
        xb = x1[row_tok[lo * MOE_ROWS:hi * MOE_ROWS]]
        first_h = jnp.concatenate([jnp.ones((1,), jnp.int32), first[lo + 1:hi]])
        n_act_h = jnp.clip(n_act - lo, 0, hi - lo)
        yb = moe_experts(block_e[lo:hi], first_h, n_act_h, xb, w_in, b_in, w_out, b_out, layer=layer,
                         block0=lo, total_blocks=n_blocks, out=yb)
    yg = yb[pos.reshape(m_rows, TOP_K).T]
    return moe_combine(yg, gate, x1, ln, tm=tm)


def _dense_kernel(x_ref, w_ref, o_ref):
    o_ref[...] = _mm(x_ref[...], w_ref[...]).astype(o_ref.dtype)


def dense(x, w, *, tm, tn, out_dtype=F32):
    m_rows, k_dim = x.shape
    n_dim = w.shape[1]
    assert m_rows % tm == 0 and n_dim % tn == 0
    return pl.pallas_call(
        _dense_kernel,
        grid=(n_dim // tn, m_rows // tm),
        in_specs=[pl.BlockSpec((tm, k_dim), lambda j, i: (i, 0)),
                  pl.BlockSpec((k_dim, tn), lambda j, i: (0, j))],
        out_specs=pl.BlockSpec((tm, tn), lambda j, i: (i, j)),
        out_shape=jax.ShapeDtypeStruct((m_rows, n_dim), out_dtype),
        compiler_params=_cparams("parallel", "parallel"),
        name="dense",
    )(x, w)


def _slope(head):
    return 2.0 ** (-8.0 * (head + 1) / N_HEADS)


def _attn_prompt_kernel(slope_ref, q_ref, kp_ref, kc_ref, vp_ref, vc_ref, o_ref, lse_ref, bias_scr, *, dil, n_pairs):
    lb = pl.program_id(1)
    n = pl.program_id(2)
    nk = ATT_STEPS
    lane = lax.broadcasted_iota(jnp.int32, (1, LANES), 1)
    m_a = lane < HEAD

    @pl.when(n == 0)
    def _():
        qi = lax.broadcasted_iota(jnp.int32, (nk, 2 * nk), 0)
        kj = lax.broadcasted_iota(jnp.int32, (nk, 2 * nk), 1)
        delta = qi + nk - kj
        valid = (delta >= 0) & (delta <= nk)
        dist = (delta * dil).astype(F32)
        for h in range(2 * n_pairs):
            bias = jnp.where(valid, -slope_ref[lb * 2 * n_pairs + h] * dist, NEG_BIG)
            bias_scr[1, h] = bias
            bias_scr[0, h] = jnp.where(kj >= nk, bias, NEG_BIG)

    table = jnp.minimum(n, 1)

    def scores(item):
        rows, p = item
        ln = slice(p * LANES, (p + 1) * LANES)
        q = q_ref[rows, ln] * (HEAD ** -0.5)
        k = jnp.concatenate([kp_ref[rows, ln], kc_ref[rows, ln]], axis=0).astype(BF16)
        return [_mm_nt(jnp.where(m_a, q, 0.0), k), _mm_nt(jnp.where(m_a, 0.0, q), k)]

    def run(items):
        groups = [items[i:i + 2] for i in range(0, len(items), 2)]
        s_next = [scores(it) for it in groups[0]]
        for gi, grp in enumerate(groups):
            s_cur = s_next
            if gi + 1 < len(groups):
                s_next = [scores(it) for it in groups[gi + 1]]
            lns = [slice(p * LANES, (p + 1) * LANES) for _, p in grp]
            vs = [jnp.concatenate([vp_ref[rows, ln], vc_ref[rows, ln]], axis=0).astype(BF16)
                  for (rows, _), ln in zip(grp, lns)]
            s = [s_cur[i][hh] + bias_scr[table, 2 * p + hh] for i, (_, p) in enumerate(grp) for hh in range(2)]
            m = [jnp.max(x, axis=-1, keepdims=True) for x in s]
            e = _each(lambda x, m_: jnp.exp(x - m_), s, m)
            l = [jnp.sum(x, axis=-1, keepdims=True) for x in e]
            pv = [jnp.dot(x.astype(BF16), vs[j // 2], preferred_element_type=F32) for j, x in enumerate(e)]
            outs = _each(jnp.divide, pv, l)
            lses = _each(lambda m_, l_: m_ + jnp.log(l_), m, l)
            for i, ((rows, _), ln) in enumerate(zip(grp, lns)):
                o_ref[rows, ln] = jnp.where(m_a, outs[2 * i], outs[2 * i + 1])
                lse_ref[rows, ln] = jnp.where(m_a, lses[2 * i], lses[2 * i + 1])

    if dil == 1:
        run([(slice(None), p) for p in range(n_pairs)])
    else:
        group = min(dil, ATT_CLASS_UNROLL)

        def body(gi, carry):
            run([(pl.ds(gi * group + u, nk, stride=dil), 0) for u in range(group)])
            return carry

        lax.fori_loop(0, dil // group, body, 0)


ATT_CLASS_UNROLL = 8


def attn_prompt_group(q, kv, *, group, bsz, seq_len, dil):
    tile = ATT_STEPS * dil
    assert seq_len % tile == 0
    n_tiles = seq_len // tile
    n_pairs = PAIRS if dil == 1 else 1
    width = n_pairs * LANES
    n_lb = D_MODEL // width
    slopes = jnp.asarray([_slope(h) for h in range(N_HEADS)], F32)

    def spec(col0, back):
        return pl.BlockSpec((tile, width),
                            lambda b, lb, n, sl: (b * n_tiles + jnp.maximum(n - back, 0), col0 * n_lb + lb))

    out = pl.BlockSpec((tile, width), lambda b, lb, n, sl: (b * n_tiles + n, lb))
    grid_spec = pltpu.PrefetchScalarGridSpec(
        num_scalar_prefetch=1,
        grid=(bsz, n_lb, n_tiles),
        in_specs=[spec(group, 0), spec(0, 1), spec(0, 0), spec(1, 1), spec(1, 0)],
        out_specs=[out, out],
        scratch_shapes=[pltpu.VMEM((2, 2 * n_pairs, ATT_STEPS, 2 * ATT_STEPS), F32)],
    )
    return pl.pallas_call(
        functools.partial(_attn_prompt_kernel, dil=dil, n_pairs=n_pairs),
        grid_spec=grid_spec,
        out_shape=[jax.ShapeDtypeStruct((bsz * seq_len, D_MODEL), F32)] * 2,
        compiler_params=_cparams("parallel", "parallel", "arbitrary"),
        name="attn_prompt",
    )(slopes, q, kv, kv, kv, kv)


SAMPLE_PAIRS = 4


def _attn_sample_kernel(q0_ref, q1_ref, q2_ref, kc_ref, kn_ref, vc_ref, vn_ref, o_ref, *, t_len, kv_buf, n_pairs):
    pb = pl.program_id(1)
    lane = lax.broadcasted_iota(jnp.int32, (1, LANES), 1)
    m_a = lane < HEAD
    n_rows = 6 * t_len
    ri = lax.broadcasted_iota(jnp.int32, (n_rows, 1), 0)
    t = ri % t_len
    grp = ri // (2 * t_len)
    hh = (ri // t_len) % 2
    dmask = jnp.where(grp == 0, GROUPS[0][1] - 1, jnp.where(grp == 1, GROUPS[1][1] - 1, GROUPS[2][1] - 1))
    win = jnp.where(grp == 0, GROUPS[0][0], jnp.where(grp == 1, GROUPS[1][0], GROUPS[2][0]))
    jc = lax.broadcasted_iota(jnp.int32, (n_rows, kv_buf), 1)
    jn = lax.broadcasted_iota(jnp.int32, (n_rows, LANES), 1)

    def band(dist):
        return (dist >= 0) & (dist <= win) & ((dist & dmask) == 0), dist.astype(F32)

    ok_c, dist_c = band(kv_buf + t - jc)
    ok_n, dist_n = band(t - jn)
    pad = jnp.zeros((LANES - t_len, LANES), F32)
    blk = 2 * t_len

    for p in range(n_pairs):
        ln = slice(p * LANES, (p + 1) * LANES)
        lhs = []
        for q_ref in (q0_ref, q1_ref, q2_ref):
            q = q_ref[:, ln] * (HEAD ** -0.5)
            lhs += [jnp.where(m_a, q, 0.0), jnp.where(m_a, 0.0, q)]
        lhs = jnp.concatenate(lhs, axis=0)
        kn = jnp.concatenate([kn_ref[:, ln], pad], axis=0)
        vn = jnp.concatenate([vn_ref[:, ln], pad], axis=0)
        head = (2 * (pb * n_pairs + p) + hh).astype(F32)
        slope = jnp.exp2(-8.0 * (head + 1.0) / N_HEADS)
        s_c = jnp.where(ok_c, _mm_nt(lhs, kc_ref[:, ln]) - slope * dist_c, NEG_BIG)
        s_n = jnp.where(ok_n, _mm_nt(lhs, kn) - slope * dist_n, NEG_BIG)
        m = jnp.maximum(jnp.max(s_c, axis=-1, keepdims=True), jnp.max(s_n, axis=-1, keepdims=True))
        e_c = jnp.exp(s_c - m)
        e_n = jnp.exp(s_n - m)
        l = jnp.sum(e_c, axis=-1, keepdims=True) + jnp.sum(e_n, axis=-1, keepdims=True)
        acc = _mm(e_c, vc_ref[:, ln]) + _mm(e_n, vn)
        m_g = [m[g * blk:(g + 1) * blk] for g in range(3)]
        m_all = jnp.maximum(jnp.maximum(m_g[0], m_g[1]), m_g[2])
        num = 0.0
        den = 0.0
        for g in range(3):
            w = jnp.exp(m_g[g] - m_all)
            num = num + w * acc[g * blk:(g + 1) * blk]
            den = den + w * l[g * blk:(g + 1) * blk]
        res = num / den
        o_ref[:, ln] = jnp.where(m_a, res[:t_len], res[t_len:])


def attn_sample(q, kv, cache_k, cache_v, *, row0, bsz, t_len):
    kv_buf = cache_k.shape[1]
    assert kv_buf >= GROUPS[-1][0] and kv_buf % LANES == 0 and row0 % t_len == 0 and t_len % 8 == 0
    blk0 = row0 // t_len
    width = SAMPLE_PAIRS * LANES
    n_lb = D_MODEL // width
    qs = [pl.BlockSpec((t_len, width), lambda b, p, g=g: (blk0 + b, g * n_lb + p)) for g in range(3)]
    cache = pl.BlockSpec((None, kv_buf, width), lambda b, p: (b, 0, p))
    k_new = pl.BlockSpec((t_len, width), lambda b, p: (blk0 + b, p))
    v_new = pl.BlockSpec((t_len, width), lambda b, p: (blk0 + b, n_lb + p))
    return pl.pallas_call(
        functools.partial(_attn_sample_kernel, t_len=t_len, kv_buf=kv_buf, n_pairs=SAMPLE_PAIRS),
        grid=(bsz, n_lb),
        in_specs=qs + [cache, k_new, cache, v_new],
        out_specs=pl.BlockSpec((t_len, width), lambda b, p: (b, p)),
        out_shape=jax.ShapeDtypeStruct((bsz * t_len, D_MODEL), F32),
        compiler_params=_cparams("parallel", "parallel"),
        name="attn_sample",
    )(q, q, q, cache_k, kv, cache_v, kv)


def _pad_cols(w):
    return jnp.pad(w, ((0, 0), (0, LORA_PAD - w.shape[1]))).astype(BF16)


def _pad_rows(w):
    return jnp.pad(w, ((0, LORA_PAD - w.shape[0]), (0, 0))).astype(BF16)


def kernel(x_prompt, x_sample, state_wkv, state_shift, cache_k, cache_v, ln_g, ln_b, rw_mu, rw_wr, rw_wk, rw_wv,
           rw_wo, rw_w0, rw_w1, rw_w2, rw_a0, rw_a1, rw_a2, rw_v0, rw_v1, rw_v2, rw_g1, rw_g2, rw_kk, rw_ka,
           rw_rk, rw_gn_g, rw_gn_b, kv_w, att_wq, att_wo, moe_wr, moe_br, moe_win, moe_bin, moe_wout, moe_bout):
    bp, seq_len, d = x_prompt.shape
    bs, dec_len, _ = x_sample.shape
    kv_buf = cache_k.shape[1]
    mp = bp * seq_len
    ms = bs * dec_len
    m_rows = mp + ms
    tm = TOKEN_TILE
    assert d == D_MODEL and mp % tm == 0 and ms % tm == 0
    dense_tile = DENSE_TILE if m_rows % DENSE_TILE == 0 else tm
    t_block = min(SCAN_T_BLOCK, seq_len)

    x = jnp.concatenate([x_prompt.reshape(mp, d), x_sample.reshape(ms, d)], axis=0)
    wkv_p, wkv_s, shift_p, shift_s = [], [], [], []
    v_first = None
    kv = None

    def moe(layer, x1, gate_l, idx_l):
        return moe_layer(x1, gate_l, idx_l, moe_win, moe_bin[layer], moe_wout, moe_bout[layer],
                         jnp.stack([ln_g[layer, 1], ln_b[layer, 1]]), tm=tm, layer=layer)

    def post(layer, y, x_in, wo):
        wr = moe_wr[layer]
        wr_hi = wr.astype(BF16)
        wr_lo = (wr - wr_hi.astype(F32)).astype(BF16)
        padc = lambda w: jnp.pad(w, ((0, 0), (0, LANES - N_EXPERTS)))
        br = jnp.pad(moe_br[layer], (0, LANES - N_EXPERTS)).reshape(1, LANES)
        return post_mix(y, x_in, wo.astype(BF16), jnp.stack([ln_g[layer, 0], ln_b[layer, 0]]),
                        padc(wr_hi), padc(wr_lo), br, tm=tm)

    for layer in range(DEPTH):
        if layer < N_A_LAYERS:
            i = layer
            xp3 = x[:mp].reshape(bp, seq_len, d)
            xs3 = x[mp:].reshape(bs, dec_len, d)
            shift_p.append(xp3[:, -1])
            shift_s.append(xs3[:, -1])
            prev_p = jnp.concatenate([jnp.zeros((bp, 1, d), F32), xp3[:, :-1]], axis=1)
            prev_s = jnp.concatenate([state_shift[i][:, None, :], xs3[:, :-1]], axis=1)
            x_prev = jnp.concatenate([prev_p.reshape(mp, d), prev_s.reshape(ms, d)], axis=0)
            vec = jnp.stack([rw_w0[i], rw_a0[i], rw_v0[i - 1] if i > 0 else jnp.zeros((d,), F32)])
            mats = [rw_wr[i].astype(BF16), rw_wk[i].astype(BF16), rw_wv[i].astype(BF16),
                    _pad_cols(rw_w1[i]), _pad_rows(rw_w2[i]), _pad_cols(rw_a1[i]), _pad_rows(rw_a2[i]),
                    _pad_cols(rw_g1[i]), _pad_rows(rw_g2[i])]
            if i > 0:
                mats += [_pad_cols(rw_v1[i - 1]), _pad_rows(rw_v2[i - 1])]
            r, k, v, a, ld, g = a_proj(x, x_prev, v_first, rw_mu[i], vec, mats, tm=tm)
            if i == 0:
                v_first = v
            prm = jnp.stack([rw_kk[i], rw_ka[i], rw_rk[i].reshape(d), rw_gn_g[i], rw_gn_b[i]])
            seqs = (r, k, v, a, ld, g)
            y, sp = wkv_scan(seqs, prm, jnp.zeros((bp, PAIRS, LANES, LANES), F32), row0=0, t_len=seq_len,
                             chunk=SCAN_CHUNK, t_block=t_block)
            y, ss = wkv_scan(seqs, prm, pair_states(state_wkv[i]), row0=mp, t_len=dec_len,
                             chunk=dec_len, t_block=dec_len, out=y)
            wkv_p.append(unpair_states(sp))
            wkv_s.append(unpair_states(ss))
            x1, gate_l, idx_l = post(layer, y, x, rw_wo[i])
        else:
            j = layer - N_A_LAYERS
            q = dense(x, att_wq[j].astype(BF16), tm=dense_tile, tn=D_MODEL)
            parts = []
            for gi, (window, dil) in enumerate(GROUPS):
                assert window // dil == ATT_STEPS
                parts += list(attn_prompt_group(q, kv, group=gi, bsz=bp, seq_len=seq_len, dil=dil))
            y_s = attn_sample(q, kv, cache_k.reshape(bs, kv_buf, d), cache_v.reshape(bs, kv_buf, d),
                              row0=mp, bsz=bs, t_len=dec_len)
            x1, gate_l, idx_l = post(layer, (tuple(parts), y_s), x, att_wo[j])
        x = moe(layer, x1, gate_l, idx_l)
        if layer == N_A_LAYERS - 1:
            kv = dense(x, kv_w.astype(BF16), tm=dense_tile, tn=D_MODEL)

    buf_p = min(GROUPS[-1][0], seq_len)
    heads = lambda t, n: t.reshape(t.shape[0], n, N_HEADS, HEAD)
    tails = [kv[(b + 1) * seq_len - buf_p:(b + 1) * seq_len] for b in range(bp)]
    kv_s = kv[mp:].reshape(bs, dec_len, 2 * d)
    k_p_out = heads(jnp.stack([t[:, :d] for t in tails]), buf_p)
    v_p_out = heads(jnp.stack([t[:, d:] for t in tails]), buf_p)
    k_s_out = jnp.concatenate([cache_k, heads(kv_s[:, :, :d], dec_len)], axis=1)[:, -kv_buf:]
    v_s_out = jnp.concatenate([cache_v, heads(kv_s[:, :, d:], dec_len)], axis=1)[:, -kv_buf:]
    return (x[:mp].reshape(bp, seq_len, d), x[mp:].reshape(bs, dec_len, d),
            jnp.stack(wkv_p), jnp.stack(shift_p), k_p_out, v_p_out,
            jnp.stack(wkv_s), jnp.stack(shift_s), k_s_out, v_s_out)
```

```python
import functools

import jax
import jax.numpy as jnp
from jax import lax
from jax.experimental import pallas as pl
from jax.experimental.pallas import tpu as pltpu

F32 = jnp.float32
BF16 = jnp.bfloat16

D_MODEL = 1024
HEAD = 64
N_HEADS = D_MODEL // HEAD
LANES = 128
PAIRS = D_MODEL // LANES
DEPTH = 4
N_A_LAYERS = DEPTH // 2
LORA_PAD = 128
GN_EPS = 64e-5
LN_EPS = 1e-5
DN_ALPHA = (2 * DEPTH) ** 0.25
GROUPS = ((128, 1), (512, 4), (2048, 16))
ATT_STEPS = 128
N_EXPERTS = 32
TOP_K = 4
SWIGLU_LIMIT = 7.0
SWIGLU_ALPHA = 1.702
MOE_ROWS = 256
TOKEN_TILE = 256
DENSE_TILE = 1280
SCAN_CHUNK = 64
INV_BASE = 8
CAST_SLAB = 128
SCAN_T_BLOCK = 512
NEG_BIG = -1e30
VMEM_LIMIT = 56 * 1024 * 1024


def _cparams(*sem):
    return pltpu.CompilerParams(dimension_semantics=sem, vmem_limit_bytes=VMEM_LIMIT)


def _mm(a, b):
    return jnp.dot(a.astype(BF16), b.astype(BF16), preferred_element_type=F32)


def _mm_nt(a, b):
    return lax.dot_general(a.astype(BF16), b.astype(BF16), (((1,), (1,)), ((), ())),
                           preferred_element_type=F32)


def _mm_tn(a, b):
    return lax.dot_general(a.astype(BF16), b.astype(BF16), (((0,), (0,)), ((), ())),
                           preferred_element_type=F32)


def _split(x, parts):
    out = []
    for _ in range(parts):
        h = x.astype(BF16)
        out.append(h)
        x = x - h.astype(F32)
    return out


def _mm_sel_r(x, sel, parts=2):
    acc = None
    for h in _split(x, parts):
        t = jnp.dot(h, sel, preferred_element_type=F32)
        acc = t if acc is None else acc + t
    return acc


def _mm_sel_l(sel, x, parts=3):
    acc = None
    for h in _split(x, parts):
        t = jnp.dot(sel, h, preferred_element_type=F32)
        acc = t if acc is None else acc + t
    return acc


def _sigmoid(x):
    return 1.0 / (1.0 + jnp.exp(-x))


def _layer_norm(x, g, b):
    mu = jnp.mean(x, axis=-1, keepdims=True)
    xc = x - mu
    var = jnp.mean(xc * xc, axis=-1, keepdims=True)
    return xc * lax.rsqrt(var + LN_EPS) * g + b


def _each(fn, *lists):
    return [fn(*xs) for xs in zip(*lists)]


def _unit_lower_inverse(a_side, cst, chunk):
    eye, row, scol, bd = cst
    base = INV_BASE
    blk = (row // base) == (scol // base)

    def mul(xs, ys):
        return _each(_mm, xs, [bd(y) for y in ys])

    x = [jnp.where(blk, -a, 0.0) for a in a_side]
    x2 = mul(x, x)
    x4 = mul(x2, x2)
    xx2 = mul(x, x2)
    y = _each(lambda x_, x2_, xx2_: eye + x_ + x2_ + xx2_, x, x2, xx2)
    t = _each(jnp.add, y, mul(y, x4))
    s = base
    while s < chunk:
        rb = row // s
        off = (rb == (scol // s) + 1) & ((rb % 2) == 1)
        a_off = [jnp.where(off, a, 0.0) for a in a_side]
        t = _each(jnp.subtract, t, mul(t, mul(a_off, t)))
        s *= 2
    return t


def _wkv_chunk(s_mat, r, kr, v, a, ld, g, prm, cst, chunk):
    kk_p, ka_p, rk_p, gng, gnb = prm
    m_a, e_seg, e_seg2, tri, eye, row, scol, strict, incl, bd = cst
    c = chunk
    kkr = _each(jnp.multiply, kr, kk_p)
    ss = [_mm_sel_r(x * x, e_seg) for x in kkr]
    kk = _each(lambda x, s_: x / jnp.maximum(jnp.sqrt(s_), 1e-12), kkr, ss)
    k = _each(lambda kr_, a_, ka_: kr_ * (1.0 + (a_ - 1.0) * ka_), kr, a, ka_p)
    b = _each(jnp.multiply, kk, a)
    cl = [_mm_sel_l(tri, x) for x in ld]
    cl_end = [x[c - 1:c, :] for x in cl]

    def stack(x):
        return jnp.concatenate([jnp.where(m_a, x, 0.0), jnp.where(m_a, 0.0, x)], axis=0)

    kkg = _each(lambda kk_, cl_, ld_: kk_ * jnp.exp(cl_ - ld_), kk, cl, ld)
    rg = _each(lambda r_, cl_: r_ * jnp.exp(cl_), r, cl)
    g_inv = [jnp.exp(-x) for x in cl]
    bd_s = _each(lambda b_, gi: stack(b_ * gi), b, g_inv)
    kd_s = _each(lambda k_, gi: stack(k_ * gi), k, g_inv)
    g_end = _each(lambda ce, cl_: jnp.exp(ce - cl_), cl_end, cl)
    be_s = _each(lambda b_, ge: stack(b_ * ge), b, g_end)
    ke_s = _each(lambda k_, ge: stack(k_ * ge), k, g_end)
    v_s = [stack(x) for x in v]

    if (2 * c) % LANES == 0:
        bk_s = _each(lambda x_, y_: jnp.concatenate([x_, y_], axis=0), bd_s, kd_s)
        ab = _each(_mm_nt, kkg, bk_s)
        rbk = _each(_mm_nt, rg, bk_s)
        a_side, b_side = [x[:, :2 * c] for x in ab], [x[:, 2 * c:] for x in ab]
        rb_side, rk_side = [x[:, :2 * c] for x in rbk], [x[:, 2 * c:] for x in rbk]
    else:
        a_side, b_side = _each(_mm_nt, kkg, bd_s), _each(_mm_nt, kkg, kd_s)
        rb_side, rk_side = _each(_mm_nt, rg, bd_s), _each(_mm_nt, rg, kd_s)
    a_side = [jnp.where(strict, x, 0.0) for x in a_side]
    b_side = [jnp.where(strict, x, 0.0) for x in b_side]
    rb_side = [jnp.where(incl, x, 0.0) for x in rb_side]
    rk_side = [jnp.where(incl, x, 0.0) for x in rk_side]
    t_side = _unit_lower_inverse(a_side, (eye, row, scol, bd), c)

    bv = _each(_mm, b_side, v_s)
    gu = _each(lambda t_, kkg_, bv_: _mm(t_, jnp.concatenate([stack(kkg_), stack(bv_)], axis=1)), t_side, kkg, bv)
    g_s = [stack(x[:, :LANES]) for x in gu]
    u1_s = [stack(x[:, LANES:]) for x in gu]
    pq = _each(lambda rb_, gs_, u1_: _mm(rb_, jnp.concatenate([gs_, u1_], axis=1)), rb_side, g_s, u1_s)
    rkv = _each(_mm, rk_side, v_s)
    p = _each(lambda rg_, pq_: rg_ - pq_[:, :LANES], rg, pq)
    q = _each(lambda rkv_, pq_: rkv_ - pq_[:, LANES:], rkv, pq)
    o = _each(lambda p_, s_, q_: _mm_nt(p_, s_) + q_, p, s_mat, q)
    gb = _each(_mm_tn, g_s, be_s)
    nt = _each(lambda vs_, u1_, ke_, be_: _mm_tn(jnp.concatenate([vs_, -u1_], axis=0),
                                                   jnp.concatenate([ke_, be_], axis=0)), v_s, u1_s, ke_s, be_s)
    s_new = _each(lambda s_, ce, gb_, nt_: s_ * jnp.exp(ce) - _mm(s_, gb_) + nt_, s_mat, cl_end, gb, nt)

    mb = _each(lambda o_, r_, k_, rk_: _mm_sel_r(jnp.concatenate([o_, r_ * k_ * rk_], axis=1), e_seg2),
               o, r, k, rk_p)
    d = _each(lambda o_, mb_: o_ - mb_[:, :LANES] * (1.0 / HEAD), o, mb)
    var = [_mm_sel_r(x * x, e_seg) * (1.0 / HEAD) for x in d]
    out = _each(lambda d_, var_, gg, gb_, mb_, v_, g_:
                (d_ * lax.rsqrt(var_ + GN_EPS) * gg + gb_ + mb_[:, LANES:] * v_) * g_,
                d, var, gng, gnb, mb, v, g)
    return s_new, out


def _wkv_kernel(r_ref, k_ref, v_ref, a_ref, ld_ref, g_ref, prm_ref, s0_ref, *rest, chunk, n_chunks, n_pairs):
    o_ref, s_out_ref, s_scr = rest[-3:]
    tb = pl.program_id(2)

    @pl.when(tb == 0)
    def _():
        s_scr[...] = s0_ref[0]

    c2 = 2 * chunk
    lane = lax.broadcasted_iota(jnp.int32, (1, LANES), 1)
    m_a = lane < HEAD
    er = lax.broadcasted_iota(jnp.int32, (2 * LANES, 2 * LANES), 0)
    ec = lax.broadcasted_iota(jnp.int32, (2 * LANES, 2 * LANES), 1)
    e_seg2 = ((er // HEAD) == (ec // HEAD)).astype(BF16)
    e_seg = e_seg2[:LANES, :LANES]
    tr = lax.broadcasted_iota(jnp.int32, (chunk, chunk), 0)
    tc = lax.broadcasted_iota(jnp.int32, (chunk, chunk), 1)
    tri = (tr >= tc).astype(BF16)
    row = lax.broadcasted_iota(jnp.int32, (chunk, c2), 0)
    col = lax.broadcasted_iota(jnp.int32, (chunk, c2), 1)
    scol = col % chunk
    strict = scol < row
    incl = scol <= row
    eye = (row == scol).astype(F32)
    left = col < chunk

    def bd(x):
        return jnp.concatenate([jnp.where(left, x, 0.0), jnp.where(left, 0.0, x)], axis=0)

    cst = (m_a, e_seg, e_seg2, tri, eye, row, scol, strict, incl, bd)

    def body(ci, carry):
        sl = pl.ds(pl.multiple_of(ci * chunk, chunk), chunk)
        lanes = [slice(p * LANES, (p + 1) * LANES) for p in range(n_pairs)]
        prm = tuple([prm_ref[i:i + 1, ln] for ln in lanes] for i in range(5))
        seqs = [[ref[sl, ln] for ln in lanes] for ref in (r_ref, k_ref, v_ref, a_ref, ld_ref, g_ref)]
        s_new, out = _wkv_chunk([s_scr[p] for p in range(n_pairs)], *seqs, prm, cst, chunk)
        for p in range(n_pairs):
            s_scr[p] = s_new[p]
            o_ref[sl, lanes[p]] = out[p]
        return carry

    lax.fori_loop(0, n_chunks, body, 0, unroll=min(2, n_chunks))

    @pl.when(tb == pl.num_programs(2) - 1)
    def _():
        s_out_ref[0] = s_scr[...]


def wkv_scan(seqs, prm, s0, *, row0, t_len, chunk, t_block, n_pairs=PAIRS, out=None):
    m_rows, d = seqs[0].shape
    bsz = s0.shape[0]
    assert d == D_MODEL and t_len % t_block == 0 and t_block % chunk == 0 and row0 % t_block == 0
    assert PAIRS % n_pairs == 0 and row0 + bsz * t_len <= m_rows
    width = n_pairs * LANES
    nt = t_len // t_block
    blk0 = row0 // t_block
    seq = pl.BlockSpec((t_block, width), lambda b, p, t: (blk0 + b * nt + t, p))
    st = pl.BlockSpec((1, n_pairs, LANES, LANES), lambda b, p, t: (b, p, 0, 0))
    in_specs = [seq] * 6 + [pl.BlockSpec((5, width), lambda b, p, t: (0, p)), st]
    args = list(seqs) + [prm, s0]
    aliases = {}
    if out is not None:
        in_specs.append(pl.BlockSpec(memory_space=pl.ANY))
        args.append(out)
        aliases = {len(args) - 1: 0}
    return pl.pallas_call(
        functools.partial(_wkv_kernel, chunk=chunk, n_chunks=t_block // chunk, n_pairs=n_pairs),
        grid=(bsz, PAIRS // n_pairs, nt),
        in_specs=in_specs,
        out_specs=[seq, st],
        out_shape=[jax.ShapeDtypeStruct((m_rows, d), F32),
                   jax.ShapeDtypeStruct((bsz, PAIRS, LANES, LANES), F32)],
        scratch_shapes=[pltpu.VMEM((n_pairs, LANES, LANES), F32)],
        input_output_aliases=aliases,
        compiler_params=_cparams("parallel", "parallel", "arbitrary"),
        name="wkv_scan",
    )(*args)


def pair_states(s):
    bsz = s.shape[0]
    s = s.reshape(bsz, PAIRS, 2, HEAD, HEAD)
    z = jnp.zeros_like(s[:, :, 0])
    top = jnp.concatenate([s[:, :, 0], z], axis=-1)
    bot = jnp.concatenate([z, s[:, :, 1]], axis=-1)
    return jnp.concatenate([top, bot], axis=-2)


def unpair_states(sp):
    bsz = sp.shape[0]
    s = jnp.stack([sp[:, :, :HEAD, :HEAD], sp[:, :, HEAD:, HEAD:]], axis=2)
    return s.reshape(bsz, N_HEADS, HEAD, HEAD)


def _a_proj_kernel(*refs, has_vres):
    if has_vres:
        (x_ref, xp_ref, vf_ref, mu_ref, vec_ref, wr, wk, wv, w1, w2, a1, a2, g1, g2, v1, v2,
         r_o, k_o, v_o, a_o, ld_o, g_o) = refs
    else:
        (x_ref, xp_ref, mu_ref, vec_ref, wr, wk, wv, w1, w2, a1, a2, g1, g2,
         r_o, k_o, v_o, a_o, ld_o, g_o) = refs
    x = x_ref[...]
    xx = xp_ref[...] - x
    xr, xw, xk, xv, xa, xg = [(x + xx * mu_ref[i:i + 1, :]).astype(BF16) for i in range(6)]
    r_o[...] = _mm(xr, wr[...])
    k_o[...] = _mm(xk, wk[...])
    v = _mm(xv, wv[...])
    z = vec_ref[0:1, :] + _mm(jnp.tanh(_mm(xw, w1[...])), w2[...])
    softplus_neg = jnp.maximum(-z, 0.0) + jnp.log(1.0 + jnp.exp(-jnp.abs(z)))
    ld_o[...] = -jnp.exp(-softplus_neg - 0.5)
    if has_vres:
        mix = _sigmoid(vec_ref[2:3, :] + _mm(_mm(xv, v1[...]), v2[...]))
        v = v + (vf_ref[...] - v) * mix
    v_o[...] = v
    a_o[...] = _sigmoid(vec_ref[1:2, :] + _mm(_mm(xa, a1[...]), a2[...]))
    g_o[...] = _mm(_sigmoid(_mm(xg, g1[...])), g2[...])


def a_proj(x, x_prev, v_first, mu, vec, mats, *, tm):
    m_rows = x.shape[0]
    assert m_rows % tm == 0
    tok = pl.BlockSpec((tm, D_MODEL), lambda i: (i, 0))
    full = lambda a: pl.BlockSpec(a.shape, lambda i: (0, 0))
    has_vres = v_first is not None
    acts = [x, x_prev] + ([v_first] if has_vres else [])
    consts = [mu, vec] + list(mats)
    return pl.pallas_call(
        functools.partial(_a_proj_kernel, has_vres=has_vres),
        grid=(m_rows // tm,),
        in_specs=[tok] * len(acts) + [full(c) for c in consts],
        out_specs=[tok] * 6,
        out_shape=[jax.ShapeDtypeStruct((m_rows, D_MODEL), F32)] * 6,
        compiler_params=_cparams("parallel"),
        name="a_proj",
    )(*acts, *consts)


def _post_kernel(*refs, n_prompt_tiles):
    if n_prompt_tiles is None:
        y_ref = refs[0]
        y_in = y_ref[...]
        rest = refs[1:]
    else:
        o0, l0, o1, l1, o2, l2, ys_ref = refs[:7]
        rest = refs[7:]
        m = jnp.maximum(jnp.maximum(l0[...], l1[...]), l2[...])
        w0 = jnp.exp(l0[...] - m)
        w1 = jnp.exp(l1[...] - m)
        w2 = jnp.exp(l2[...] - m)
        merged = (w0 * o0[...] + w1 * o1[...] + w2 * o2[...]) / (w0 + w1 + w2)
        y_in = jnp.where(pl.program_id(0) < n_prompt_tiles, merged, ys_ref[...])
    x_ref, wo_ref, ln_ref, wrh_ref, wrl_ref, br_ref, x1_ref, gate_ref, idx_ref = rest
    y = _mm(y_in, wo_ref[...])
    x1 = _layer_norm(DN_ALPHA * x_ref[...] + y, ln_ref[0:1, :], ln_ref[1:2, :])
    x1_ref[...] = x1
    parts = _split(x1, 3)
    acc = br_ref[...]
    for h in parts:
        acc = acc + jnp.dot(h, wrh_ref[...], preferred_element_type=F32)
    for h in parts[:2]:
        acc = acc + jnp.dot(h, wrl_ref[...], preferred_element_type=F32)
    lane = lax.broadcasted_iota(jnp.int32, acc.shape, 1)
    lane_f = lane.astype(F32)
    lg = jnp.where(lane < N_EXPERTS, acc, -jnp.inf)
    vals = []
    idx_out = jnp.zeros(acc.shape, F32)
    for k in range(TOP_K):
        v = jnp.max(lg, axis=-1, keepdims=True)
        idx = jnp.min(jnp.where(lg == v, lane_f, float(LANES)), axis=-1, keepdims=True)
        vals.append(v)
        idx_out = jnp.where(lane == k, idx, idx_out)
        lg = jnp.where(lane_f == idx, -jnp.inf, lg)
    es = [jnp.exp(v - vals[0]) for v in vals]
    den = es[0]
    for e in es[1:]:
        den = den + e
    gate = jnp.zeros(acc.shape, F32)
    for k in range(TOP_K):
        gate = jnp.where(lane == k, es[k] / den, gate)
    gate_ref[...] = gate
    idx_ref[...] = idx_out.astype(jnp.int32)


def post_mix(y, x, wo, ln, wr_hi, wr_lo, br, *, tm):
    m_rows = x.shape[0]
    tok = pl.BlockSpec((tm, D_MODEL), lambda i: (i, 0))
    full = lambda a: pl.BlockSpec(a.shape, lambda i: (0, 0))
    if isinstance(y, tuple):
        parts, y_sample = y
        n_p = parts[0].shape[0] // tm
        assert parts[0].shape[0] % tm == 0 and y_sample.shape[0] % tm == 0
        prompt = pl.BlockSpec((tm, D_MODEL), lambda i: (jnp.minimum(i, n_p - 1), 0))
        sample = pl.BlockSpec((tm, D_MODEL), lambda i: (jnp.maximum(i - n_p, 0), 0))
        y_args, y_specs = list(parts) + [y_sample], [prompt] * 6 + [sample]
    else:
        n_p = None
        y_args, y_specs = [y], [tok]
    return pl.pallas_call(
        functools.partial(_post_kernel, n_prompt_tiles=n_p),
        grid=(m_rows // tm,),
        in_specs=y_specs + [tok] + [full(c) for c in (wo, ln, wr_hi, wr_lo, br)],
        out_specs=[tok, pl.BlockSpec((tm, LANES), lambda i: (i, 0)), pl.BlockSpec((tm, LANES), lambda i: (i, 0))],
        out_shape=[jax.ShapeDtypeStruct((m_rows, D_MODEL), F32),
                   jax.ShapeDtypeStruct((m_rows, LANES), F32),
                   jax.ShapeDtypeStruct((m_rows, LANES), jnp.int32)],
        compiler_params=_cparams("parallel"),
        name="post_mix",
    )(*y_args, x, wo, ln, wr_hi, wr_lo, br)


def _moe_kernel(be_ref, first_ref, nact_ref, xb_ref, win_ref, bin_ref, wout_ref, bout_ref, y_ref, win_s, wout_s):
    i = pl.program_id(0)

    @pl.when(first_ref[i] == 1)
    def _():
        for j in range(D_MODEL // CAST_SLAB):
            rows = slice(j * CAST_SLAB, (j + 1) * CAST_SLAB)
            win_s[rows, :] = win_ref[0, rows, :].astype(BF16)
            wout_s[rows, :] = wout_ref[0, rows, :].astype(BF16)

    @pl.when(i < nact_ref[0])
    def _():
        h = jnp.dot(xb_ref[...].astype(BF16), win_s[...], preferred_element_type=F32) + bin_ref[0]
        h_gate = jnp.minimum(h[:, :D_MODEL], SWIGLU_LIMIT)
        h_up = jnp.clip(h[:, D_MODEL:], -SWIGLU_LIMIT, SWIGLU_LIMIT)
        act = (h_up + 1.0) * h_gate * _sigmoid(SWIGLU_ALPHA * h_gate)
        y_ref[...] = jnp.dot(act.astype(BF16), wout_s[...], preferred_element_type=F32) + bout_ref[0]


def moe_experts(block_e, first, n_act, xb, w_in, b_in, w_out, b_out, *, layer):
    rows = xb.shape[0]
    n_blocks = rows // MOE_ROWS
    grid_spec = pltpu.PrefetchScalarGridSpec(
        num_scalar_prefetch=3,
        grid=(n_blocks,),
        in_specs=[
            pl.BlockSpec((MOE_ROWS, D_MODEL), lambda i, be, fi, na: (i, 0)),
            pl.BlockSpec((None, 1, D_MODEL, 2 * D_MODEL), lambda i, be, fi, na: (layer, be[i], 0, 0)),
            pl.BlockSpec((1, 1, 2 * D_MODEL), lambda i, be, fi, na: (be[i], 0, 0)),
            pl.BlockSpec((None, 1, D_MODEL, D_MODEL), lambda i, be, fi, na: (layer, be[i], 0, 0)),
            pl.BlockSpec((1, 1, D_MODEL), lambda i, be, fi, na: (be[i], 0, 0)),
        ],
        out_specs=pl.BlockSpec((MOE_ROWS, D_MODEL), lambda i, be, fi, na: (i, 0)),
        scratch_shapes=[pltpu.VMEM((D_MODEL, 2 * D_MODEL), BF16), pltpu.VMEM((D_MODEL, D_MODEL), BF16)],
    )
    return pl.pallas_call(
        _moe_kernel,
        grid_spec=grid_spec,
        out_shape=jax.ShapeDtypeStruct((rows, D_MODEL), F32),
        compiler_params=_cparams("arbitrary"),
        name="moe_experts",
    )(block_e, first, n_act, xb, w_in, b_in.reshape(N_EXPERTS, 1, -1), w_out, b_out.reshape(N_EXPERTS, 1, -1))


def _combine_kernel(yg_ref, gate_ref, x_ref, ln_ref, o_ref):
    gate = gate_ref[...]
    ffn = gate[:, 0:1] * yg_ref[0]
    for k in range(1, TOP_K):
        ffn = ffn + gate[:, k:k + 1] * yg_ref[k]
    o_ref[...] = _layer_norm(DN_ALPHA * x_ref[...] + ffn, ln_ref[0:1, :], ln_ref[1:2, :])


def moe_combine(yg, gate, x, ln, *, tm):
    m_rows = x.shape[0]
    tok = pl.BlockSpec((tm, D_MODEL), lambda i: (i, 0))
    return pl.pallas_call(
        _combine_kernel,
        grid=(m_rows // tm,),
        in_specs=[pl.BlockSpec((TOP_K, tm, D_MODEL), lambda i: (0, i, 0)),
                  pl.BlockSpec((tm, TOP_K), lambda i: (i, 0)), tok,
                  pl.BlockSpec(ln.shape, lambda i: (0, 0))],
        out_specs=tok,
        out_shape=jax.ShapeDtypeStruct((m_rows, D_MODEL), F32),
        compiler_params=_cparams("parallel"),
        name="moe_combine",
    )(yg, gate, x, ln)


def _route(top_e, m_rows):
    mk = m_rows * TOP_K
    flat_e = top_e.reshape(-1).astype(jnp.int32)
    onehot = (flat_e[:, None] == jnp.arange(N_EXPERTS, dtype=jnp.int32)[None, :]).astype(jnp.int32)
    csum = jnp.cumsum(onehot, axis=0)
    counts = csum[-1]
    rank = jnp.sum((csum - onehot) * onehot, axis=1)
    padded = (counts + MOE_ROWS - 1) // MOE_ROWS * MOE_ROWS
    pad_end = jnp.cumsum(padded)
    pad_start = pad_end - padded
    start = jnp.cumsum(counts) - counts
    pos = jnp.sum(onehot * pad_start[None, :], axis=1) + rank
    order = jnp.argsort(flat_e).astype(jnp.int32)
    n_blocks = -(-mk // MOE_ROWS) + N_EXPERTS
    blk_start = jnp.arange(n_blocks, dtype=jnp.int32) * MOE_ROWS
    block_e = jnp.sum((pad_end[None, :] <= blk_start[:, None]).astype(jnp.int32), axis=1)
    block_e = jnp.minimum(block_e, N_EXPERTS - 1)
    first = ((blk_start == pad_start[block_e]) & (blk_start < pad_end[-1])).astype(jnp.int32)
    n_act = (pad_end[-1:] // MOE_ROWS).astype(jnp.int32)
    e_row = jnp.repeat(block_e, MOE_ROWS)
    j_row = jnp.arange(n_blocks * MOE_ROWS, dtype=jnp.int32) - pad_start[e_row]
    compact = jnp.clip(start[e_row] + j_row, 0, mk - 1)
    row_tok = jnp.where(j_row < counts[e_row], order[compact] // TOP_K, 0)
    return row_tok, pos, block_e, first, n_act


def moe_layer(x1, gate_l, idx_l, w_in, b_in, w_out, b_out, ln, *, tm, layer):
    m_rows = x1.shape[0]
    gate = gate_l[:, :TOP_K]
    row_tok, pos, block_e, first, n_act = _route(idx_l[:, :TOP_K], m_rows)
    xb = x1[row_tok]
    yb = moe_experts(block_e, first, n_act, xb, w_in, b_in, w_out, b_out, layer=layer)
    yg = yb[pos.reshape(m_rows, TOP_K).T]
    return moe_combine(yg, gate, x1, ln, tm=tm)


def _dense_kernel(x_ref, w_ref, o_ref):
    o_ref[...] = _mm(x_ref[...], w_ref[...]).astype(o_ref.dtype)


def dense(x, w, *, tm, tn, out_dtype=F32):
    m_rows, k_dim = x.shape
    n_dim = w.shape[1]
    assert m_rows % tm == 0 and n_dim % tn == 0
    return pl.pallas_call(
        _dense_kernel,
        grid=(n_dim // tn, m_rows // tm),
        in_specs=[pl.BlockSpec((tm, k_dim), lambda j, i: (i, 0)),
                  pl.BlockSpec((k_dim, tn), lambda j, i: (0, j))],
        out_specs=pl.BlockSpec((tm, tn), lambda j, i: (i, j)),
        out_shape=jax.ShapeDtypeStruct((m_rows, n_dim), out_dtype),
        compiler_params=_cparams("parallel", "parallel"),
        name="dense",
    )(x, w)


def _slope(head):
    return 2.0 ** (-8.0 * (head + 1) / N_HEADS)


def _attn_prompt_kernel(slope_ref, q_ref, kp_ref, kc_ref, vp_ref, vc_ref, o_ref, lse_ref, bias_scr, *, dil, n_pairs):
    lb = pl.program_id(1)
    n = pl.program_id(2)
    nk = ATT_STEPS
    lane = lax.broadcasted_iota(jnp.int32, (1, LANES), 1)
    m_a = lane < HEAD

    @pl.when(n == 0)
    def _():
        qi = lax.broadcasted_iota(jnp.int32, (nk, 2 * nk), 0)
        kj = lax.broadcasted_iota(jnp.int32, (nk, 2 * nk), 1)
        delta = qi + nk - kj
        valid = (delta >= 0) & (delta <= nk)
        dist = (delta * dil).astype(F32)
        for h in range(2 * n_pairs):
            bias = jnp.where(valid, -slope_ref[lb * 2 * n_pairs + h] * dist, NEG_BIG)
            bias_scr[1, h] = bias
            bias_scr[0, h] = jnp.where(kj >= nk, bias, NEG_BIG)

    table = jnp.minimum(n, 1)

    def scores(item):
        rows, p = item
        ln = slice(p * LANES, (p + 1) * LANES)
        q = q_ref[rows, ln] * (HEAD ** -0.5)
        k = jnp.concatenate([kp_ref[rows, ln], kc_ref[rows, ln]], axis=0).astype(BF16)
        return [_mm_nt(jnp.where(m_a, q, 0.0), k), _mm_nt(jnp.where(m_a, 0.0, q), k)]

    def run(items):
        groups = [items[i:i + 2] for i in range(0, len(items), 2)]
        s_next = [scores(it) for it in groups[0]]
        for gi, grp in enumerate(groups):
            s_cur = s_next
            if gi + 1 < len(groups):
                s_next = [scores(it) for it in groups[gi + 1]]
            lns = [slice(p * LANES, (p + 1) * LANES) for _, p in grp]
            vs = [jnp.concatenate([vp_ref[rows, ln], vc_ref[rows, ln]], axis=0).astype(BF16)
                  for (rows, _), ln in zip(grp, lns)]
            s = [s_cur[i][hh] + bias_scr[table, 2 * p + hh] for i, (_, p) in enumerate(grp) for hh in range(2)]
            m = [jnp.max(x, axis=-1, keepdims=True) for x in s]
            e = _each(lambda x, m_: jnp.exp(x - m_), s, m)
            l = [jnp.sum(x, axis=-1, keepdims=True) for x in e]
            pv = [jnp.dot(x.astype(BF16), vs[j // 2], preferred_element_type=F32) for j, x in enumerate(e)]
            outs = _each(jnp.divide, pv, l)
            lses = _each(lambda m_, l_: m_ + jnp.log(l_), m, l)
            for i, ((rows, _), ln) in enumerate(zip(grp, lns)):
                o_ref[rows, ln] = jnp.where(m_a, outs[2 * i], outs[2 * i + 1])
                lse_ref[rows, ln] = jnp.where(m_a, lses[2 * i], lses[2 * i + 1])

    if dil == 1:
        run([(slice(None), p) for p in range(n_pairs)])
    else:
        group = min(dil, ATT_CLASS_UNROLL)

        def body(gi, carry):
            run([(pl.ds(gi * group + u, nk, stride=dil), 0) for u in range(group)])
            return carry

        lax.fori_loop(0, dil // group, body, 0)


ATT_CLASS_UNROLL = 8


def attn_prompt_group(q, kv, *, group, bsz, seq_len, dil):
    tile = ATT_STEPS * dil
    assert seq_len % tile == 0
    n_tiles = seq_len // tile
    n_pairs = PAIRS if dil == 1 else 1
    width = n_pairs * LANES
    n_lb = D_MODEL // width
    slopes = jnp.asarray([_slope(h) for h in range(N_HEADS)], F32)

    def spec(col0, back):
        return pl.BlockSpec((tile, width),
                            lambda b, lb, n, sl: (b * n_tiles + jnp.maximum(n - back, 0), col0 * n_lb + lb))

    out = pl.BlockSpec((tile, width), lambda b, lb, n, sl: (b * n_tiles + n, lb))
    grid_spec = pltpu.PrefetchScalarGridSpec(
        num_scalar_prefetch=1,
        grid=(bsz, n_lb, n_tiles),
        in_specs=[spec(group, 0), spec(0, 1), spec(0, 0), spec(1, 1), spec(1, 0)],
        out_specs=[out, out],
        scratch_shapes=[pltpu.VMEM((2, 2 * n_pairs, ATT_STEPS, 2 * ATT_STEPS), F32)],
    )
    return pl.pallas_call(
        functools.partial(_attn_prompt_kernel, dil=dil, n_pairs=n_pairs),
        grid_spec=grid_spec,
        out_shape=[jax.ShapeDtypeStruct((bsz * seq_len, D_MODEL), F32)] * 2,
        compiler_params=_cparams("parallel", "parallel", "arbitrary"),
        name="attn_prompt",
    )(slopes, q, kv, kv, kv, kv)


SAMPLE_PAIRS = 8


def _attn_sample_kernel(q0_ref, q1_ref, q2_ref, kc_ref, kn_ref, vc_ref, vn_ref, o_ref, *, t_len, kv_buf, n_pairs):
    pb = pl.program_id(1)
    lane = lax.broadcasted_iota(jnp.int32, (1, LANES), 1)
    m_a = lane < HEAD
    n_rows = 6 * t_len
    ri = lax.broadcasted_iota(jnp.int32, (n_rows, 1), 0)
    t = ri % t_len
    grp = ri // (2 * t_len)
    hh = (ri // t_len) % 2
    dmask = jnp.where(grp == 0, GROUPS[0][1] - 1, jnp.where(grp == 1, GROUPS[1][1] - 1, GROUPS[2][1] - 1))
    win = jnp.where(grp == 0, GROUPS[0][0], jnp.where(grp == 1, GROUPS[1][0], GROUPS[2][0]))
    jc = lax.broadcasted_iota(jnp.int32, (n_rows, kv_buf), 1)
    jn = lax.broadcasted_iota(jnp.int32, (n_rows, LANES), 1)

    def band(dist):
        return (dist >= 0) & (dist <= win) & ((dist & dmask) == 0), dist.astype(F32)

    ok_c, dist_c = band(kv_buf + t - jc)
    ok_n, dist_n = band(t - jn)
    pad = jnp.zeros((LANES - t_len, LANES), F32)
    blk = 2 * t_len

    for p in range(n_pairs):
        ln = slice(p * LANES, (p + 1) * LANES)
        lhs = []
        for q_ref in (q0_ref, q1_ref, q2_ref):
            q = q_ref[:, ln] * (HEAD ** -0.5)
            lhs += [jnp.where(m_a, q, 0.0), jnp.where(m_a, 0.0, q)]
        lhs = jnp.concatenate(lhs, axis=0)
        kn = jnp.concatenate([kn_ref[:, ln], pad], axis=0)
        vn = jnp.concatenate([vn_ref[:, ln], pad], axis=0)
        head = (2 * (pb * n_pairs + p) + hh).astype(F32)
        slope = jnp.exp2(-8.0 * (head + 1.0) / N_HEADS)
        s_c = jnp.where(ok_c, _mm_nt(lhs, kc_ref[:, ln]) - slope * dist_c, NEG_BIG)
        s_n = jnp.where(ok_n, _mm_nt(lhs, kn) - slope * dist_n, NEG_BIG)
        m = jnp.maximum(jnp.max(s_c, axis=-1, keepdims=True), jnp.max(s_n, axis=-1, keepdims=True))
        e_c = jnp.exp(s_c - m)
        e_n = jnp.exp(s_n - m)
        l = jnp.sum(e_c, axis=-1, keepdims=True) + jnp.sum(e_n, axis=-1, keepdims=True)
        acc = _mm(e_c, vc_ref[:, ln]) + _mm(e_n, vn)
        m_g = [m[g * blk:(g + 1) * blk] for g in range(3)]
        m_all = jnp.maximum(jnp.maximum(m_g[0], m_g[1]), m_g[2])
        num = 0.0
        den = 0.0
        for g in range(3):
            w = jnp.exp(m_g[g] - m_all)
            num = num + w * acc[g * blk:(g + 1) * blk]
            den = den + w * l[g * blk:(g + 1) * blk]
        res = num / den
        o_ref[:, ln] = jnp.where(m_a, res[:t_len], res[t_len:])


def attn_sample(q, kv, cache_k, cache_v, *, row0, bsz, t_len):
    kv_buf = cache_k.shape[1]
    assert kv_buf >= GROUPS[-1][0] and kv_buf % LANES == 0 and row0 % t_len == 0 and t_len % 8 == 0
    blk0 = row0 // t_len
    width = SAMPLE_PAIRS * LANES
    n_lb = D_MODEL // width
    qs = [pl.BlockSpec((t_len, width), lambda b, p, g=g: (blk0 + b, g * n_lb + p)) for g in range(3)]
    cache = pl.BlockSpec((None, kv_buf, width), lambda b, p: (b, 0, p))
    k_new = pl.BlockSpec((t_len, width), lambda b, p: (blk0 + b, p))
    v_new = pl.BlockSpec((t_len, width), lambda b, p: (blk0 + b, n_lb + p))
    return pl.pallas_call(
        functools.partial(_attn_sample_kernel, t_len=t_len, kv_buf=kv_buf, n_pairs=SAMPLE_PAIRS),
        grid=(bsz, n_lb),
        in_specs=qs + [cache, k_new, cache, v_new],
        out_specs=pl.BlockSpec((t_len, width), lambda b, p: (b, p)),
        out_shape=jax.ShapeDtypeStruct((bsz * t_len, D_MODEL), F32),
        compiler_params=_cparams("parallel", "parallel"),
        name="attn_sample",
    )(q, q, q, cache_k, kv, cache_v, kv)


def _pad_cols(w):
    return jnp.pad(w, ((0, 0), (0, LORA_PAD - w.shape[1]))).astype(BF16)


def _pad_rows(w):
    return jnp.pad(w, ((0, LORA_PAD - w.shape[0]), (0, 0))).astype(BF16)


def kernel(x_prompt, x_sample, state_wkv, state_shift, cache_k, cache_v, ln_g, ln_b, rw_mu, rw_wr, rw_wk, rw_wv,
           rw_wo, rw_w0, rw_w1, rw_w2, rw_a0, rw_a1, rw_a2, rw_v0, rw_v1, rw_v2, rw_g1, rw_g2, rw_kk, rw_ka,
           rw_rk, rw_gn_g, rw_gn_b, kv_w, att_wq, att_wo, moe_wr, moe_br, moe_win, moe_bin, moe_wout, moe_bout):
    bp, seq_len, d = x_prompt.shape
    bs, dec_len, _ = x_sample.shape
    kv_buf = cache_k.shape[1]
    mp = bp * seq_len
    ms = bs * dec_len
    m_rows = mp + ms
    tm = TOKEN_TILE
    assert d == D_MODEL and mp % tm == 0 and ms % tm == 0
    dense_tile = DENSE_TILE if m_rows % DENSE_TILE == 0 else tm
    t_block = min(SCAN_T_BLOCK, seq_len)

    x = jnp.concatenate([x_prompt.reshape(mp, d), x_sample.reshape(ms, d)], axis=0)
    wkv_p, wkv_s, shift_p, shift_s = [], [], [], []
    v_first = None
    kv = None

    def moe(layer, x1, gate_l, idx_l):
        return moe_layer(x1, gate_l, idx_l, moe_win, moe_bin[layer], moe_wout, moe_bout[layer],
                         jnp.stack([ln_g[layer, 1], ln_b[layer, 1]]), tm=tm, layer=layer)

    def post(layer, y, x_in, wo):
        wr = moe_wr[layer]
        wr_hi = wr.astype(BF16)
        wr_lo = (wr - wr_hi.astype(F32)).astype(BF16)
        padc = lambda w: jnp.pad(w, ((0, 0), (0, LANES - N_EXPERTS)))
        br = jnp.pad(moe_br[layer], (0, LANES - N_EXPERTS)).reshape(1, LANES)
        return post_mix(y, x_in, wo.astype(BF16), jnp.stack([ln_g[layer, 0], ln_b[layer, 0]]),
                        padc(wr_hi), padc(wr_lo), br, tm=tm)

    for layer in range(DEPTH):
        if layer < N_A_LAYERS:
            i = layer
            xp3 = x[:mp].reshape(bp, seq_len, d)
            xs3 = x[mp:].reshape(bs, dec_len, d)
            shift_p.append(xp3[:, -1])
            shift_s.append(xs3[:, -1])
            prev_p = jnp.concatenate([jnp.zeros((bp, 1, d), F32), xp3[:, :-1]], axis=1)
            prev_s = jnp.concatenate([state_shift[i][:, None, :], xs3[:, :-1]], axis=1)
            x_prev = jnp.concatenate([prev_p.reshape(mp, d), prev_s.reshape(ms, d)], axis=0)
            vec = jnp.stack([rw_w0[i], rw_a0[i], rw_v0[i - 1] if i > 0 else jnp.zeros((d,), F32)])
            mats = [rw_wr[i].astype(BF16), rw_wk[i].astype(BF16), rw_wv[i].astype(BF16),
                    _pad_cols(rw_w1[i]), _pad_rows(rw_w2[i]), _pad_cols(rw_a1[i]), _pad_rows(rw_a2[i]),
                    _pad_cols(rw_g1[i]), _pad_rows(rw_g2[i])]
            if i > 0:
                mats += [_pad_cols(rw_v1[i - 1]), _pad_rows(rw_v2[i - 1])]
            r, k, v, a, ld, g = a_proj(x, x_prev, v_first, rw_mu[i], vec, mats, tm=tm)
            if i == 0:
                v_first = v
            prm = jnp.stack([rw_kk[i], rw_ka[i], rw_rk[i].reshape(d), rw_gn_g[i], rw_gn_b[i]])
            seqs = (r, k, v, a, ld, g)
            y, sp = wkv_scan(seqs, prm, jnp.zeros((bp, PAIRS, LANES, LANES), F32), row0=0, t_len=seq_len,
                             chunk=SCAN_CHUNK, t_block=t_block)
            y, ss = wkv_scan(seqs, prm, pair_states(state_wkv[i]), row0=mp, t_len=dec_len,
                             chunk=dec_len, t_block=dec_len, out=y)
            wkv_p.append(unpair_states(sp))
            wkv_s.append(unpair_states(ss))
            x1, gate_l, idx_l = post(layer, y, x, rw_wo[i])
        else:
            j = layer - N_A_LAYERS
            q = dense(x, att_wq[j].astype(BF16), tm=dense_tile, tn=D_MODEL)
            parts = []
            for gi, (window, dil) in enumerate(GROUPS):
                assert window // dil == ATT_STEPS
                parts += list(attn_prompt_group(q, kv, group=gi, bsz=bp, seq_len=seq_len, dil=dil))
            y_s = attn_sample(q, kv, cache_k.reshape(bs, kv_buf, d), cache_v.reshape(bs, kv_buf, d),
                              row0=mp, bsz=bs, t_len=dec_len)
            x1, gate_l, idx_l = post(layer, (tuple(parts), y_s), x, att_wo[j])
        x = moe(layer, x1, gate_l, idx_l)
        if layer == N_A_LAYERS - 1:
            kv = dense(x, kv_w.astype(BF16), tm=dense_tile, tn=D_MODEL)

    buf_p = min(GROUPS[-1][0], seq_len)
    heads = lambda t, n: t.reshape(t.shape[0], n, N_HEADS, HEAD)
    tails = [kv[(b + 1) * seq_len - buf_p:(b + 1) * seq_len] for b in range(bp)]
    kv_s = kv[mp:].reshape(bs, dec_len, 2 * d)
    k_p_out = heads(jnp.stack([t[:, :d] for t in tails]), buf_p)
    v_p_out = heads(jnp.stack([t[:, d:] for t in tails]), buf_p)
    k_s_out = jnp.concatenate([cache_k, heads(kv_s[:, :, :d], dec_len)], axis=1)[:, -kv_buf:]
    v_s_out = jnp.concatenate([cache_v, heads(kv_s[:, :, d:], dec_len)], axis=1)[:, -kv_buf:]
    return (x[:mp].reshape(bp, seq_len, d), x[mp:].reshape(bs, dec_len, d),
            jnp.stack(wkv_p), jnp.stack(shift_p), k_p_out, v_p_out,
            jnp.stack(wkv_s), jnp.stack(shift_s), k_s_out, v_s_out)
```

```python
import functools

import jax
import jax.numpy as jnp
from jax import lax
from jax.experimental import pallas as pl
from jax.experimental.pallas import tpu as pltpu

F32 = jnp.float32
BF16 = jnp.bfloat16

D_MODEL = 1024
HEAD = 64
N_HEADS = D_MODEL // HEAD
LANES = 128
PAIRS = D_MODEL // LANES
DEPTH = 4
N_A_LAYERS = DEPTH // 2
LORA_PAD = 128
GN_EPS = 64e-5
LN_EPS = 1e-5
DN_ALPHA = (2 * DEPTH) ** 0.25
GROUPS = ((128, 1), (512, 4), (2048, 16))
ATT_STEPS = 128
N_EXPERTS = 32
TOP_K = 4
SWIGLU_LIMIT = 7.0
SWIGLU_ALPHA = 1.702
MOE_ROWS = 512
TOKEN_TILE = 256
DENSE_TILE = 1280
SCAN_CHUNK = 64
INV_BASE = 8
CAST_SLAB = 128
SCAN_T_BLOCK = 256
NEG_BIG = -1e30
VMEM_LIMIT = 56 * 1024 * 1024


def _cparams(*sem):
    return pltpu.CompilerParams(dimension_semantics=sem, vmem_limit_bytes=VMEM_LIMIT)


def _mm(a, b):
    return jnp.dot(a.astype(BF16), b.astype(BF16), preferred_element_type=F32)


def _mm_nt(a, b):
    return lax.dot_general(a.astype(BF16), b.astype(BF16), (((1,), (1,)), ((), ())),
                           preferred_element_type=F32)


def _mm_tn(a, b):
    return lax.dot_general(a.astype(BF16), b.astype(BF16), (((0,), (0,)), ((), ())),
                           preferred_element_type=F32)


def _split(x, parts):
    out = []
    for _ in range(parts):
        h = x.astype(BF16)
        out.append(h)
        x = x - h.astype(F32)
    return out


def _mm_sel_r(x, sel, parts=2):
    acc = None
    for h in _split(x, parts):
        t = jnp.dot(h, sel, preferred_element_type=F32)
        acc = t if acc is None else acc + t
    return acc


def _mm_sel_l(sel, x, parts=3):
    acc = None
    for h in _split(x, parts):
        t = jnp.dot(sel, h, preferred_element_type=F32)
        acc = t if acc is None else acc + t
    return acc


def _sigmoid(x):
    return 1.0 / (1.0 + jnp.exp(-x))


def _layer_norm(x, g, b):
    mu = jnp.mean(x, axis=-1, keepdims=True)
    xc = x - mu
    var = jnp.mean(xc * xc, axis=-1, keepdims=True)
    return xc * lax.rsqrt(var + LN_EPS) * g + b


def _each(fn, *lists):
    return [fn(*xs) for xs in zip(*lists)]


def _unit_lower_inverse(a_side, cst, chunk):
    eye, row, scol, bd = cst
    base = INV_BASE
    blk = (row // base) == (scol // base)

    def mul(xs, ys):
        return _each(_mm, xs, [bd(y) for y in ys])

    x = [jnp.where(blk, -a, 0.0) for a in a_side]
    x2 = mul(x, x)
    x4 = mul(x2, x2)
    xx2 = mul(x, x2)
    y = _each(lambda x_, x2_, xx2_: eye + x_ + x2_ + xx2_, x, x2, xx2)
    t = _each(jnp.add, y, mul(y, x4))
    s = base
    while s < chunk:
        rb = row // s
        off = (rb == (scol // s) + 1) & ((rb % 2) == 1)
        a_off = [jnp.where(off, a, 0.0) for a in a_side]
        t = _each(jnp.subtract, t, mul(t, mul(a_off, t)))
        s *= 2
    return t


def _wkv_chunk(s_mat, r, kr, v, a, ld, g, prm, cst, chunk):
    kk_p, ka_p, rk_p, gng, gnb = prm
    m_a, e_seg, e_seg2, tri, eye, row, scol, strict, incl, bd = cst
    c = chunk
    kkr = _each(jnp.multiply, kr, kk_p)
    ss = [_mm_sel_r(x * x, e_seg) for x in kkr]
    kk = _each(lambda x, s_: x / jnp.maximum(jnp.sqrt(s_), 1e-12), kkr, ss)
    k = _each(lambda kr_, a_, ka_: kr_ * (1.0 + (a_ - 1.0) * ka_), kr, a, ka_p)
    b = _each(jnp.multiply, kk, a)
    cl = [_mm_sel_l(tri, x) for x in ld]
    cl_end = [x[c - 1:c, :] for x in cl]

    def stack(x):
        return jnp.concatenate([jnp.where(m_a, x, 0.0), jnp.where(m_a, 0.0, x)], axis=0)

    kkg = _each(lambda kk_, cl_, ld_: kk_ * jnp.exp(cl_ - ld_), kk, cl, ld)
    rg = _each(lambda r_, cl_: r_ * jnp.exp(cl_), r, cl)
    g_inv = [jnp.exp(-x) for x in cl]
    bd_s = _each(lambda b_, gi: stack(b_ * gi), b, g_inv)
    kd_s = _each(lambda k_, gi: stack(k_ * gi), k, g_inv)
    g_end = _each(lambda ce, cl_: jnp.exp(ce - cl_), cl_end, cl)
    be_s = _each(lambda b_, ge: stack(b_ * ge), b, g_end)
    ke_s = _each(lambda k_, ge: stack(k_ * ge), k, g_end)
    v_s = [stack(x) for x in v]

    if (2 * c) % LANES == 0:
        bk_s = _each(lambda x_, y_: jnp.concatenate([x_, y_], axis=0), bd_s, kd_s)
        ab = _each(_mm_nt, kkg, bk_s)
        rbk = _each(_mm_nt, rg, bk_s)
        a_side, b_side = [x[:, :2 * c] for x in ab], [x[:, 2 * c:] for x in ab]
        rb_side, rk_side = [x[:, :2 * c] for x in rbk], [x[:, 2 * c:] for x in rbk]
    else:
        a_side, b_side = _each(_mm_nt, kkg, bd_s), _each(_mm_nt, kkg, kd_s)
        rb_side, rk_side = _each(_mm_nt, rg, bd_s), _each(_mm_nt, rg, kd_s)
    a_side = [jnp.where(strict, x, 0.0) for x in a_side]
    b_side = [jnp.where(strict, x, 0.0) for x in b_side]
    rb_side = [jnp.where(incl, x, 0.0) for x in rb_side]
    rk_side = [jnp.where(incl, x, 0.0) for x in rk_side]
    t_side = _unit_lower_inverse(a_side, (eye, row, scol, bd), c)

    bv = _each(_mm, b_side, v_s)
    gu = _each(lambda t_, kkg_, bv_: _mm(t_, jnp.concatenate([stack(kkg_), stack(bv_)], axis=1)), t_side, kkg, bv)
    g_s = [stack(x[:, :LANES]) for x in gu]
    u1_s = [stack(x[:, LANES:]) for x in gu]
    pq = _each(lambda rb_, gs_, u1_: _mm(rb_, jnp.concatenate([gs_, u1_], axis=1)), rb_side, g_s, u1_s)
    rkv = _each(_mm, rk_side, v_s)
    p = _each(lambda rg_, pq_: rg_ - pq_[:, :LANES], rg, pq)
    q = _each(lambda rkv_, pq_: rkv_ - pq_[:, LANES:], rkv, pq)
    o = _each(lambda p_, s_, q_: _mm_nt(p_, s_) + q_, p, s_mat, q)
    gb = _each(_mm_tn, g_s, be_s)
    nt = _each(lambda vs_, u1_, ke_, be_: _mm_tn(jnp.concatenate([vs_, -u1_], axis=0),
                                                   jnp.concatenate([ke_, be_], axis=0)), v_s, u1_s, ke_s, be_s)
    s_new = _each(lambda s_, ce, gb_, nt_: s_ * jnp.exp(ce) - _mm(s_, gb_) + nt_, s_mat, cl_end, gb, nt)

    mb = _each(lambda o_, r_, k_, rk_: _mm_sel_r(jnp.concatenate([o_, r_ * k_ * rk_], axis=1), e_seg2),
               o, r, k, rk_p)
    d = _each(lambda o_, mb_: o_ - mb_[:, :LANES] * (1.0 / HEAD), o, mb)
    var = [_mm_sel_r(x * x, e_seg) * (1.0 / HEAD) for x in d]
    out = _each(lambda d_, var_, gg, gb_, mb_, v_, g_:
                (d_ * lax.rsqrt(var_ + GN_EPS) * gg + gb_ + mb_[:, LANES:] * v_) * g_,
                d, var, gng, gnb, mb, v, g)
    return s_new, out


def _wkv_kernel(r_ref, k_ref, v_ref, a_ref, ld_ref, g_ref, prm_ref, s0_ref, *rest, chunk, n_chunks, n_pairs):
    o_ref, s_out_ref, s_scr = rest[-3:]
    tb = pl.program_id(2)

    @pl.when(tb == 0)
    def _():
        s_scr[...] = s0_ref[0]

    c2 = 2 * chunk
    lane = lax.broadcasted_iota(jnp.int32, (1, LANES), 1)
    m_a = lane < HEAD
    er = lax.broadcasted_iota(jnp.int32, (2 * LANES, 2 * LANES), 0)
    ec = lax.broadcasted_iota(jnp.int32, (2 * LANES, 2 * LANES), 1)
    e_seg2 = ((er // HEAD) == (ec // HEAD)).astype(BF16)
    e_seg = e_seg2[:LANES, :LANES]
    tr = lax.broadcasted_iota(jnp.int32, (chunk, chunk), 0)
    tc = lax.broadcasted_iota(jnp.int32, (chunk, chunk), 1)
    tri = (tr >= tc).astype(BF16)
    row = lax.broadcasted_iota(jnp.int32, (chunk, c2), 0)
    col = lax.broadcasted_iota(jnp.int32, (chunk, c2), 1)
    scol = col % chunk
    strict = scol < row
    incl = scol <= row
    eye = (row == scol).astype(F32)
    left = col < chunk

    def bd(x):
        return jnp.concatenate([jnp.where(left, x, 0.0), jnp.where(left, 0.0, x)], axis=0)

    cst = (m_a, e_seg, e_seg2, tri, eye, row, scol, strict, incl, bd)

    def body(ci, carry):
        sl = pl.ds(pl.multiple_of(ci * chunk, chunk), chunk)
        lanes = [slice(p * LANES, (p + 1) * LANES) for p in range(n_pairs)]
        prm = tuple([prm_ref[i:i + 1, ln] for ln in lanes] for i in range(5))
        seqs = [[ref[sl, ln] for ln in lanes] for ref in (r_ref, k_ref, v_ref, a_ref, ld_ref, g_ref)]
        s_new, out = _wkv_chunk([s_scr[p] for p in range(n_pairs)], *seqs, prm, cst, chunk)
        for p in range(n_pairs):
            s_scr[p] = s_new[p]
            o_ref[sl, lanes[p]] = out[p]
        return carry

    lax.fori_loop(0, n_chunks, body, 0, unroll=min(2, n_chunks))

    @pl.when(tb == pl.num_programs(2) - 1)
    def _():
        s_out_ref[0] = s_scr[...]


def wkv_scan(seqs, prm, s0, *, row0, t_len, chunk, t_block, n_pairs=PAIRS, out=None):
    m_rows, d = seqs[0].shape
    bsz = s0.shape[0]
    assert d == D_MODEL and t_len % t_block == 0 and t_block % chunk == 0 and row0 % t_block == 0
    assert PAIRS % n_pairs == 0 and row0 + bsz * t_len <= m_rows
    width = n_pairs * LANES
    nt = t_len // t_block
    blk0 = row0 // t_block
    seq = pl.BlockSpec((t_block, width), lambda b, p, t: (blk0 + b * nt + t, p))
    st = pl.BlockSpec((1, n_pairs, LANES, LANES), lambda b, p, t: (b, p, 0, 0))
    in_specs = [seq] * 6 + [pl.BlockSpec((5, width), lambda b, p, t: (0, p)), st]
    args = list(seqs) + [prm, s0]
    aliases = {}
    if out is not None:
        in_specs.append(pl.BlockSpec(memory_space=pl.ANY))
        args.append(out)
        aliases = {len(args) - 1: 0}
    return pl.pallas_call(
        functools.partial(_wkv_kernel, chunk=chunk, n_chunks=t_block // chunk, n_pairs=n_pairs),
        grid=(bsz, PAIRS // n_pairs, nt),
        in_specs=in_specs,
        out_specs=[seq, st],
        out_shape=[jax.ShapeDtypeStruct((m_rows, d), F32),
                   jax.ShapeDtypeStruct((bsz, PAIRS, LANES, LANES), F32)],
        scratch_shapes=[pltpu.VMEM((n_pairs, LANES, LANES), F32)],
        input_output_aliases=aliases,
        compiler_params=_cparams("parallel", "parallel", "arbitrary"),
        name="wkv_scan",
    )(*args)


def pair_states(s):
    bsz = s.shape[0]
    s = s.reshape(bsz, PAIRS, 2, HEAD, HEAD)
    z = jnp.zeros_like(s[:, :, 0])
    top = jnp.concatenate([s[:, :, 0], z], axis=-1)
    bot = jnp.concatenate([z, s[:, :, 1]], axis=-1)
    return jnp.concatenate([top, bot], axis=-2)


def unpair_states(sp):
    bsz = sp.shape[0]
    s = jnp.stack([sp[:, :, :HEAD, :HEAD], sp[:, :, HEAD:, HEAD:]], axis=2)
    return s.reshape(bsz, N_HEADS, HEAD, HEAD)


def _a_proj_kernel(*refs, has_vres):
    if has_vres:
        (x_ref, xp_ref, vf_ref, mu_ref, vec_ref, wr, wk, wv, w1, w2, a1, a2, g1, g2, v1, v2,
         r_o, k_o, v_o, a_o, ld_o, g_o) = refs
    else:
        (x_ref, xp_ref, mu_ref, vec_ref, wr, wk, wv, w1, w2, a1, a2, g1, g2,
         r_o, k_o, v_o, a_o, ld_o, g_o) = refs
    x = x_ref[...]
    xx = xp_ref[...] - x
    xr, xw, xk, xv, xa, xg = [(x + xx * mu_ref[i:i + 1, :]).astype(BF16) for i in range(6)]
    r_o[...] = _mm(xr, wr[...])
    k_o[...] = _mm(xk, wk[...])
    v = _mm(xv, wv[...])
    z = vec_ref[0:1, :] + _mm(jnp.tanh(_mm(xw, w1[...])), w2[...])
    softplus_neg = jnp.maximum(-z, 0.0) + jnp.log(1.0 + jnp.exp(-jnp.abs(z)))
    ld_o[...] = -jnp.exp(-softplus_neg - 0.5)
    if has_vres:
        mix = _sigmoid(vec_ref[2:3, :] + _mm(_mm(xv, v1[...]), v2[...]))
        v = v + (vf_ref[...] - v) * mix
    v_o[...] = v
    a_o[...] = _sigmoid(vec_ref[1:2, :] + _mm(_mm(xa, a1[...]), a2[...]))
    g_o[...] = _mm(_sigmoid(_mm(xg, g1[...])), g2[...])


def a_proj(x, x_prev, v_first, mu, vec, mats, *, tm):
    m_rows = x.shape[0]
    assert m_rows % tm == 0
    tok = pl.BlockSpec((tm, D_MODEL), lambda i: (i, 0))
    full = lambda a: pl.BlockSpec(a.shape, lambda i: (0, 0))
    has_vres = v_first is not None
    acts = [x, x_prev] + ([v_first] if has_vres else [])
    consts = [mu, vec] + list(mats)
    return pl.pallas_call(
        functools.partial(_a_proj_kernel, has_vres=has_vres),
        grid=(m_rows // tm,),
        in_specs=[tok] * len(acts) + [full(c) for c in consts],
        out_specs=[tok] * 6,
        out_shape=[jax.ShapeDtypeStruct((m_rows, D_MODEL), F32)] * 6,
        compiler_params=_cparams("parallel"),
        name="a_proj",
    )(*acts, *consts)


def _post_kernel(*refs, n_prompt_tiles):
    if n_prompt_tiles is None:
        y_ref = refs[0]
        y_in = y_ref[...]
        rest = refs[1:]
    else:
        o0, l0, o1, l1, o2, l2, ys_ref = refs[:7]
        rest = refs[7:]
        m = jnp.maximum(jnp.maximum(l0[...], l1[...]), l2[...])
        w0 = jnp.exp(l0[...] - m)
        w1 = jnp.exp(l1[...] - m)
        w2 = jnp.exp(l2[...] - m)
        merged = (w0 * o0[...] + w1 * o1[...] + w2 * o2[...]) / (w0 + w1 + w2)
        y_in = jnp.where(pl.program_id(0) < n_prompt_tiles, merged, ys_ref[...])
    x_ref, wo_ref, ln_ref, wrh_ref, wrl_ref, br_ref, x1_ref, gate_ref, idx_ref = rest
    y = _mm(y_in, wo_ref[...])
    x1 = _layer_norm(DN_ALPHA * x_ref[...] + y, ln_ref[0:1, :], ln_ref[1:2, :])
    x1_ref[...] = x1
    parts = _split(x1, 3)
    acc = br_ref[...]
    for h in parts:
        acc = acc + jnp.dot(h, wrh_ref[...], preferred_element_type=F32)
    for h in parts[:2]:
        acc = acc + jnp.dot(h, wrl_ref[...], preferred_element_type=F32)
    lane = lax.broadcasted_iota(jnp.int32, acc.shape, 1)
    lane_f = lane.astype(F32)
    lg = jnp.where(lane < N_EXPERTS, acc, -jnp.inf)
    vals = []
    idx_out = jnp.zeros(acc.shape, F32)
    for k in range(TOP_K):
        v = jnp.max(lg, axis=-1, keepdims=True)
        idx = jnp.min(jnp.where(lg == v, lane_f, float(LANES)), axis=-1, keepdims=True)
        vals.append(v)
        idx_out = jnp.where(lane == k, idx, idx_out)
        lg = jnp.where(lane_f == idx, -jnp.inf, lg)
    es = [jnp.exp(v - vals[0]) for v in vals]
    den = es[0]
    for e in es[1:]:
        den = den + e
    gate = jnp.zeros(acc.shape, F32)
    for k in range(TOP_K):
        gate = jnp.where(lane == k, es[k] / den, gate)
    gate_ref[...] = gate
    idx_ref[...] = idx_out.astype(jnp.int32)


def post_mix(y, x, wo, ln, wr_hi, wr_lo, br, *, tm):
    m_rows = x.shape[0]
    tok = pl.BlockSpec((tm, D_MODEL), lambda i: (i, 0))
    full = lambda a: pl.BlockSpec(a.shape, lambda i: (0, 0))
    if isinstance(y, tuple):
        parts, y_sample = y
        n_p = parts[0].shape[0] // tm
        assert parts[0].shape[0] % tm == 0 and y_sample.shape[0] % tm == 0
        prompt = pl.BlockSpec((tm, D_MODEL), lambda i: (jnp.minimum(i, n_p - 1), 0))
        sample = pl.BlockSpec((tm, D_MODEL), lambda i: (jnp.maximum(i - n_p, 0), 0))
        y_args, y_specs = list(parts) + [y_sample], [prompt] * 6 + [sample]
    else:
        n_p = None
        y_args, y_specs = [y], [tok]
    return pl.pallas_call(
        functools.partial(_post_kernel, n_prompt_tiles=n_p),
        grid=(m_rows // tm,),
        in_specs=y_specs + [tok] + [full(c) for c in (wo, ln, wr_hi, wr_lo, br)],
        out_specs=[tok, pl.BlockSpec((tm, LANES), lambda i: (i, 0)), pl.BlockSpec((tm, LANES), lambda i: (i, 0))],
        out_shape=[jax.ShapeDtypeStruct((m_rows, D_MODEL), F32),
                   jax.ShapeDtypeStruct((m_rows, LANES), F32),
                   jax.ShapeDtypeStruct((m_rows, LANES), jnp.int32)],
        compiler_params=_cparams("parallel"),
        name="post_mix",
    )(*y_args, x, wo, ln, wr_hi, wr_lo, br)


def _moe_kernel(be_ref, first_ref, nact_ref, xb_ref, win_ref, bin_ref, wout_ref, bout_ref, y_ref, win_s, wout_s):
    i = pl.program_id(0)

    @pl.when(first_ref[i] == 1)
    def _():
        for j in range(D_MODEL // CAST_SLAB):
            rows = slice(j * CAST_SLAB, (j + 1) * CAST_SLAB)
            win_s[rows, :] = win_ref[0, rows, :].astype(BF16)
            wout_s[rows, :] = wout_ref[0, rows, :].astype(BF16)

    @pl.when(i < nact_ref[0])
    def _():
        h = jnp.dot(xb_ref[...].astype(BF16), win_s[...], preferred_element_type=F32) + bin_ref[0]
        h_gate = jnp.minimum(h[:, :D_MODEL], SWIGLU_LIMIT)
        h_up = jnp.clip(h[:, D_MODEL:], -SWIGLU_LIMIT, SWIGLU_LIMIT)
        act = (h_up + 1.0) * h_gate * _sigmoid(SWIGLU_ALPHA * h_gate)
        y_ref[...] = jnp.dot(act.astype(BF16), wout_s[...], preferred_element_type=F32) + bout_ref[0]


def moe_experts(block_e, first, n_act, xb, w_in, b_in, w_out, b_out, *, layer):
    rows = xb.shape[0]
    n_blocks = rows // MOE_ROWS
    grid_spec = pltpu.PrefetchScalarGridSpec(
        num_scalar_prefetch=3,
        grid=(n_blocks,),
        in_specs=[
            pl.BlockSpec((MOE_ROWS, D_MODEL), lambda i, be, fi, na: (i, 0)),
            pl.BlockSpec((None, 1, D_MODEL, 2 * D_MODEL), lambda i, be, fi, na: (layer, be[i], 0, 0)),
            pl.BlockSpec((1, 1, 2 * D_MODEL), lambda i, be, fi, na: (be[i], 0, 0)),
            pl.BlockSpec((None, 1, D_MODEL, D_MODEL), lambda i, be, fi, na: (layer, be[i], 0, 0)),
            pl.BlockSpec((1, 1, D_MODEL), lambda i, be, fi, na: (be[i], 0, 0)),
        ],
        out_specs=pl.BlockSpec((MOE_ROWS, D_MODEL), lambda i, be, fi, na: (i, 0)),
        scratch_shapes=[pltpu.VMEM((D_MODEL, 2 * D_MODEL), BF16), pltpu.VMEM((D_MODEL, D_MODEL), BF16)],
    )
    return pl.pallas_call(
        _moe_kernel,
        grid_spec=grid_spec,
        out_shape=jax.ShapeDtypeStruct((rows, D_MODEL), F32),
        compiler_params=_cparams("arbitrary"),
        name="moe_experts",
    )(block_e, first, n_act, xb, w_in, b_in.reshape(N_EXPERTS, 1, -1), w_out, b_out.reshape(N_EXPERTS, 1, -1))


def _combine_kernel(yg_ref, gate_ref, x_ref, ln_ref, o_ref):
    gate = gate_ref[...]
    ffn = gate[:, 0:1] * yg_ref[0]
    for k in range(1, TOP_K):
        ffn = ffn + gate[:, k:k + 1] * yg_ref[k]
    o_ref[...] = _layer_norm(DN_ALPHA * x_ref[...] + ffn, ln_ref[0:1, :], ln_ref[1:2, :])


def moe_combine(yg, gate, x, ln, *, tm):
    m_rows = x.shape[0]
    tok = pl.BlockSpec((tm, D_MODEL), lambda i: (i, 0))
    return pl.pallas_call(
        _combine_kernel,
        grid=(m_rows // tm,),
        in_specs=[pl.BlockSpec((TOP_K, tm, D_MODEL), lambda i: (0, i, 0)),
                  pl.BlockSpec((tm, TOP_K), lambda i: (i, 0)), tok,
                  pl.BlockSpec(ln.shape, lambda i: (0, 0))],
        out_specs=tok,
        out_shape=jax.ShapeDtypeStruct((m_rows, D_MODEL), F32),
        compiler_params=_cparams("parallel"),
        name="moe_combine",
    )(yg, gate, x, ln)


def _route(top_e, m_rows):
    mk = m_rows * TOP_K
    flat_e = top_e.reshape(-1).astype(jnp.int32)
    onehot = (flat_e[:, None] == jnp.arange(N_EXPERTS, dtype=jnp.int32)[None, :]).astype(jnp.int32)
    csum = jnp.cumsum(onehot, axis=0)
    counts = csum[-1]
    rank = jnp.sum((csum - onehot) * onehot, axis=1)
    padded = (counts + MOE_ROWS - 1) // MOE_ROWS * MOE_ROWS
    pad_end = jnp.cumsum(padded)
    pad_start = pad_end - padded
    start = jnp.cumsum(counts) - counts
    pos = jnp.sum(onehot * pad_start[None, :], axis=1) + rank
    order = jnp.argsort(flat_e).astype(jnp.int32)
    n_blocks = -(-mk // MOE_ROWS) + N_EXPERTS
    blk_start = jnp.arange(n_blocks, dtype=jnp.int32) * MOE_ROWS
    block_e = jnp.sum((pad_end[None, :] <= blk_start[:, None]).astype(jnp.int32), axis=1)
    block_e = jnp.minimum(block_e, N_EXPERTS - 1)
    first = ((blk_start == pad_start[block_e]) & (blk_start < pad_end[-1])).astype(jnp.int32)
    n_act = (pad_end[-1:] // MOE_ROWS).astype(jnp.int32)
    e_row = jnp.repeat(block_e, MOE_ROWS)
    j_row = jnp.arange(n_blocks * MOE_ROWS, dtype=jnp.int32) - pad_start[e_row]
    compact = jnp.clip(start[e_row] + j_row, 0, mk - 1)
    row_tok = jnp.where(j_row < counts[e_row], order[compact] // TOP_K, 0)
    return row_tok, pos, block_e, first, n_act


def moe_layer(x1, gate_l, idx_l, w_in, b_in, w_out, b_out, ln, *, tm, layer):
    m_rows = x1.shape[0]
    gate = gate_l[:, :TOP_K]
    row_tok, pos, block_e, first, n_act = _route(idx_l[:, :TOP_K], m_rows)
    xb = x1[row_tok]
    yb = moe_experts(block_e, first, n_act, xb, w_in, b_in, w_out, b_out, layer=layer)
    yg = yb[pos.reshape(m_rows, TOP_K).T]
    return moe_combine(yg, gate, x1, ln, tm=tm)


def _dense_kernel(x_ref, w_ref, o_ref):
    o_ref[...] = _mm(x_ref[...], w_ref[...]).astype(o_ref.dtype)


def dense(x, w, *, tm, tn, out_dtype=F32):
    m_rows, k_dim = x.shape
    n_dim = w.shape[1]
    assert m_rows % tm == 0 and n_dim % tn == 0
    return pl.pallas_call(
        _dense_kernel,
        grid=(n_dim // tn, m_rows // tm),
        in_specs=[pl.BlockSpec((tm, k_dim), lambda j, i: (i, 0)),
                  pl.BlockSpec((k_dim, tn), lambda j, i: (0, j))],
        out_specs=pl.BlockSpec((tm, tn), lambda j, i: (i, j)),
        out_shape=jax.ShapeDtypeStruct((m_rows, n_dim), out_dtype),
        compiler_params=_cparams("parallel", "parallel"),
        name="dense",
    )(x, w)


def _slope(head):
    return 2.0 ** (-8.0 * (head + 1) / N_HEADS)


def _attn_prompt_kernel(slope_ref, q_ref, kp_ref, kc_ref, vp_ref, vc_ref, o_ref, lse_ref, bias_scr, *, dil, n_pairs):
    lb = pl.program_id(1)
    n = pl.program_id(2)
    nk = ATT_STEPS
    lane = lax.broadcasted_iota(jnp.int32, (1, LANES), 1)
    m_a = lane < HEAD

    @pl.when(n == 0)
    def _():
        qi = lax.broadcasted_iota(jnp.int32, (nk, 2 * nk), 0)
        kj = lax.broadcasted_iota(jnp.int32, (nk, 2 * nk), 1)
        delta = qi + nk - kj
        valid = (delta >= 0) & (delta <= nk)
        dist = (delta * dil).astype(F32)
        for h in range(2 * n_pairs):
            bias = jnp.where(valid, -slope_ref[lb * 2 * n_pairs + h] * dist, NEG_BIG)
            bias_scr[1, h] = bias
            bias_scr[0, h] = jnp.where(kj >= nk, bias, NEG_BIG)

    table = jnp.minimum(n, 1)

    def scores(item):
        rows, p = item
        ln = slice(p * LANES, (p + 1) * LANES)
        q = q_ref[rows, ln] * (HEAD ** -0.5)
        k = jnp.concatenate([kp_ref[rows, ln], kc_ref[rows, ln]], axis=0).astype(BF16)
        return [_mm_nt(jnp.where(m_a, q, 0.0), k), _mm_nt(jnp.where(m_a, 0.0, q), k)]

    def run(items):
        groups = [items[i:i + 2] for i in range(0, len(items), 2)]
        s_next = [scores(it) for it in groups[0]]
        for gi, grp in enumerate(groups):
            s_cur = s_next
            if gi + 1 < len(groups):
                s_next = [scores(it) for it in groups[gi + 1]]
            lns = [slice(p * LANES, (p + 1) * LANES) for _, p in grp]
            vs = [jnp.concatenate([vp_ref[rows, ln], vc_ref[rows, ln]], axis=0).astype(BF16)
                  for (rows, _), ln in zip(grp, lns)]
            s = [s_cur[i][hh] + bias_scr[table, 2 * p + hh] for i, (_, p) in enumerate(grp) for hh in range(2)]
            m = [jnp.max(x, axis=-1, keepdims=True) for x in s]
            e = _each(lambda x, m_: jnp.exp(x - m_), s, m)
            l = [jnp.sum(x, axis=-1, keepdims=True) for x in e]
            pv = [jnp.dot(x.astype(BF16), vs[j // 2], preferred_element_type=F32) for j, x in enumerate(e)]
            outs = _each(jnp.divide, pv, l)
            lses = _each(lambda m_, l_: m_ + jnp.log(l_), m, l)
            for i, ((rows, _), ln) in enumerate(zip(grp, lns)):
                o_ref[rows, ln] = jnp.where(m_a, outs[2 * i], outs[2 * i + 1])
                lse_ref[rows, ln] = jnp.where(m_a, lses[2 * i], lses[2 * i + 1])

    if dil == 1:
        run([(slice(None), p) for p in range(n_pairs)])
    else:
        group = min(dil, ATT_CLASS_UNROLL)

        def body(gi, carry):
            run([(pl.ds(gi * group + u, nk, stride=dil), 0) for u in range(group)])
            return carry

        lax.fori_loop(0, dil // group, body, 0)


ATT_CLASS_UNROLL = 8


def attn_prompt_group(q, kv, *, group, bsz, seq_len, dil):
    tile = ATT_STEPS * dil
    assert seq_len % tile == 0
    n_tiles = seq_len // tile
    n_pairs = PAIRS if dil == 1 else 1
    width = n_pairs * LANES
    n_lb = D_MODEL // width
    slopes = jnp.asarray([_slope(h) for h in range(N_HEADS)], F32)

    def spec(col0, back):
        return pl.BlockSpec((tile, width),
                            lambda b, lb, n, sl: (b * n_tiles + jnp.maximum(n - back, 0), col0 * n_lb + lb))

    out = pl.BlockSpec((tile, width), lambda b, lb, n, sl: (b * n_tiles + n, lb))
    grid_spec = pltpu.PrefetchScalarGridSpec(
        num_scalar_prefetch=1,
        grid=(bsz, n_lb, n_tiles),
        in_specs=[spec(group, 0), spec(0, 1), spec(0, 0), spec(1, 1), spec(1, 0)],
        out_specs=[out, out],
        scratch_shapes=[pltpu.VMEM((2, 2 * n_pairs, ATT_STEPS, 2 * ATT_STEPS), F32)],
    )
    return pl.pallas_call(
        functools.partial(_attn_prompt_kernel, dil=dil, n_pairs=n_pairs),
        grid_spec=grid_spec,
        out_shape=[jax.ShapeDtypeStruct((bsz * seq_len, D_MODEL), F32)] * 2,
        compiler_params=_cparams("parallel", "parallel", "arbitrary"),
        name="attn_prompt",
    )(slopes, q, kv, kv, kv, kv)


SAMPLE_PAIRS = 4


def _attn_sample_kernel(q0_ref, q1_ref, q2_ref, kc_ref, kn_ref, vc_ref, vn_ref, o_ref, *, t_len, kv_buf, n_pairs):
    pb = pl.program_id(1)
    lane = lax.broadcasted_iota(jnp.int32, (1, LANES), 1)
    m_a = lane < HEAD
    n_rows = 6 * t_len
    ri = lax.broadcasted_iota(jnp.int32, (n_rows, 1), 0)
    t = ri % t_len
    grp = ri // (2 * t_len)
    hh = (ri // t_len) % 2
    dmask = jnp.where(grp == 0, GROUPS[0][1] - 1, jnp.where(grp == 1, GROUPS[1][1] - 1, GROUPS[2][1] - 1))
    win = jnp.where(grp == 0, GROUPS[0][0], jnp.where(grp == 1, GROUPS[1][0], GROUPS[2][0]))
    jc = lax.broadcasted_iota(jnp.int32, (n_rows, kv_buf), 1)
    jn = lax.broadcasted_iota(jnp.int32, (n_rows, LANES), 1)

    def band(dist):
        return (dist >= 0) & (dist <= win) & ((dist & dmask) == 0), dist.astype(F32)

    ok_c, dist_c = band(kv_buf + t - jc)
    ok_n, dist_n = band(t - jn)
    pad = jnp.zeros((LANES - t_len, LANES), F32)
    blk = 2 * t_len

    for p in range(n_pairs):
        ln = slice(p * LANES, (p + 1) * LANES)
        lhs = []
        for q_ref in (q0_ref, q1_ref, q2_ref):
            q = q_ref[:, ln] * (HEAD ** -0.5)
            lhs += [jnp.where(m_a, q, 0.0), jnp.where(m_a, 0.0, q)]
        lhs = jnp.concatenate(lhs, axis=0)
        kn = jnp.concatenate([kn_ref[:, ln], pad], axis=0)
        vn = jnp.concatenate([vn_ref[:, ln], pad], axis=0)
        head = (2 * (pb * n_pairs + p) + hh).astype(F32)
        slope = jnp.exp2(-8.0 * (head + 1.0) / N_HEADS)
        s_c = jnp.where(ok_c, _mm_nt(lhs, kc_ref[:, ln]) - slope * dist_c, NEG_BIG)
        s_n = jnp.where(ok_n, _mm_nt(lhs, kn) - slope * dist_n, NEG_BIG)
        m = jnp.maximum(jnp.max(s_c, axis=-1, keepdims=True), jnp.max(s_n, axis=-1, keepdims=True))
        e_c = jnp.exp(s_c - m)
        e_n = jnp.exp(s_n - m)
        l = jnp.sum(e_c, axis=-1, keepdims=True) + jnp.sum(e_n, axis=-1, keepdims=True)
        acc = _mm(e_c, vc_ref[:, ln]) + _mm(e_n, vn)
        m_g = [m[g * blk:(g + 1) * blk] for g in range(3)]
        m_all = jnp.maximum(jnp.maximum(m_g[0], m_g[1]), m_g[2])
        num = 0.0
        den = 0.0
        for g in range(3):
            w = jnp.exp(m_g[g] - m_all)
            num = num + w * acc[g * blk:(g + 1) * blk]
            den = den + w * l[g * blk:(g + 1) * blk]
        res = num / den
        o_ref[:, ln] = jnp.where(m_a, res[:t_len], res[t_len:])


def attn_sample(q, kv, cache_k, cache_v, *, row0, bsz, t_len):
    kv_buf = cache_k.shape[1]
    assert kv_buf >= GROUPS[-1][0] and kv_buf % LANES == 0 and row0 % t_len == 0 and t_len % 8 == 0
    blk0 = row0 // t_len
    width = SAMPLE_PAIRS * LANES
    n_lb = D_MODEL // width
    qs = [pl.BlockSpec((t_len, width), lambda b, p, g=g: (blk0 + b, g * n_lb + p)) for g in range(3)]
    cache = pl.BlockSpec((None, kv_buf, width), lambda b, p: (b, 0, p))
    k_new = pl.BlockSpec((t_len, width), lambda b, p: (blk0 + b, p))
    v_new = pl.BlockSpec((t_len, width), lambda b, p: (blk0 + b, n_lb + p))
    return pl.pallas_call(
        functools.partial(_attn_sample_kernel, t_len=t_len, kv_buf=kv_buf, n_pairs=SAMPLE_PAIRS),
        grid=(bsz, n_lb),
        in_specs=qs + [cache, k_new, cache, v_new],
        out_specs=pl.BlockSpec((t_len, width), lambda b, p: (b, p)),
        out_shape=jax.ShapeDtypeStruct((bsz * t_len, D_MODEL), F32),
        compiler_params=_cparams("parallel", "parallel"),
        name="attn_sample",
    )(q, q, q, cache_k, kv, cache_v, kv)


def _pad_cols(w):
    return jnp.pad(w, ((0, 0), (0, LORA_PAD - w.shape[1]))).astype(BF16)


def _pad_rows(w):
    return jnp.pad(w, ((0, LORA_PAD - w.shape[0]), (0, 0))).astype(BF16)


def kernel(x_prompt, x_sample, state_wkv, state_shift, cache_k, cache_v, ln_g, ln_b, rw_mu, rw_wr, rw_wk, rw_wv,
           rw_wo, rw_w0, rw_w1, rw_w2, rw_a0, rw_a1, rw_a2, rw_v0, rw_v1, rw_v2, rw_g1, rw_g2, rw_kk, rw_ka,
           rw_rk, rw_gn_g, rw_gn_b, kv_w, att_wq, att_wo, moe_wr, moe_br, moe_win, moe_bin, moe_wout, moe_bout):
    bp, seq_len, d = x_prompt.shape
    bs, dec_len, _ = x_sample.shape
    kv_buf = cache_k.shape[1]
    mp = bp * seq_len
    ms = bs * dec_len
    m_rows = mp + ms
    tm = TOKEN_TILE
    assert d == D_MODEL and mp % tm == 0 and ms % tm == 0
    dense_tile = DENSE_TILE if m_rows % DENSE_TILE == 0 else tm
    t_block = min(SCAN_T_BLOCK, seq_len)

    x = jnp.concatenate([x_prompt.reshape(mp, d), x_sample.reshape(ms, d)], axis=0)
    wkv_p, wkv_s, shift_p, shift_s = [], [], [], []
    v_first = None
    kv = None

    def moe(layer, x1, gate_l, idx_l):
        return moe_layer(x1, gate_l, idx_l, moe_win, moe_bin[layer], moe_wout, moe_bout[layer],
                         jnp.stack([ln_g[layer, 1], ln_b[layer, 1]]), tm=tm, layer=layer)

    def post(layer, y, x_in, wo):
        wr = moe_wr[layer]
        wr_hi = wr.astype(BF16)
        wr_lo = (wr - wr_hi.astype(F32)).astype(BF16)
        padc = lambda w: jnp.pad(w, ((0, 0), (0, LANES - N_EXPERTS)))
        br = jnp.pad(moe_br[layer], (0, LANES - N_EXPERTS)).reshape(1, LANES)
        return post_mix(y, x_in, wo.astype(BF16), jnp.stack([ln_g[layer, 0], ln_b[layer, 0]]),
                        padc(wr_hi), padc(wr_lo), br, tm=tm)

    for layer in range(DEPTH):
        if layer < N_A_LAYERS:
            i = layer
            xp3 = x[:mp].reshape(bp, seq_len, d)
            xs3 = x[mp:].reshape(bs, dec_len, d)
            shift_p.append(xp3[:, -1])
            shift_s.append(xs3[:, -1])
            prev_p = jnp.concatenate([jnp.zeros((bp, 1, d), F32), xp3[:, :-1]], axis=1)
            prev_s = jnp.concatenate([state_shift[i][:, None, :], xs3[:, :-1]], axis=1)
            x_prev = jnp.concatenate([prev_p.reshape(mp, d), prev_s.reshape(ms, d)], axis=0)
            vec = jnp.stack([rw_w0[i], rw_a0[i], rw_v0[i - 1] if i > 0 else jnp.zeros((d,), F32)])
            mats = [rw_wr[i].astype(BF16), rw_wk[i].astype(BF16), rw_wv[i].astype(BF16),
                    _pad_cols(rw_w1[i]), _pad_rows(rw_w2[i]), _pad_cols(rw_a1[i]), _pad_rows(rw_a2[i]),
                    _pad_cols(rw_g1[i]), _pad_rows(rw_g2[i])]
            if i > 0:
                mats += [_pad_cols(rw_v1[i - 1]), _pad_rows(rw_v2[i - 1])]
            r, k, v, a, ld, g = a_proj(x, x_prev, v_first, rw_mu[i], vec, mats, tm=tm)
            if i == 0:
                v_first = v
            prm = jnp.stack([rw_kk[i], rw_ka[i], rw_rk[i].reshape(d), rw_gn_g[i], rw_gn_b[i]])
            seqs = (r, k, v, a, ld, g)
            y, sp = wkv_scan(seqs, prm, jnp.zeros((bp, PAIRS, LANES, LANES), F32), row0=0, t_len=seq_len,
                             chunk=SCAN_CHUNK, t_block=t_block)
            y, ss = wkv_scan(seqs, prm, pair_states(state_wkv[i]), row0=mp, t_len=dec_len,
                             chunk=dec_len, t_block=dec_len, out=y)
            wkv_p.append(unpair_states(sp))
            wkv_s.append(unpair_states(ss))
            x1, gate_l, idx_l = post(layer, y, x, rw_wo[i])
        else:
            j = layer - N_A_LAYERS
            q = dense(x, att_wq[j].astype(BF16), tm=dense_tile, tn=D_MODEL)
            parts = []
            for gi, (window, dil) in enumerate(GROUPS):
                assert window // dil == ATT_STEPS
                parts += list(attn_prompt_group(q, kv, group=gi, bsz=bp, seq_len=seq_len, dil=dil))
            y_s = attn_sample(q, kv, cache_k.reshape(bs, kv_buf, d), cache_v.reshape(bs, kv_buf, d),
                              row0=mp, bsz=bs, t_len=dec_len)
            x1, gate_l, idx_l = post(layer, (tuple(parts), y_s), x, att_wo[j])
        x = moe(layer, x1, gate_l, idx_l)
        if layer == N_A_LAYERS - 1:
            kv = dense(x, kv_w.astype(BF16), tm=dense_tile, tn=D_MODEL)

    buf_p = min(GROUPS[-1][0], seq_len)
    heads = lambda t, n: t.reshape(t.shape[0], n, N_HEADS, HEAD)
    tails = [kv[(b + 1) * seq_len - buf_p:(b + 1) * seq_len] for b in range(bp)]
    kv_s = kv[mp:].reshape(bs, dec_len, 2 * d)
    k_p_out = heads(jnp.stack([t[:, :d] for t in tails]), buf_p)
    v_p_out = heads(jnp.stack([t[:, d:] for t in tails]), buf_p)
    k_s_out = jnp.concatenate([cache_k, heads(kv_s[:, :, :d], dec_len)], axis=1)[:, -kv_buf:]
    v_s_out = jnp.concatenate([cache_v, heads(kv_s[:, :, d:], dec_len)], axis=1)[:, -kv_buf:]
    return (x[:mp].reshape(bp, seq_len, d), x[mp:].reshape(bs, dec_len, d),
            jnp.stack(wkv_p), jnp.stack(shift_p), k_p_out, v_p_out,
            jnp.stack(wkv_s), jnp.stack(shift_s), k_s_out, v_s_out)
```
